```python
import math
import jax, jax.numpy as jnp
from jax import lax
import numpy as np

D_MODEL = 2048
BATCH = 8
SEQ = 2048
DEPTH = 1
DEC_BATCH = 2
DEC_SEQ = 4096
PAST_LEN = 128

HEAD_DIM = 128
N_HEADS_A = 8
N_HEADS_B = 8
N_KV_B = 2
GQA_GROUP = N_HEADS_B // N_KV_B
WIDTH_A = N_HEADS_A * HEAD_DIM
WIDTH_B = N_HEADS_B * HEAD_DIM
KV_WIDTH_B = N_KV_B * HEAD_DIM
MIX_WIDTH = WIDTH_A + WIDTH_B
IN_COLS = 3 * WIDTH_A + WIDTH_B + 2 * KV_WIDTH_B
DIL_CONFIGS = ((128, 1), (512, 4), (2048, 16))
NUM_BUCKETS = 32
MAX_DISTANCE = 1024
GRID_W = 64
ROPE_THETA = 10000.0
Q_BLOCK = 128
N_EXPERTS = 16
EC_CAPACITY_FACTOR = 2
D_EXPERT = 2048
EPS = 1e-6
NEG_INF = -1e30

kernel_name = 'hybrid_dilated_gqa_ec_encoder'


def rms_norm(x, g):
    xf = x.astype(jnp.float32)
    y = xf * lax.rsqrt(jnp.mean(xf * xf, axis=-1, keepdims=True) + EPS)
    return (y * g.astype(jnp.float32)).astype(x.dtype)


def t5_bucket(rel):
    nb = NUM_BUCKETS // 2
    max_exact = nb // 2
    sign_off = jnp.where(rel > 0, nb, 0)
    n = jnp.abs(rel)
    nf = jnp.maximum(n, 1).astype(jnp.float32)
    large = max_exact + (jnp.log(nf / max_exact) / math.log(MAX_DISTANCE / max_exact)
                         * (nb - max_exact)).astype(jnp.int32)
    large = jnp.minimum(large, nb - 1)
    return sign_off + jnp.where(n < max_exact, n, large)


def dilated_branch(q, k, v, rel_bias_table, window, dil):
    B, T, H, Dh = q.shape
    r = window // (2 * dil)
    L = T // dil
    nb = -(-L // r)
    Lp = nb * r
    N = B * dil

    def to_sub(a):
        a = a.reshape(B, L, dil, H, Dh).transpose(0, 2, 1, 3, 4).reshape(N, L, H, Dh)
        return jnp.pad(a, ((0, 0), (0, Lp - L), (0, 0), (0, 0)))

    def windows(a):
        ap = jnp.pad(to_sub(a), ((0, 0), (r, r), (0, 0), (0, 0))).reshape(N, nb + 2, r, H, Dh)
        return jnp.concatenate([ap[:, :-2], ap[:, 1:-1], ap[:, 2:]], axis=2)

    qb = to_sub(q).reshape(N, nb, r, H, Dh)
    kw = windows(k)
    vw = windows(v)
    s = jnp.einsum('nbqhd,nbkhd->nbhqk', qb, kw,
                   preferred_element_type=jnp.float32) * (HEAD_DIM ** -0.5)
    qi = jnp.arange(r, dtype=jnp.int32)[:, None]
    kj = jnp.arange(3 * r, dtype=jnp.int32)[None, :]
    rel = kj - r - qi
    key_idx = jnp.arange(nb, dtype=jnp.int32)[:, None, None] * r + kj[None] - r
    valid = (jnp.abs(rel) <= r)[None] & (key_idx >= 0) & (key_idx < L)
    bias = rel_bias_table[t5_bucket(rel * dil)].astype(jnp.float32).transpose(2, 0, 1)
    s = jnp.where(valid[None, :, None], s + bias[None, None], NEG_INF)
    mx = jnp.max(s, axis=-1, keepdims=True)
    p = jnp.exp(s - mx)
    den = jnp.sum(p, axis=-1)
    o = jnp.einsum('nbhqk,nbkhd->nbqhd', p, vw.astype(jnp.float32))
    o = o / den.transpose(0, 1, 3, 2)[..., None]
    lse = (mx[..., 0] + jnp.log(den)).transpose(0, 1, 3, 2)
    o = o.reshape(N, Lp, H, Dh)[:, :L].reshape(B, dil, L, H, Dh).transpose(0, 2, 1, 3, 4).reshape(B, T, H, Dh)
    lse = lse.reshape(N, Lp, H)[:, :L].reshape(B, dil, L, H).transpose(0, 2, 1, 3).reshape(B, T, H)
    return o, lse


def dilated_mixture_attention(q, k, v, rel_bias_table):
    outs, lses = [], []
    for window, dil in DIL_CONFIGS:
        o, lse = dilated_branch(q, k, v, rel_bias_table, window, dil)
        outs.append(o)
        lses.append(lse)
    w = jax.nn.softmax(jnp.stack(lses, axis=0), axis=0)
    return jnp.sum(w[..., None] * jnp.stack(outs, axis=0), axis=0)


def axial_rope(x, row_ids, col_ids):
    half = HEAD_DIM // 2
    quarter = half // 2
    freqs = ROPE_THETA ** (-(jnp.arange(quarter, dtype=jnp.float32) / quarter))

    def rot(xp, pos):
        ang = pos.astype(jnp.float32)[:, None] * freqs[None, :]
        cos = jnp.cos(ang)[None, :, None, :]
        sin = jnp.sin(ang)[None, :, None, :]
        x1, x2 = xp[..., :quarter], xp[..., quarter:]
        return jnp.concatenate([x1 * cos - x2 * sin, x2 * cos + x1 * sin], axis=-1)

    xf = x.astype(jnp.float32)
    return jnp.concatenate([rot(xf[..., :half], row_ids), rot(xf[..., half:], col_ids)],
                           axis=-1).astype(x.dtype)


def gqa_block_attention(q, k, v):
    B, T, _, Dh = q.shape
    nq = T // Q_BLOCK
    qb = q.reshape(B, nq, Q_BLOCK, N_KV_B, GQA_GROUP, Dh).transpose(1, 0, 2, 3, 4, 5)
    vf = v.astype(jnp.float32)

    def one_block(qblk):
        s = jnp.einsum('bqhgd,bkhd->bhgqk', qblk, k,
                       preferred_element_type=jnp.float32) * (HEAD_DIM ** -0.5)
        p = jax.nn.softmax(s, axis=-1)
        return jnp.einsum('bhgqk,bkhd->bqhgd', p, vf)

    o = lax.map(one_block, qb)
    return o.transpose(1, 0, 2, 3, 4, 5).reshape(B, T, N_HEADS_B * Dh)


def token_mix(h, row_ids, col_ids, rel_bias_table, w_in, g_q_a, g_k_a, g_q_b, g_k_b,
              g_out_a, g_out_b, w_out):
    B, T, _ = h.shape
    proj = h @ w_in
    cuts = [WIDTH_A, 2 * WIDTH_A, 3 * WIDTH_A, 3 * WIDTH_A + WIDTH_B,
            3 * WIDTH_A + WIDTH_B + KV_WIDTH_B]
    qa, ka, va, qb, kb, vb = jnp.split(proj, cuts, axis=-1)
    qa = rms_norm(qa.reshape(B, T, N_HEADS_A, HEAD_DIM), g_q_a)
    ka = rms_norm(ka.reshape(B, T, N_HEADS_A, HEAD_DIM), g_k_a)
    va = va.reshape(B, T, N_HEADS_A, HEAD_DIM)
    oa = dilated_mixture_attention(qa, ka, va, rel_bias_table)
    qb = axial_rope(rms_norm(qb.reshape(B, T, N_HEADS_B, HEAD_DIM), g_q_b), row_ids, col_ids)
    kb = axial_rope(rms_norm(kb.reshape(B, T, N_KV_B, HEAD_DIM), g_k_b), row_ids, col_ids)
    vb = vb.reshape(B, T, N_KV_B, HEAD_DIM)
    ob = gqa_block_attention(qb, kb, vb)
    oa = rms_norm(oa.reshape(B, T, WIDTH_A).astype(h.dtype), g_out_a)
    ob = rms_norm(ob.astype(h.dtype), g_out_b)
    return jnp.concatenate([oa, ob], axis=-1) @ w_out


def expert_choice_ffn(h, w_router, w_gate, w_up, w_down):
    B, T, D = h.shape
    n = B * T
    hf = h.reshape(n, D)
    aff = jax.nn.softmax((hf @ w_router).astype(jnp.float32), axis=-1)
    cap = EC_CAPACITY_FACTOR * n // N_EXPERTS
    g, idx = lax.top_k(aff.T, cap)
    xe = hf[idx]
    a = jnp.einsum('ecd,edf->ecf', xe, w_gate)
    u = jnp.einsum('ecd,edf->ecf', xe, w_up)
    ye = jnp.einsum('ecf,efd->ecd', jax.nn.silu(a) * u, w_down) * g[..., None].astype(h.dtype)
    out = jnp.zeros_like(hf).at[idx.reshape(-1)].add(ye.reshape(-1, D))
    return out.reshape(B, T, D)


def encoder_layer(x, c, row_ids, col_ids, rel_bias_table, w_ada, b_ada, g_norm_mix, g_norm_ffn,
                  w_in, g_q_a, g_k_a, g_q_b, g_k_b, g_out_a, g_out_b, w_out,
                  w_router, w_gate, w_up, w_down):
    mod = jax.nn.silu(c) @ w_ada + b_ada
    sh1, sc1, gt1, sh2, sc2, gt2 = jnp.split(mod[:, None, :], 6, axis=-1)
    h = rms_norm(x, g_norm_mix) * (1.0 + sc1) + sh1
    x = x + gt1 * token_mix(h, row_ids, col_ids, rel_bias_table, w_in, g_q_a, g_k_a,
                            g_q_b, g_k_b, g_out_a, g_out_b, w_out)
    h = rms_norm(x, g_norm_ffn) * (1.0 + sc2) + sh2
    x = x + gt2 * expert_choice_ffn(h, w_router, w_gate, w_up, w_down)
    return x


def setup_inputs(seed: int = 0) -> dict:
    key = jax.random.key(seed)
    ks = jax.random.split(key, 24)
    f32 = jnp.float32
    D = D_MODEL

    def nrm(k, shape, scale):
        return jax.random.normal(k, shape, f32) * scale

    def gain(k, shape):
        return 1.0 + 0.05 * jax.random.normal(k, shape, f32)

    return {
        'x_prompt': nrm(ks[0], (BATCH, SEQ, D), 1.0),
        'x_sample': nrm(ks[1], (DEC_BATCH, DEC_SEQ, D), 1.0),
        'c_prompt': nrm(ks[2], (BATCH, D), 1.0),
        'c_sample': nrm(ks[3], (DEC_BATCH, D), 1.0),
        'rel_bias_table': nrm(ks[4], (NUM_BUCKETS, N_HEADS_A), 0.5),
        'w_ada': nrm(ks[5], (DEPTH, D, 6 * D), 0.5 * D ** -0.5),
        'b_ada': nrm(ks[6], (DEPTH, 6 * D), 0.02),
        'g_norm_mix': gain(ks[7], (DEPTH, D)),
        'g_norm_ffn': gain(ks[8], (DEPTH, D)),
        'w_in': nrm(ks[9], (DEPTH, D, IN_COLS), D ** -0.5),
        'g_q_a': gain(ks[10], (DEPTH, HEAD_DIM)),
        'g_k_a': gain(ks[11], (DEPTH, HEAD_DIM)),
        'g_q_b': gain(ks[12], (DEPTH, HEAD_DIM)),
        'g_k_b': gain(ks[13], (DEPTH, HEAD_DIM)),
        'g_out_a': gain(ks[14], (DEPTH, WIDTH_A)),
        'g_out_b': gain(ks[15], (DEPTH, WIDTH_B)),
        'w_out': nrm(ks[16], (DEPTH, MIX_WIDTH, D), MIX_WIDTH ** -0.5),
        'w_router': nrm(ks[17], (DEPTH, D, N_EXPERTS), D ** -0.5),
        'w_gate': nrm(ks[18], (DEPTH, N_EXPERTS, D, D_EXPERT), D ** -0.5),
        'w_up': nrm(ks[19], (DEPTH, N_EXPERTS, D, D_EXPERT), D ** -0.5),
        'w_down': nrm(ks[20], (DEPTH, N_EXPERTS, D_EXPERT, D), D_EXPERT ** -0.5),
    }


def reference(x_prompt, x_sample, c_prompt, c_sample, rel_bias_table, w_ada, b_ada,
              g_norm_mix, g_norm_ffn, w_in, g_q_a, g_k_a, g_q_b, g_k_b, g_out_a, g_out_b,
              w_out, w_router, w_gate, w_up, w_down):
    def run(x, c):
        rows = x.shape[1] // GRID_W
        row_ids = jnp.repeat(jnp.arange(rows, dtype=jnp.int32), GRID_W)
        col_ids = jnp.tile(jnp.arange(GRID_W, dtype=jnp.int32), rows)
        for l in range(DEPTH):
            x = encoder_layer(x, c, row_ids, col_ids, rel_bias_table, w_ada[l], b_ada[l],
                              g_norm_mix[l], g_norm_ffn[l], w_in[l], g_q_a[l], g_k_a[l],
                              g_q_b[l], g_k_b[l], g_out_a[l], g_out_b[l], w_out[l],
                              w_router[l], w_gate[l], w_up[l], w_down[l])
        return x

    y_prompt = run(x_prompt, c_prompt)
    y_sample = run(x_sample, c_sample)
    return (y_prompt, y_sample)
```

```python
import functools
import math

import jax
import jax.numpy as jnp
from jax import lax
from jax.experimental import pallas as pl
from jax.experimental.pallas import tpu as pltpu

F32 = jnp.float32
BF16 = jnp.bfloat16

D_MODEL = 2048
HEAD_DIM = 128
N_HEADS_A = 8
N_HEADS_B = 8
N_KV_B = 2
GQA_GROUP = N_HEADS_B // N_KV_B
WIDTH_A = N_HEADS_A * HEAD_DIM
WIDTH_B = N_HEADS_B * HEAD_DIM
KV_WIDTH_B = N_KV_B * HEAD_DIM
IN_COLS = 3 * WIDTH_A + WIDTH_B + 2 * KV_WIDTH_B
DIL_CONFIGS = ((128, 1), (512, 4), (2048, 16))
NUM_BUCKETS = 32
MAX_DISTANCE = 1024
GRID_W = 64
ROPE_THETA = 10000.0
N_EXPERTS = 16
EC_CAPACITY_FACTOR = 2
D_EXPERT = 2048
EPS = 1e-6
NEG_INF = -1e30

VMEM_LIMIT_V7X = 56 * 1024 * 1024

COL_QA, COL_KA, COL_VA = 0, 8, 16
COL_QB, COL_KB, COL_VB = 24, 32, 34

IN_TN = 512
A_QB = 128
A_KB = 256
A_RADIUS = 64
A_PAD = A_RADIUS * 16


def _cparams(sem):
    return pltpu.CompilerParams(dimension_semantics=sem, vmem_limit_bytes=VMEM_LIMIT_V7X)


def _ada_kernel(c_ref, w_ref, b_ref, o_ref):
    c = c_ref[...]
    s = c * (1.0 / (1.0 + jnp.exp(-c)))
    o_ref[...] = jnp.dot(s.astype(BF16), w_ref[...].astype(BF16),
                         preferred_element_type=F32) + b_ref[...]


def _ada(c_all, w_ada, b_ada):
    rows, d = c_all.shape
    n = w_ada.shape[1]
    tn = 1024
    return pl.pallas_call(
        _ada_kernel,
        grid=(n // tn,),
        in_specs=[pl.BlockSpec((rows, d), lambda j: (0, 0)),
                  pl.BlockSpec((d, tn), lambda j: (0, j)),
                  pl.BlockSpec((1, tn), lambda j: (0, j))],
        out_specs=pl.BlockSpec((rows, tn), lambda j: (0, j)),
        out_shape=jax.ShapeDtypeStruct((rows, n), F32),
        compiler_params=_cparams(("arbitrary",)),
        name="ada_mod",
    )(c_all, w_ada, b_ada.reshape(1, n))


def _head_norm(a, g):
    ms = jnp.mean(a * a, axis=-1, keepdims=True)
    return a * lax.rsqrt(ms + EPS) * g


def _inproj_kernel(x_ref, sc_ref, sh_ref, gn_ref, w_ref, gains_ref, cos_ref, sa_ref, sb_ref,
                   o_ref, h_scr):
    j = pl.program_id(2)

    @pl.when(j == 0)
    def _():
        x = x_ref[0]
        ms = jnp.mean(x * x, axis=-1, keepdims=True)
        y = x * lax.rsqrt(ms + EPS) * gn_ref[...]
        h_scr[...] = (y * (1.0 + sc_ref[0]) + sh_ref[0]).astype(BF16)

    acc = jnp.dot(h_scr[...], w_ref[...], preferred_element_type=F32)

    def rope(a):
        return (a * cos_ref[...] + pltpu.roll(a, 96, 1) * sa_ref[...]
                + pltpu.roll(a, 32, 1) * sb_ref[...])

    def store_heads(fn, first, count):
        for hh in range(first, first + count):
            sl = slice(hh * HEAD_DIM, (hh + 1) * HEAD_DIM)
            o_ref[0, :, sl] = fn(acc[:, sl]).astype(BF16)

    @pl.when(j < 2)
    def _():
        store_heads(lambda a: _head_norm(a, gains_ref[0:1, :]), 0, 4)

    @pl.when((j >= 2) & (j < 4))
    def _():
        store_heads(lambda a: _head_norm(a, gains_ref[1:2, :]), 0, 4)

    @pl.when((j >= 4) & (j < 6))
    def _():
        o_ref[0] = acc.astype(BF16)

    @pl.when((j >= 6) & (j < 8))
    def _():
        store_heads(lambda a: rope(_head_norm(a, gains_ref[2:3, :])), 0, 4)

    @pl.when(j == 8)
    def _():
        store_heads(lambda a: rope(_head_norm(a, gains_ref[3:4, :])), 0, 2)
        store_heads(lambda a: a, 2, 2)


def _inproj(x, mod3, boff, g_norm, w_in_bf, gains, cos_t, sa_t, sb_t):
    b, t, d = x.shape
    tm = 1024
    nj = IN_COLS // IN_TN
    return pl.pallas_call(
        _inproj_kernel,
        grid=(b, t // tm, nj),
        in_specs=[
            pl.BlockSpec((1, tm, d), lambda bi, ti, j: (bi, ti, 0)),
            pl.BlockSpec((1, 1, d), lambda bi, ti, j: ((bi + boff) * 6 + 1, 0, 0)),
            pl.BlockSpec((1, 1, d), lambda bi, ti, j: ((bi + boff) * 6 + 0, 0, 0)),
            pl.BlockSpec((1, d), lambda bi, ti, j: (0, 0)),
            pl.BlockSpec((d, IN_TN), lambda bi, ti, j: (0, j)),
            pl.BlockSpec((4, HEAD_DIM), lambda bi, ti, j: (0, 0)),
            pl.BlockSpec((tm, HEAD_DIM), lambda bi, ti, j: (ti, 0)),
            pl.BlockSpec((tm, HEAD_DIM), lambda bi, ti, j: (ti, 0)),
            pl.BlockSpec((tm, HEAD_DIM), lambda bi, ti, j: (ti, 0)),
        ],
        out_specs=pl.BlockSpec((1, tm, IN_TN), lambda bi, ti, j: (bi, ti, j)),
        out_shape=jax.ShapeDtypeStruct((b, t, IN_COLS), BF16),
        scratch_shapes=[pltpu.VMEM((tm, d), BF16)],
        compiler_params=_cparams(("arbitrary", "arbitrary", "arbitrary")),
        name="inproj",
    )(x, mod3, mod3, g_norm, w_in_bf, gains, cos_t, sa_t, sb_t)


def _attn_a_kernel(q_ref, k_ref, v_ref, bias_ref, o_ref, qf, kf, vf, acc, mm, ll, *, t):
    zpad = jnp.zeros((A_PAD, HEAD_DIM), F32)
    kf[0:A_PAD, :] = zpad
    vf[0:A_PAD, :] = zpad
    kf[A_PAD + t:A_PAD + t + A_PAD, :] = zpad
    vf[A_PAD + t:A_PAD + t + A_PAD, :] = zpad
    kf[A_PAD:A_PAD + t, :] = k_ref[0].astype(F32)
    vf[A_PAD:A_PAD + t, :] = v_ref[0].astype(F32)
    qf[...] = q_ref[0].astype(F32)
    ones = jnp.ones((A_KB, HEAD_DIM), BF16)

    for bi, (_, dil) in enumerate(DIL_CONFIGS):
        sub_len = t // dil
        nmb = sub_len // A_QB
        shift = int(math.log2(nmb))

        def rows(start, size, dil=dil):
            return pl.ds(start, size) if dil == 1 else pl.ds(start, size, stride=dil)

        def body(idx, carry, bi=bi, dil=dil, sub_len=sub_len, nmb=nmb, shift=shift, rows=rows):
            rho = lax.shift_right_logical(idx, shift)
            mb = lax.bitwise_and(idx, nmb - 1)
            qstart = rho + mb * (A_QB * dil)
            kstart = A_PAD + qstart - A_RADIUS * dil
            q = qf[rows(qstart, A_QB), :].astype(BF16)
            k = kf[rows(kstart, A_KB), :].astype(BF16)
            v = vf[rows(kstart, A_KB), :].astype(BF16)
            s = lax.dot_general(q, k, (((1,), (1,)), ((), ())), preferred_element_type=F32)
            s = s + bias_ref[bi, 0]
            kidx = mb * A_QB - A_RADIUS + lax.broadcasted_iota(jnp.int32, (1, A_KB), 1)
            s = jnp.where((kidx >= 0) & (kidx < sub_len), s, NEG_INF)
            mblk = jnp.max(s, axis=-1, keepdims=True)
            v1 = jnp.concatenate([v, ones], axis=1)
            if bi == 0:
                p = jnp.exp(s - mblk).astype(BF16)
                pv = jnp.dot(p, v1, preferred_element_type=F32)
                acc[rows(qstart, A_QB), :] = pv[:, :HEAD_DIM]
                ll[rows(qstart, A_QB), :] = pv[:, HEAD_DIM:]
                mm[rows(qstart, A_QB), :] = jnp.broadcast_to(mblk, (A_QB, HEAD_DIM))
            else:
                mold = mm[rows(qstart, A_QB), :]
                mnew = jnp.maximum(mold, mblk)
                alpha = jnp.exp(mold - mnew)
                p = jnp.exp(s - mnew[:, 0:1]).astype(BF16)
                pv = jnp.dot(p, v1, preferred_element_type=F32)
                acc[rows(qstart, A_QB), :] = alpha * acc[rows(qstart, A_QB), :] + pv[:, :HEAD_DIM]
                ll[rows(qstart, A_QB), :] = alpha * ll[rows(qstart, A_QB), :] + pv[:, HEAD_DIM:]
                mm[rows(qstart, A_QB), :] = mnew
            return carry

        lax.fori_loop(0, dil * nmb, body, 0)

    o_ref[0] = (acc[...] / ll[...]).astype(BF16)


def _attn_a(proj, bias_tiles):
    b, t, _ = proj.shape
    kern = functools.partial(_attn_a_kernel, t=t)
    return pl.pallas_call(
        kern,
        grid=(b, N_HEADS_A),
        in_specs=[
            pl.BlockSpec((1, t, HEAD_DIM), lambda bi, h: (bi, 0, COL_QA + h)),
            pl.BlockSpec((1, t, HEAD_DIM), lambda bi, h: (bi, 0, COL_KA + h)),
            pl.BlockSpec((1, t, HEAD_DIM), lambda bi, h: (bi, 0, COL_VA + h)),
            pl.BlockSpec((3, 1, A_QB, A_KB), lambda bi, h: (0, h, 0, 0)),
        ],
        out_specs=pl.BlockSpec((1, t, HEAD_DIM), lambda bi, h: (bi, 0, h)),
        out_shape=jax.ShapeDtypeStruct((b, t, WIDTH_A), BF16),
        scratch_shapes=[
            pltpu.VMEM((t, HEAD_DIM), F32),
            pltpu.VMEM((t + 2 * A_PAD, HEAD_DIM), F32),
            pltpu.VMEM((t + 2 * A_PAD, HEAD_DIM), F32),
            pltpu.VMEM((t, HEAD_DIM), F32),
            pltpu.VMEM((t, HEAD_DIM), F32),
            pltpu.VMEM((t, HEAD_DIM), F32),
        ],
        compiler_params=_cparams(("arbitrary", "arbitrary")),
        name="attn_dilated",
    )(proj, proj, proj, bias_tiles)


def _t5_bucket(rel):
    nb = NUM_BUCKETS // 2
    max_exact = nb // 2
    sign_off = jnp.where(rel > 0, nb, 0)
    n = jnp.abs(rel)
    nf = jnp.maximum(n, 1).astype(F32)
    large = max_exact + (jnp.log(nf / max_exact) / math.log(MAX_DISTANCE / max_exact)
                         * (nb - max_exact)).astype(jnp.int32)
    large = jnp.minimum(large, nb - 1)
    return sign_off + jnp.where(n < max_exact, n, large)


def _bias_tiles(rel_bias_table):
    qi = jnp.arange(A_QB, dtype=jnp.int32)[:, None]
    kj = jnp.arange(A_KB, dtype=jnp.int32)[None, :]
    rel = kj - A_RADIUS - qi
    tiles = []
    for _, dil in DIL_CONFIGS:
        bias = rel_bias_table[_t5_bucket(rel * dil)].astype(F32).transpose(2, 0, 1)
        tiles.append(jnp.where((jnp.abs(rel) <= A_RADIUS)[None], bias, NEG_INF))
    return jnp.stack(tiles, axis=0)


def _attn_b_kernel(q_ref, k_ref, v_ref, o_ref, v1_scr, *, tq):
    @pl.when(pl.program_id(2) == 0)
    def _():
        v1_scr[:, 0:HEAD_DIM] = v_ref[0]
        v1_scr[:, HEAD_DIM:2 * HEAD_DIM] = jnp.ones(v_ref.shape[1:], BF16)

    q = q_ref[0]
    qs = jnp.concatenate([q[:, i * HEAD_DIM:(i + 1) * HEAD_DIM] for i in range(GQA_GROUP)], axis=0)
    s = lax.dot_general(qs, k_ref[0], (((1,), (1,)), ((), ())), preferred_element_type=F32)
    m = jnp.max(s, axis=-1, keepdims=True)
    p = jnp.exp(s - m).astype(BF16)
    pv = jnp.dot(p, v1_scr[...], preferred_element_type=F32)
    o = pv[:, :HEAD_DIM] / pv[:, HEAD_DIM:]
    o_ref[0] = jnp.concatenate([o[i * tq:(i + 1) * tq] for i in range(GQA_GROUP)],
                               axis=1).astype(BF16)


def _attn_b(proj):
    b, t, _ = proj.shape
    tq = 128
    gw = GQA_GROUP * HEAD_DIM
    return pl.pallas_call(
        functools.partial(_attn_b_kernel, tq=tq),
        grid=(b, N_KV_B, t // tq),
        in_specs=[
            pl.BlockSpec((1, tq, gw), lambda bi, g, qi: (bi, qi, COL_QB // GQA_GROUP + g)),
            pl.BlockSpec((1, t, HEAD_DIM), lambda bi, g, qi: (bi, 0, COL_KB + g)),
            pl.BlockSpec((1, t, HEAD_DIM), lambda bi, g, qi: (bi, 0, COL_VB + g)),
        ],
        out_specs=pl.BlockSpec((1, tq, gw), lambda bi, g, qi: (bi, qi, g)),
        out_shape=jax.ShapeDtypeStruct((b, t, WIDTH_B), BF16),
        scratch_shapes=[pltpu.VMEM((t, 2 * HEAD_DIM), BF16)],
        compiler_params=_cparams(("arbitrary", "arbitrary", "arbitrary")),
        name="attn_gqa",
    )(proj, proj, proj)


def _outproj_kernel(oa_ref, ob_ref, x_ref, gt_ref, sc_ref, sh_ref, ga_ref, gb_ref, gn_ref,
                    w_ref, wr_ref, x1_ref, h2_ref, lg_ref):
    def wide_norm(o_ref_, g_ref_):
        o = o_ref_[0].astype(F32)
        ms = jnp.mean(o * o, axis=-1, keepdims=True)
        return (o * lax.rsqrt(ms + EPS) * g_ref_[...]).astype(BF16)

    na = wide_norm(oa_ref, ga_ref)
    nb = wide_norm(ob_ref, gb_ref)
    mix = (jnp.dot(na, w_ref[0:WIDTH_A, :], preferred_element_type=F32)
           + jnp.dot(nb, w_ref[WIDTH_A:WIDTH_A + WIDTH_B, :], preferred_element_type=F32))
    x1 = x_ref[0] + gt_ref[0] * mix
    x1_ref[0] = x1
    ms = jnp.mean(x1 * x1, axis=-1, keepdims=True)
    h2 = (x1 * lax.rsqrt(ms + EPS) * gn_ref[...]) * (1.0 + sc_ref[0]) + sh_ref[0]
    h2b = h2.astype(BF16)
    h2_ref[0] = h2b
    lg_ref[0] = jnp.dot(h2b, wr_ref[...], preferred_element_type=F32)


def _outproj(oa, ob, x, mod3, boff, g_out_a, g_out_b, g_norm_ffn, w_out_bf, w_router_pad):
    b, t, d = x.shape
    tm = 512
    nr = w_router_pad.shape[1]
    row = lambda k: (lambda bi, ti: ((bi + boff) * 6 + k, 0, 0))
    return pl.pallas_call(
        _outproj_kernel,
        grid=(b, t // tm),
        in_specs=[
            pl.BlockSpec((1, tm, WIDTH_A), lambda bi, ti: (bi, ti, 0)),
            pl.BlockSpec((1, tm, WIDTH_B), lambda bi, ti: (bi, ti, 0)),
            pl.BlockSpec((1, tm, d), lambda bi, ti: (bi, ti, 0)),
            pl.BlockSpec((1, 1, d), row(2)),
            pl.BlockSpec((1, 1, d), row(4)),
            pl.BlockSpec((1, 1, d), row(3)),
            pl.BlockSpec((1, WIDTH_A), lambda bi, ti: (0, 0)),
            pl.BlockSpec((1, WIDTH_B), lambda bi, ti: (0, 0)),
            pl.BlockSpec((1, d), lambda bi, ti: (0, 0)),
            pl.BlockSpec((WIDTH_A + WIDTH_B, d), lambda bi, ti: (0, 0)),
            pl.BlockSpec((d, nr), lambda bi, ti: (0, 0)),
        ],
        out_specs=[
            pl.BlockSpec((1, tm, d), lambda bi, ti: (bi, ti, 0)),
            pl.BlockSpec((1, tm, d), lambda bi, ti: (bi, ti, 0)),
            pl.BlockSpec((1, tm, nr), lambda bi, ti: (bi, ti, 0)),
        ],
        out_shape=[jax.ShapeDtypeStruct((b, t, d), F32),
                   jax.ShapeDtypeStruct((b, t, d), BF16),
                   jax.ShapeDtypeStruct((b, t, nr), F32)],
        compiler_params=_cparams(("arbitrary", "arbitrary")),
        name="outproj",
    )(oa, ob, x, mod3, mod3, mod3, g_out_a, g_out_b, g_norm_ffn, w_out_bf, w_router_pad)


def _ffn_kernel(x_ref, wg_ref, wu_ref, wd_ref, g_ref, o_ref):
    f = pl.program_id(2)
    x = x_ref[0]
    a = jnp.dot(x, wg_ref[0], preferred_element_type=F32)
    u = jnp.dot(x, wu_ref[0], preferred_element_type=F32)
    hmid = (a * (1.0 / (1.0 + jnp.exp(-a))) * u).astype(BF16)
    y = jnp.dot(hmid, wd_ref[0], preferred_element_type=F32)

    @pl.when(f == 0)
    def _():
        o_ref[0] = y

    @pl.when(f > 0)
    def _():
        o_ref[0] += y

    @pl.when(f == pl.num_programs(2) - 1)
    def _():
        o_ref[0] = o_ref[0] * g_ref[0]


def _ffn(xe, wg, wu, wd, g):
    e, cap, d = xe.shape
    fdim = wg.shape[2]
    tm = min(cap, 1024)
    tf = 512
    return pl.pallas_call(
        _ffn_kernel,
        grid=(e, cap // tm, fdim // tf),
        in_specs=[
            pl.BlockSpec((1, tm, d), lambda ei, mi, fi: (ei, mi, 0)),
            pl.BlockSpec((1, d, tf), lambda ei, mi, fi: (ei, 0, fi)),
            pl.BlockSpec((1, d, tf), lambda ei, mi, fi: (ei, 0, fi)),
            pl.BlockSpec((1, tf, d), lambda ei, mi, fi: (ei, fi, 0)),
            pl.BlockSpec((1, tm, 1), lambda ei, mi, fi: (ei, mi, 0)),
        ],
        out_specs=pl.BlockSpec((1, tm, d), lambda ei, mi, fi: (ei, mi, 0)),
        out_shape=jax.ShapeDtypeStruct((e, cap, d), F32),
        compiler_params=_cparams(("arbitrary", "arbitrary", "arbitrary")),
        name="expert_ffn",
    )(xe, wg, wu, wd, g.reshape(e, cap, 1))


def _rope_tables(t):
    half = HEAD_DIM // 2
    quarter = half // 2
    freqs = ROPE_THETA ** (-(jnp.arange(quarter, dtype=F32) / quarter))
    rows = t // GRID_W
    row_ids = jnp.repeat(jnp.arange(rows, dtype=jnp.int32), GRID_W).astype(F32)
    col_ids = jnp.tile(jnp.arange(GRID_W, dtype=jnp.int32), rows).astype(F32)
    ang_r = row_ids[:, None] * freqs[None, :]
    ang_c = col_ids[:, None] * freqs[None, :]
    z = jnp.zeros_like(ang_r)
    cos_t = jnp.concatenate([jnp.cos(ang_r)] * 2 + [jnp.cos(ang_c)] * 2, axis=1)
    sa_t = jnp.concatenate([-jnp.sin(ang_r), z, -jnp.sin(ang_c), z], axis=1)
    sb_t = jnp.concatenate([z, jnp.sin(ang_r), z, jnp.sin(ang_c)], axis=1)
    return cos_t, sa_t, sb_t


def kernel(x_prompt, x_sample, c_prompt, c_sample, rel_bias_table, w_ada, b_ada, g_norm_mix, g_norm_ffn, w_in, g_q_a, g_k_a, g_q_b, g_k_b, g_out_a, g_out_b, w_out, w_router, w_gate, w_up, w_down):
    d = D_MODEL
    nbp, nbs = c_prompt.shape[0], c_sample.shape[0]
    c_all = jnp.concatenate([c_prompt, c_sample, jnp.zeros((16 - nbp - nbs, d), F32)], axis=0)
    mod = _ada(c_all, w_ada[0], b_ada[0])
    mod3 = mod.reshape(16 * 6, 1, d)

    scale = HEAD_DIM ** -0.5
    gains = jnp.stack([g_q_a[0] * scale, g_k_a[0], g_q_b[0] * scale, g_k_b[0]], axis=0)
    w_in_bf = w_in[0].astype(BF16)
    w_out_bf = w_out[0].astype(BF16)
    w_router_pad = jnp.pad(w_router[0], ((0, 0), (0, HEAD_DIM - N_EXPERTS))).astype(BF16)
    wg, wu, wd = w_gate[0].astype(BF16), w_up[0].astype(BF16), w_down[0].astype(BF16)
    bias_tiles = _bias_tiles(rel_bias_table)
    gn1 = g_norm_mix[0].reshape(1, d)
    gn2 = g_norm_ffn[0].reshape(1, d)
    ga = g_out_a[0].reshape(1, WIDTH_A)
    gb = g_out_b[0].reshape(1, WIDTH_B)

    def run(x, boff):
        b, t, _ = x.shape
        cos_t, sa_t, sb_t = _rope_tables(t)
        proj = _inproj(x, mod3, boff, gn1, w_in_bf, gains, cos_t, sa_t, sb_t)
        oa = _attn_a(proj, bias_tiles)
        ob = _attn_b(proj)
        x1, h2, logits = _outproj(oa, ob, x, mod3, boff, ga, gb, gn2, w_out_bf, w_router_pad)
        n = b * t
        aff = jax.nn.softmax(logits.reshape(n, -1)[:, :N_EXPERTS], axis=-1)
        cap = EC_CAPACITY_FACTOR * n // N_EXPERTS
        g, idx = lax.top_k(aff.T, cap)
        xe = h2.reshape(n, d)[idx]
        ye = _ffn(xe, wg, wu, wd, g)
        ffn = jnp.zeros((n, d), F32).at[idx.reshape(-1)].add(ye.reshape(-1, d))
        gt2 = mod[boff:boff + b, 5 * d:6 * d][:, None, :]
        return x1 + gt2 * ffn.reshape(b, t, d)

    return (run(x_prompt, 0), run(x_sample, nbp))
```

```python
import functools
import math

import jax
import jax.numpy as jnp
from jax import lax
from jax.experimental import pallas as pl
from jax.experimental.pallas import tpu as pltpu

F32 = jnp.float32
BF16 = jnp.bfloat16

D_MODEL = 2048
HEAD_DIM = 128
N_HEADS_A = 8
N_HEADS_B = 8
N_KV_B = 2
GQA_GROUP = N_HEADS_B // N_KV_B
WIDTH_A = N_HEADS_A * HEAD_DIM
WIDTH_B = N_HEADS_B * HEAD_DIM
KV_WIDTH_B = N_KV_B * HEAD_DIM
IN_COLS = 3 * WIDTH_A + WIDTH_B + 2 * KV_WIDTH_B
DIL_CONFIGS = ((128, 1), (512, 4), (2048, 16))
NUM_BUCKETS = 32
MAX_DISTANCE = 1024
GRID_W = 64
ROPE_THETA = 10000.0
N_EXPERTS = 16
EC_CAPACITY_FACTOR = 2
D_EXPERT = 2048
EPS = 1e-6
NEG_INF = -1e30

VMEM_LIMIT_V7X = 56 * 1024 * 1024

COL_QA, COL_KA, COL_VA = 0, 8, 16
COL_QB, COL_KB, COL_VB = 24, 32, 34

IN_TN = 512
A_QB = 128
A_KB = 256
A_RADIUS = 64
A_PAD = A_RADIUS * 16


def _cparams(sem):
    return pltpu.CompilerParams(dimension_semantics=sem, vmem_limit_bytes=VMEM_LIMIT_V7X)


def _ada_kernel(c_ref, w_ref, b_ref, o_ref):
    c = c_ref[...]
    s = c * (1.0 / (1.0 + jnp.exp(-c)))
    o_ref[...] = jnp.dot(s.astype(BF16), w_ref[...].astype(BF16),
                         preferred_element_type=F32) + b_ref[...]


def _ada(c_all, w_ada, b_ada):
    rows, d = c_all.shape
    n = w_ada.shape[1]
    tn = 1024
    return pl.pallas_call(
        _ada_kernel,
        grid=(n // tn,),
        in_specs=[pl.BlockSpec((rows, d), lambda j: (0, 0)),
                  pl.BlockSpec((d, tn), lambda j: (0, j)),
                  pl.BlockSpec((1, tn), lambda j: (0, j))],
        out_specs=pl.BlockSpec((rows, tn), lambda j: (0, j)),
        out_shape=jax.ShapeDtypeStruct((rows, n), F32),
        compiler_params=_cparams(("arbitrary",)),
        name="ada_mod",
    )(c_all, w_ada, b_ada.reshape(1, n))


def _head_norm(a, g):
    ms = jnp.mean(a * a, axis=-1, keepdims=True)
    return a * lax.rsqrt(ms + EPS) * g


def _inproj_kernel(x_ref, sc_ref, sh_ref, gn_ref, w_ref, gains_ref, cos_ref, sa_ref, sb_ref,
                   o_ref, h_scr):
    j = pl.program_id(2)

    @pl.when(j == 0)
    def _():
        x = x_ref[0]
        ms = jnp.mean(x * x, axis=-1, keepdims=True)
        y = x * lax.rsqrt(ms + EPS) * gn_ref[...]
        h_scr[...] = (y * (1.0 + sc_ref[0]) + sh_ref[0]).astype(BF16)

    acc = jnp.dot(h_scr[...], w_ref[...], preferred_element_type=F32)

    def rope(a):
        return (a * cos_ref[...] + pltpu.roll(a, 96, 1) * sa_ref[...]
                + pltpu.roll(a, 32, 1) * sb_ref[...])

    def store_heads(fn, first, count):
        for hh in range(first, first + count):
            sl = slice(hh * HEAD_DIM, (hh + 1) * HEAD_DIM)
            o_ref[0, :, sl] = fn(acc[:, sl]).astype(BF16)

    @pl.when(j < 2)
    def _():
        store_heads(lambda a: _head_norm(a, gains_ref[0:1, :]), 0, 4)

    @pl.when((j >= 2) & (j < 4))
    def _():
        store_heads(lambda a: _head_norm(a, gains_ref[1:2, :]), 0, 4)

    @pl.when((j >= 4) & (j < 6))
    def _():
        o_ref[0] = acc.astype(BF16)

    @pl.when((j >= 6) & (j < 8))
    def _():
        store_heads(lambda a: rope(_head_norm(a, gains_ref[2:3, :])), 0, 4)

    @pl.when(j == 8)
    def _():
        store_heads(lambda a: rope(_head_norm(a, gains_ref[3:4, :])), 0, 2)
        store_heads(lambda a: a, 2, 2)


def _inproj(x, mod3, boff, g_norm, w_in_bf, gains, cos_t, sa_t, sb_t):
    b, t, d = x.shape
    tm = 1024
    nj = IN_COLS // IN_TN
    return pl.pallas_call(
        _inproj_kernel,
        grid=(b, t // tm, nj),
        in_specs=[
            pl.BlockSpec((1, tm, d), lambda bi, ti, j: (bi, ti, 0)),
            pl.BlockSpec((1, 1, d), lambda bi, ti, j: ((bi + boff) * 6 + 1, 0, 0)),
            pl.BlockSpec((1, 1, d), lambda bi, ti, j: ((bi + boff) * 6 + 0, 0, 0)),
            pl.BlockSpec((1, d), lambda bi, ti, j: (0, 0)),
            pl.BlockSpec((d, IN_TN), lambda bi, ti, j: (0, j)),
            pl.BlockSpec((4, HEAD_DIM), lambda bi, ti, j: (0, 0)),
            pl.BlockSpec((tm, HEAD_DIM), lambda bi, ti, j: (ti, 0)),
            pl.BlockSpec((tm, HEAD_DIM), lambda bi, ti, j: (ti, 0)),
            pl.BlockSpec((tm, HEAD_DIM), lambda bi, ti, j: (ti, 0)),
        ],
        out_specs=pl.BlockSpec((1, tm, IN_TN), lambda bi, ti, j: (bi, ti, j)),
        out_shape=jax.ShapeDtypeStruct((b, t, IN_COLS), BF16),
        scratch_shapes=[pltpu.VMEM((tm, d), BF16)],
        compiler_params=_cparams(("arbitrary", "arbitrary", "arbitrary")),
        name="inproj",
    )(x, mod3, mod3, g_norm, w_in_bf, gains, cos_t, sa_t, sb_t)


def _attn_a_kernel(q_ref, k_ref, v_ref, bias_ref, o_ref, qf, kf, vf, acc, mm, ll, *, t):
    zpad = jnp.zeros((A_PAD, HEAD_DIM), F32)
    kf[0:A_PAD, :] = zpad
    vf[0:A_PAD, :] = zpad
    kf[A_PAD + t:A_PAD + t + A_PAD, :] = zpad
    vf[A_PAD + t:A_PAD + t + A_PAD, :] = zpad
    kf[A_PAD:A_PAD + t, :] = k_ref[0].astype(F32)
    vf[A_PAD:A_PAD + t, :] = v_ref[0].astype(F32)
    qf[...] = q_ref[0].astype(F32)
    ones = jnp.ones((A_KB, HEAD_DIM), BF16)

    for bi, (_, dil) in enumerate(DIL_CONFIGS):
        sub_len = t // dil
        nmb = sub_len // A_QB
        shift = int(math.log2(nmb))

        def rows(start, size, dil=dil):
            return pl.ds(start, size) if dil == 1 else pl.ds(start, size, stride=dil)

        def body(idx, carry, bi=bi, dil=dil, sub_len=sub_len, nmb=nmb, shift=shift, rows=rows):
            rho = lax.shift_right_logical(idx, shift)
            mb = lax.bitwise_and(idx, nmb - 1)
            qstart = rho + mb * (A_QB * dil)
            kstart = A_PAD + qstart - A_RADIUS * dil
            q = qf[rows(qstart, A_QB), :].astype(BF16)
            k = kf[rows(kstart, A_KB), :].astype(BF16)
            v = vf[rows(kstart, A_KB), :].astype(BF16)
            s = lax.dot_general(q, k, (((1,), (1,)), ((), ())), preferred_element_type=F32)
            s = s + bias_ref[bi, 0]
            kidx = mb * A_QB - A_RADIUS + lax.broadcasted_iota(jnp.int32, (1, A_KB), 1)
            s = jnp.where((kidx >= 0) & (kidx < sub_len), s, NEG_INF)
            mblk = jnp.max(s, axis=-1, keepdims=True)
            v1 = jnp.concatenate([v, ones], axis=1)
            if bi == 0:
                p = jnp.exp(s - mblk).astype(BF16)
                pv = jnp.dot(p, v1, preferred_element_type=F32)
                acc[rows(qstart, A_QB), :] = pv[:, :HEAD_DIM]
                ll[rows(qstart, A_QB), :] = pv[:, HEAD_DIM:]
                mm[rows(qstart, A_QB), :] = jnp.broadcast_to(mblk, (A_QB, HEAD_DIM))
            else:
                mold = mm[rows(qstart, A_QB), :]
                mnew = jnp.maximum(mold, mblk)
                alpha = jnp.exp(mold - mnew)
                p = jnp.exp(s - mnew[:, 0:1]).astype(BF16)
                pv = jnp.dot(p, v1, preferred_element_type=F32)
                acc[rows(qstart, A_QB), :] = alpha * acc[rows(qstart, A_QB), :] + pv[:, :HEAD_DIM]
                ll[rows(qstart, A_QB), :] = alpha * ll[rows(qstart, A_QB), :] + pv[:, HEAD_DIM:]
                mm[rows(qstart, A_QB), :] = mnew
            return carry

        lax.fori_loop(0, dil * nmb, body, 0)

    o_ref[0] = (acc[...] / ll[...]).astype(BF16)


def _attn_a(proj, bias_tiles):
    b, t, _ = proj.shape
    kern = functools.partial(_attn_a_kernel, t=t)
    return pl.pallas_call(
        kern,
        grid=(b, N_HEADS_A),
        in_specs=[
            pl.BlockSpec((1, t, HEAD_DIM), lambda bi, h: (bi, 0, COL_QA + h)),
            pl.BlockSpec((1, t, HEAD_DIM), lambda bi, h: (bi, 0, COL_KA + h)),
            pl.BlockSpec((1, t, HEAD_DIM), lambda bi, h: (bi, 0, COL_VA + h)),
            pl.BlockSpec((3, 1, A_QB, A_KB), lambda bi, h: (0, h, 0, 0)),
        ],
        out_specs=pl.BlockSpec((1, t, HEAD_DIM), lambda bi, h: (bi, 0, h)),
        out_shape=jax.ShapeDtypeStruct((b, t, WIDTH_A), BF16),
        scratch_shapes=[
            pltpu.VMEM((t, HEAD_DIM), F32),
            pltpu.VMEM((t + 2 * A_PAD, HEAD_DIM), F32),
            pltpu.VMEM((t + 2 * A_PAD, HEAD_DIM), F32),
            pltpu.VMEM((t, HEAD_DIM), F32),
            pltpu.VMEM((t, HEAD_DIM), F32),
            pltpu.VMEM((t, HEAD_DIM), F32),
        ],
        compiler_params=_cparams(("arbitrary", "arbitrary")),
        name="attn_dilated",
    )(proj, proj, proj, bias_tiles)


def _t5_bucket(rel):
    nb = NUM_BUCKETS // 2
    max_exact = nb // 2
    sign_off = jnp.where(rel > 0, nb, 0)
    n = jnp.abs(rel)
    nf = jnp.maximum(n, 1).astype(F32)
    large = max_exact + (jnp.log(nf / max_exact) / math.log(MAX_DISTANCE / max_exact)
                         * (nb - max_exact)).astype(jnp.int32)
    large = jnp.minimum(large, nb - 1)
    return sign_off + jnp.where(n < max_exact, n, large)


def _bias_tiles(rel_bias_table):
    qi = jnp.arange(A_QB, dtype=jnp.int32)[:, None]
    kj = jnp.arange(A_KB, dtype=jnp.int32)[None, :]
    rel = kj - A_RADIUS - qi
    tiles = []
    for _, dil in DIL_CONFIGS:
        bias = rel_bias_table[_t5_bucket(rel * dil)].astype(F32).transpose(2, 0, 1)
        tiles.append(jnp.where((jnp.abs(rel) <= A_RADIUS)[None], bias, NEG_INF))
    return jnp.stack(tiles, axis=0)


def _attn_b_kernel(q_ref, k_ref, v_ref, o_ref, v1_scr, *, tq):
    @pl.when(pl.program_id(2) == 0)
    def _():
        v1_scr[:, 0:HEAD_DIM] = v_ref[0]
        v1_scr[:, HEAD_DIM:2 * HEAD_DIM] = jnp.ones(v_ref.shape[1:], BF16)

    q = q_ref[0]
    qs = jnp.concatenate([q[:, i * HEAD_DIM:(i + 1) * HEAD_DIM] for i in range(GQA_GROUP)], axis=0)
    s = lax.dot_general(qs, k_ref[0], (((1,), (1,)), ((), ())), preferred_element_type=F32)
    m = jnp.max(s, axis=-1, keepdims=True)
    p = jnp.exp(s - m).astype(BF16)
    pv = jnp.dot(p, v1_scr[...], preferred_element_type=F32)
    o = pv[:, :HEAD_DIM] / pv[:, HEAD_DIM:]
    o_ref[0] = jnp.concatenate([o[i * tq:(i + 1) * tq] for i in range(GQA_GROUP)],
                               axis=1).astype(BF16)


def _attn_b(proj):
    b, t, _ = proj.shape
    tq = 128
    gw = GQA_GROUP * HEAD_DIM
    return pl.pallas_call(
        functools.partial(_attn_b_kernel, tq=tq),
        grid=(b, N_KV_B, t // tq),
        in_specs=[
            pl.BlockSpec((1, tq, gw), lambda bi, g, qi: (bi, qi, COL_QB // GQA_GROUP + g)),
            pl.BlockSpec((1, t, HEAD_DIM), lambda bi, g, qi: (bi, 0, COL_KB + g)),
            pl.BlockSpec((1, t, HEAD_DIM), lambda bi, g, qi: (bi, 0, COL_VB + g)),
        ],
        out_specs=pl.BlockSpec((1, tq, gw), lambda bi, g, qi: (bi, qi, g)),
        out_shape=jax.ShapeDtypeStruct((b, t, WIDTH_B), BF16),
        scratch_shapes=[pltpu.VMEM((t, 2 * HEAD_DIM), BF16)],
        compiler_params=_cparams(("arbitrary", "arbitrary", "arbitrary")),
        name="attn_gqa",
    )(proj, proj, proj)


def _outproj_kernel(oa_ref, ob_ref, x_ref, gt_ref, sc_ref, sh_ref, ga_ref, gb_ref, gn_ref,
                    w_ref, wr_ref, x1_ref, h2_ref, lg_ref):
    def wide_norm(o_ref_, g_ref_):
        o = o_ref_[0].astype(F32)
        ms = jnp.mean(o * o, axis=-1, keepdims=True)
        return (o * lax.rsqrt(ms + EPS) * g_ref_[...]).astype(BF16)

    na = wide_norm(oa_ref, ga_ref)
    nb = wide_norm(ob_ref, gb_ref)
    mix = (jnp.dot(na, w_ref[0:WIDTH_A, :], preferred_element_type=F32)
           + jnp.dot(nb, w_ref[WIDTH_A:WIDTH_A + WIDTH_B, :], preferred_element_type=F32))
    x1 = x_ref[0] + gt_ref[0] * mix
    x1_ref[0] = x1
    ms = jnp.mean(x1 * x1, axis=-1, keepdims=True)
    h2 = (x1 * lax.rsqrt(ms + EPS) * gn_ref[...]) * (1.0 + sc_ref[0]) + sh_ref[0]
    h2b = h2.astype(BF16)
    lg_ref[0] = jnp.dot(h2b, wr_ref[...], preferred_element_type=F32)
    half = h2.shape[1] // 2
    rounded = h2b.astype(F32)
    lo = lax.bitcast_convert_type(rounded[:, :half], jnp.uint32)
    hi = lax.bitcast_convert_type(rounded[:, half:], jnp.uint32)
    h2_ref[0] = (hi & jnp.uint32(0xFFFF0000)) | (lo >> 16)


def _outproj(oa, ob, x, mod3, boff, g_out_a, g_out_b, g_norm_ffn, w_out_bf, w_router_pad):
    b, t, d = x.shape
    tm = 512
    nr = w_router_pad.shape[1]
    row = lambda k: (lambda bi, ti: ((bi + boff) * 6 + k, 0, 0))
    return pl.pallas_call(
        _outproj_kernel,
        grid=(b, t // tm),
        in_specs=[
            pl.BlockSpec((1, tm, WIDTH_A), lambda bi, ti: (bi, ti, 0)),
            pl.BlockSpec((1, tm, WIDTH_B), lambda bi, ti: (bi, ti, 0)),
            pl.BlockSpec((1, tm, d), lambda bi, ti: (bi, ti, 0)),
            pl.BlockSpec((1, 1, d), row(2)),
            pl.BlockSpec((1, 1, d), row(4)),
            pl.BlockSpec((1, 1, d), row(3)),
            pl.BlockSpec((1, WIDTH_A), lambda bi, ti: (0, 0)),
            pl.BlockSpec((1, WIDTH_B), lambda bi, ti: (0, 0)),
            pl.BlockSpec((1, d), lambda bi, ti: (0, 0)),
            pl.BlockSpec((WIDTH_A + WIDTH_B, d), lambda bi, ti: (0, 0)),
            pl.BlockSpec((d, nr), lambda bi, ti: (0, 0)),
        ],
        out_specs=[
            pl.BlockSpec((1, tm, d), lambda bi, ti: (bi, ti, 0)),
            pl.BlockSpec((1, tm, d // 2), lambda bi, ti: (bi, ti, 0)),
            pl.BlockSpec((1, tm, nr), lambda bi, ti: (bi, ti, 0)),
        ],
        out_shape=[jax.ShapeDtypeStruct((b, t, d), F32),
                   jax.ShapeDtypeStruct((b, t, d // 2), jnp.uint32),
                   jax.ShapeDtypeStruct((b, t, nr), F32)],
        compiler_params=_cparams(("arbitrary", "arbitrary")),
        name="outproj",
    )(oa, ob, x, mod3, mod3, mod3, g_out_a, g_out_b, g_norm_ffn, w_out_bf, w_router_pad)


def _ffn_kernel(x_ref, wg_ref, wu_ref, wd_ref, g_ref, o_ref, xb):
    f = pl.program_id(2)

    @pl.when(f == 0)
    def _():
        packed = x_ref[0]
        half = packed.shape[1]
        xb[:, 0:half] = lax.bitcast_convert_type(packed << 16, F32).astype(BF16)
        xb[:, half:2 * half] = lax.bitcast_convert_type(
            packed & jnp.uint32(0xFFFF0000), F32).astype(BF16)

    x = xb[...]
    a = jnp.dot(x, wg_ref[0], preferred_element_type=F32)
    u = jnp.dot(x, wu_ref[0], preferred_element_type=F32)
    hmid = (a * (1.0 / (1.0 + jnp.exp(-a))) * u).astype(BF16)
    y = jnp.dot(hmid, wd_ref[0], preferred_element_type=F32)

    @pl.when(f == 0)
    def _():
        o_ref[0] = y

    @pl.when(f > 0)
    def _():
        o_ref[0] += y

    @pl.when(f == pl.num_programs(2) - 1)
    def _():
        o_ref[0] = o_ref[0] * g_ref[0]


def _ffn(xe, wg, wu, wd, g):
    e, cap, dh = xe.shape
    d = 2 * dh
    fdim = wg.shape[2]
    tm = min(cap, 1024)
    tf = 512
    return pl.pallas_call(
        _ffn_kernel,
        grid=(e, cap // tm, fdim // tf),
        in_specs=[
            pl.BlockSpec((1, tm, dh), lambda ei, mi, fi: (ei, mi, 0)),
            pl.BlockSpec((1, d, tf), lambda ei, mi, fi: (ei, 0, fi)),
            pl.BlockSpec((1, d, tf), lambda ei, mi, fi: (ei, 0, fi)),
            pl.BlockSpec((1, tf, d), lambda ei, mi, fi: (ei, fi, 0)),
            pl.BlockSpec((1, tm, 1), lambda ei, mi, fi: (ei, mi, 0)),
        ],
        out_specs=pl.BlockSpec((1, tm, d), lambda ei, mi, fi: (ei, mi, 0)),
        out_shape=jax.ShapeDtypeStruct((e, cap, d), F32),
        scratch_shapes=[pltpu.VMEM((tm, d), BF16)],
        compiler_params=_cparams(("arbitrary", "arbitrary", "arbitrary")),
        name="expert_ffn",
    )(xe, wg, wu, wd, g.reshape(e, cap, 1))


LANES = 128


def _prefix_counts(mask_f32, upper, lower):
    within = jnp.dot(mask_f32.astype(BF16), upper, preferred_element_type=F32)
    tot = jnp.broadcast_to(within[:, LANES - 1:LANES], within.shape)
    offs = jnp.dot(lower, tot.astype(BF16), preferred_element_type=F32)
    return within, offs


def _route_kernel(lt_ref, idx_ref, g_ref, aff_scr, *, cap):
    e = pl.program_id(0)
    nchunk = lt_ref.shape[1]

    @pl.when(e == 0)
    def _():
        l = lt_ref[...]
        ex = jnp.exp(l - jnp.max(l, axis=0, keepdims=True))
        aff_scr[...] = ex / jnp.sum(ex, axis=0, keepdims=True)

    a = aff_scr[e]
    bits = lax.bitcast_convert_type(a, jnp.int32)
    capf = jnp.float32(cap)

    def count(m):
        return jnp.sum(m.astype(F32), axis=(0, 1), keepdims=True)

    def bit_step(i, thr):
        cand = thr | lax.shift_left(jnp.int32(1), 30 - i)
        return jnp.where(count(bits >= cand) >= capf, cand, thr)

    thr = lax.fori_loop(0, 31, bit_step, jnp.zeros((1, 1), jnp.int32))

    ri = lax.broadcasted_iota(jnp.int32, (LANES, LANES), 0)
    ci = lax.broadcasted_iota(jnp.int32, (LANES, LANES), 1)
    upper = (ri <= ci).astype(BF16)
    rc = lax.broadcasted_iota(jnp.int32, (nchunk, nchunk), 0)
    cc = lax.broadcasted_iota(jnp.int32, (nchunk, nchunk), 1)
    lower = (cc < rc).astype(BF16)

    gt = bits > thr
    eq = bits == thr
    need = capf - count(gt)
    eq_within, eq_offs = _prefix_counts(eq.astype(F32), upper, lower)
    sel = gt | (eq & ((eq_within + eq_offs) <= need))
    within, offs = _prefix_counts(sel.astype(F32), upper, lower)

    offs_col = offs[:, 0:1]
    ends_col = offs_col + within[:, LANES - 1:LANES]
    slot = lax.broadcasted_iota(jnp.int32, (1, cap), 1).astype(F32)
    cstar = jnp.sum((ends_col <= slot).astype(F32), axis=0, keepdims=True)
    chunk_oh = lax.broadcasted_iota(jnp.int32, (nchunk, cap), 0).astype(F32) == cstar
    rank = slot - jnp.sum(jnp.where(chunk_oh, offs_col, 0.0), axis=0, keepdims=True)
    oh = chunk_oh.astype(BF16)
    tdot = functools.partial(lax.dot_general, dimension_numbers=(((0,), (0,)), ((), ())),
                             preferred_element_type=F32)
    wsel = tdot(within.astype(BF16), oh)
    lstar = jnp.sum((wsel <= rank).astype(F32), axis=0, keepdims=True)
    idx_ref[0] = (cstar * LANES + lstar).astype(jnp.int32)

    a1 = a.astype(BF16)
    r1 = a - a1.astype(F32)
    a2 = r1.astype(BF16)
    a3 = (r1 - a2.astype(F32)).astype(BF16)
    asel = (tdot(a1, oh) + tdot(a2, oh)) + tdot(a3, oh)
    lane_oh = lax.broadcasted_iota(jnp.int32, (LANES, cap), 0).astype(F32) == lstar
    g_ref[0] = jnp.sum(jnp.where(lane_oh, asel, 0.0), axis=0, keepdims=True)


def _route(lt, cap):
    e, nchunk, _ = lt.shape
    return pl.pallas_call(
        functools.partial(_route_kernel, cap=cap),
        grid=(e,),
        in_specs=[pl.BlockSpec((e, nchunk, LANES), lambda ei: (0, 0, 0))],
        out_specs=[pl.BlockSpec((1, 1, cap), lambda ei: (ei, 0, 0)),
                   pl.BlockSpec((1, 1, cap), lambda ei: (ei, 0, 0))],
        out_shape=[jax.ShapeDtypeStruct((e, 1, cap), jnp.int32),
                   jax.ShapeDtypeStruct((e, 1, cap), F32)],
        scratch_shapes=[pltpu.VMEM((e, nchunk, LANES), F32)],
        compiler_params=_cparams(("arbitrary",)),
        name="route",
    )(lt)


GATHER_ROWS = 2048


def _gather_kernel(idx_ref, src_ref, dst_ref, sems):
    k = pl.program_id(0)
    nk = pl.num_programs(0)
    base = k * GATHER_ROWS
    slot = lax.rem(k, 2)

    def row_copy(r, sl):
        return pltpu.make_async_copy(src_ref.at[pl.ds(idx_ref[0, 0, r], 1), :],
                                     dst_ref.at[pl.ds(base + r, 1), :], sems.at[sl])

    def issue(r, carry):
        row_copy(r, slot).start()
        return carry

    lax.fori_loop(0, GATHER_ROWS, issue, 0, unroll=8)

    def wait_step(step, sl):
        pltpu.make_async_copy(src_ref.at[pl.ds(0, GATHER_ROWS), :],
                              dst_ref.at[pl.ds(step * GATHER_ROWS, GATHER_ROWS), :],
                              sems.at[sl]).wait()

    @pl.when(k > 0)
    def _():
        wait_step(k - 1, 1 - slot)

    @pl.when(k == nk - 1)
    def _():
        wait_step(k, slot)


def _gather_rows(idx_flat, src):
    n_rows = idx_flat.shape[0]
    steps = n_rows // GATHER_ROWS
    return pl.pallas_call(
        _gather_kernel,
        grid=(steps,),
        in_specs=[pl.BlockSpec((1, 1, GATHER_ROWS), lambda k: (k, 0, 0), memory_space=pltpu.SMEM),
                  pl.BlockSpec(memory_space=pl.ANY)],
        out_specs=pl.BlockSpec(memory_space=pl.ANY),
        out_shape=jax.ShapeDtypeStruct((n_rows, src.shape[1]), src.dtype),
        scratch_shapes=[pltpu.SemaphoreType.DMA((2,))],
        compiler_params=_cparams(("arbitrary",)),
        name="gather_rows",
    )(idx_flat.reshape(steps, 1, GATHER_ROWS), src)


SCATTER_ROWS = 512
SCATTER_BUFS = 3


def _scatter_kernel(idx_ref, idx_next_ref, ye_ref, idxv_ref, gate_ref, x1_ref, out_ref,
                    obuf, gsem, ssem, *, tiles_per_expert, tok_shift, boff):
    del x1_ref
    k = pl.program_id(0)
    nk = pl.num_programs(0)
    first = lax.rem(k, tiles_per_expert) == 0
    prev_first = lax.rem(k - 1, tiles_per_expert) == 0
    next_first = lax.rem(k + 1, tiles_per_expert) == 0
    slot = lax.rem(k, SCATTER_BUFS)

    def gather_copy(ref_idx, r, sl):
        return pltpu.make_async_copy(out_ref.at[pl.ds(ref_idx[0, 0, r], 1), :],
                                     obuf.at[sl, pl.ds(r, 1), :], gsem.at[sl])

    def scatter_copy(ref_idx, r, sl):
        return pltpu.make_async_copy(obuf.at[sl, pl.ds(r, 1), :],
                                     out_ref.at[pl.ds(ref_idx[0, 0, r], 1), :], ssem.at[sl])

    def wait_all(sem, sl):
        pltpu.make_async_copy(out_ref.at[pl.ds(0, SCATTER_ROWS), :], obuf.at[sl], sem.at[sl]).wait()

    def wait_scatter(step):
        wait_all(ssem, lax.rem(step, SCATTER_BUFS))

    @pl.when(first & (k >= 1))
    def _():
        wait_scatter(k - 1)

    @pl.when((k >= 2) & (first | jnp.logical_not(prev_first)))
    def _():
        wait_scatter(k - 2)

    def issue_gather(step_idx_ref, sl):
        def body(r, carry):
            gather_copy(step_idx_ref, r, sl).start()
            return carry
        lax.fori_loop(0, SCATTER_ROWS, body, 0, unroll=8)

    @pl.when(first)
    def _():
        issue_gather(idx_ref, slot)

    @pl.when((k + 1 < nk) & jnp.logical_not(next_first))
    def _():
        issue_gather(idx_next_ref, lax.rem(k + 1, SCATTER_BUFS))

    wait_all(gsem, slot)

    bid = lax.shift_right_logical(idxv_ref[0], tok_shift) + boff
    oh = (bid == lax.broadcasted_iota(jnp.int32, (1, gate_ref.shape[0]), 1)).astype(BF16)
    gt = gate_ref[...]
    g1 = gt.astype(BF16)
    r1 = gt - g1.astype(F32)
    g2 = r1.astype(BF16)
    g3 = (r1 - g2.astype(F32)).astype(BF16)
    dot = functools.partial(jnp.dot, preferred_element_type=F32)
    gate_rows = (dot(oh, g1) + dot(oh, g2)) + dot(oh, g3)
    obuf[slot] = obuf[slot] + gate_rows * ye_ref[...]

    def issue_scatter(r, carry):
        scatter_copy(idx_ref, r, slot).start()
        return carry

    lax.fori_loop(0, SCATTER_ROWS, issue_scatter, 0, unroll=8)

    @pl.when(k == nk - 1)
    def _():
        wait_scatter(k)
        wait_scatter(k - 1)


def _scatter_add(idx_flat, ye, gate_rows, x1, *, cap, tok_shift, boff):
    n_rows, d = ye.shape
    steps = n_rows // SCATTER_ROWS
    tiles_per_expert = cap // SCATTER_ROWS
    assert tiles_per_expert >= 2
    kern = functools.partial(_scatter_kernel, tiles_per_expert=tiles_per_expert,
                             tok_shift=tok_shift, boff=boff)
    idx3 = idx_flat.reshape(steps, 1, SCATTER_ROWS)
    return pl.pallas_call(
        kern,
        grid=(steps,),
        in_specs=[
            pl.BlockSpec((1, 1, SCATTER_ROWS), lambda k: (k, 0, 0), memory_space=pltpu.SMEM),
            pl.BlockSpec((1, 1, SCATTER_ROWS), lambda k: (jnp.minimum(k + 1, steps - 1), 0, 0),
                         memory_space=pltpu.SMEM),
            pl.BlockSpec((SCATTER_ROWS, d), lambda k: (k, 0)),
            pl.BlockSpec((1, SCATTER_ROWS, 1), lambda k: (k, 0, 0)),
            pl.BlockSpec(gate_rows.shape, lambda k: (0, 0)),
            pl.BlockSpec(memory_space=pl.ANY),
        ],
        out_specs=pl.BlockSpec(memory_space=pl.ANY),
        out_shape=jax.ShapeDtypeStruct(x1.shape, x1.dtype),
        scratch_shapes=[pltpu.VMEM((SCATTER_BUFS, SCATTER_ROWS, d), F32),
                        pltpu.SemaphoreType.DMA((SCATTER_BUFS,)),
                        pltpu.SemaphoreType.DMA((SCATTER_BUFS,))],
        input_output_aliases={5: 0},
        compiler_params=_cparams(("arbitrary",)),
        name="scatter_add",
    )(idx3, idx3, ye, idx_flat.reshape(steps, SCATTER_ROWS, 1), gate_rows, x1)


def _rope_tables(t):
    half = HEAD_DIM // 2
    quarter = half // 2
    freqs = ROPE_THETA ** (-(jnp.arange(quarter, dtype=F32) / quarter))
    rows = t // GRID_W
    row_ids = jnp.repeat(jnp.arange(rows, dtype=jnp.int32), GRID_W).astype(F32)
    col_ids = jnp.tile(jnp.arange(GRID_W, dtype=jnp.int32), rows).astype(F32)
    ang_r = row_ids[:, None] * freqs[None, :]
    ang_c = col_ids[:, None] * freqs[None, :]
    z = jnp.zeros_like(ang_r)
    cos_t = jnp.concatenate([jnp.cos(ang_r)] * 2 + [jnp.cos(ang_c)] * 2, axis=1)
    sa_t = jnp.concatenate([-jnp.sin(ang_r), z, -jnp.sin(ang_c), z], axis=1)
    sb_t = jnp.concatenate([z, jnp.sin(ang_r), z, jnp.sin(ang_c)], axis=1)
    return cos_t, sa_t, sb_t


def kernel(x_prompt, x_sample, c_prompt, c_sample, rel_bias_table, w_ada, b_ada, g_norm_mix, g_norm_ffn, w_in, g_q_a, g_k_a, g_q_b, g_k_b, g_out_a, g_out_b, w_out, w_router, w_gate, w_up, w_down):
    d = D_MODEL
    nbp, nbs = c_prompt.shape[0], c_sample.shape[0]
    c_all = jnp.concatenate([c_prompt, c_sample, jnp.zeros((16 - nbp - nbs, d), F32)], axis=0)
    mod = _ada(c_all, w_ada[0], b_ada[0])
    mod3 = mod.reshape(16 * 6, 1, d)
    gt2_all = mod[:, 5 * d:6 * d]

    scale = HEAD_DIM ** -0.5
    gains = jnp.stack([g_q_a[0] * scale, g_k_a[0], g_q_b[0] * scale, g_k_b[0]], axis=0)
    w_in_bf = w_in[0].astype(BF16)
    w_out_bf = w_out[0].astype(BF16)
    w_router_pad = jnp.pad(w_router[0], ((0, 0), (0, HEAD_DIM - N_EXPERTS))).astype(BF16)
    wg, wu, wd = w_gate[0].astype(BF16), w_up[0].astype(BF16), w_down[0].astype(BF16)
    bias_tiles = _bias_tiles(rel_bias_table)
    gn1 = g_norm_mix[0].reshape(1, d)
    gn2 = g_norm_ffn[0].reshape(1, d)
    ga = g_out_a[0].reshape(1, WIDTH_A)
    gb = g_out_b[0].reshape(1, WIDTH_B)

    def run(x, boff):
        b, t, _ = x.shape
        cos_t, sa_t, sb_t = _rope_tables(t)
        proj = _inproj(x, mod3, boff, gn1, w_in_bf, gains, cos_t, sa_t, sb_t)
        oa = _attn_a(proj, bias_tiles)
        ob = _attn_b(proj)
        x1, h2, logits = _outproj(oa, ob, x, mod3, boff, ga, gb, gn2, w_out_bf, w_router_pad)
        n = b * t
        cap = EC_CAPACITY_FACTOR * n // N_EXPERTS
        lt = logits.reshape(n, -1)[:, :N_EXPERTS].T.reshape(N_EXPERTS, n // LANES, LANES)
        idx, g = _route(lt, cap)
        idx_flat = idx.reshape(N_EXPERTS * cap)
        xe = _gather_rows(idx_flat, h2.reshape(n, d // 2))
        ye = _ffn(xe.reshape(N_EXPERTS, cap, d // 2), wg, wu, wd, g.reshape(N_EXPERTS, cap))
        out = _scatter_add(idx_flat, ye.reshape(N_EXPERTS * cap, d), gt2_all, x1.reshape(n, d),
                           cap=cap, tok_shift=int(math.log2(t)), boff=boff)
        return out.reshape(b, t, d)

    return (run(x_prompt, 0), run(x_sample, nbp))
```

```python
import functools
import math

import jax
import jax.numpy as jnp
from jax import lax
from jax.experimental import pallas as pl
from jax.experimental.pallas import tpu as pltpu

F32 = jnp.float32
BF16 = jnp.bfloat16

D_MODEL = 2048
HEAD_DIM = 128
N_HEADS_A = 8
N_HEADS_B = 8
N_KV_B = 2
GQA_GROUP = N_HEADS_B // N_KV_B
WIDTH_A = N_HEADS_A * HEAD_DIM
WIDTH_B = N_HEADS_B * HEAD_DIM
KV_WIDTH_B = N_KV_B * HEAD_DIM
IN_COLS = 3 * WIDTH_A + WIDTH_B + 2 * KV_WIDTH_B
DIL_CONFIGS = ((128, 1), (512, 4), (2048, 16))
NUM_BUCKETS = 32
MAX_DISTANCE = 1024
GRID_W = 64
ROPE_THETA = 10000.0
N_EXPERTS = 16
EC_CAPACITY_FACTOR = 2
D_EXPERT = 2048
EPS = 1e-6
NEG_INF = -1e30

VMEM_LIMIT_V7X = 56 * 1024 * 1024
LANES = 128
SUBLANES = 8

COL_QA, COL_KA, COL_VA = 0, 8, 16
COL_QB, COL_KB, COL_VB = 24, 32, 34

IN_TN = 512
A_QB = 128
A_KB = 256
A_RADIUS = 64
A_PAD = A_RADIUS * 16


def _cparams(sem):
    return pltpu.CompilerParams(dimension_semantics=sem, vmem_limit_bytes=VMEM_LIMIT_V7X)


def _ada_kernel(c_ref, w_ref, b_ref, o_ref):
    c = c_ref[...]
    s = c * (1.0 / (1.0 + jnp.exp(-c)))
    o_ref[...] = jnp.dot(s.astype(BF16), w_ref[...].astype(BF16),
                         preferred_element_type=F32) + b_ref[...]


def _ada(c_all, w_ada, b_ada):
    rows, d = c_all.shape
    n = w_ada.shape[1]
    tn = 1024
    return pl.pallas_call(
        _ada_kernel,
        grid=(n // tn,),
        in_specs=[pl.BlockSpec((rows, d), lambda j: (0, 0)),
                  pl.BlockSpec((d, tn), lambda j: (0, j)),
                  pl.BlockSpec((1, tn), lambda j: (0, j))],
        out_specs=pl.BlockSpec((rows, tn), lambda j: (0, j)),
        out_shape=jax.ShapeDtypeStruct((rows, n), F32),
        compiler_params=_cparams(("arbitrary",)),
        name="ada_mod",
    )(c_all, w_ada, b_ada.reshape(1, n))


def _head_norm(a, g):
    ms = jnp.mean(a * a, axis=-1, keepdims=True)
    return a * lax.rsqrt(ms + EPS) * g


def _inproj_kernel(x_ref, sc_ref, sh_ref, gn_ref, w_ref, gains_ref, cos_ref, sa_ref, sb_ref,
                   o_ref, h_scr):
    j = pl.program_id(2)

    @pl.when(j == 0)
    def _():
        x = x_ref[0]
        ms = jnp.mean(x * x, axis=-1, keepdims=True)
        y = x * lax.rsqrt(ms + EPS) * gn_ref[...]
        h_scr[...] = (y * (1.0 + sc_ref[0]) + sh_ref[0]).astype(BF16)

    acc = jnp.dot(h_scr[...], w_ref[...], preferred_element_type=F32)

    def rope(a):
        return (a * cos_ref[...] + pltpu.roll(a, 96, 1) * sa_ref[...]
                + pltpu.roll(a, 32, 1) * sb_ref[...])

    def store_heads(fn, first, count):
        for hh in range(first, first + count):
            sl = slice(hh * HEAD_DIM, (hh + 1) * HEAD_DIM)
            o_ref[0, :, sl] = fn(acc[:, sl]).astype(BF16)

    @pl.when(j < 2)
    def _():
        store_heads(lambda a: _head_norm(a, gains_ref[0:1, :]), 0, 4)

    @pl.when((j >= 2) & (j < 4))
    def _():
        store_heads(lambda a: _head_norm(a, gains_ref[1:2, :]), 0, 4)

    @pl.when((j >= 4) & (j < 6))
    def _():
        o_ref[0] = acc.astype(BF16)

    @pl.when((j >= 6) & (j < 8))
    def _():
        store_heads(lambda a: rope(_head_norm(a, gains_ref[2:3, :])), 0, 4)

    @pl.when(j == 8)
    def _():
        store_heads(lambda a: rope(_head_norm(a, gains_ref[3:4, :])), 0, 2)
        store_heads(lambda a: a, 2, 2)


def _inproj(x, mod3, boff, g_norm, w_in_bf, gains, cos_t, sa_t, sb_t):
    b, t, d = x.shape
    tm = 1024
    nj = IN_COLS // IN_TN
    return pl.pallas_call(
        _inproj_kernel,
        grid=(b, t // tm, nj),
        in_specs=[
            pl.BlockSpec((1, tm, d), lambda bi, ti, j: (bi, ti, 0)),
            pl.BlockSpec((1, 1, d), lambda bi, ti, j: ((bi + boff) * 6 + 1, 0, 0)),
            pl.BlockSpec((1, 1, d), lambda bi, ti, j: ((bi + boff) * 6 + 0, 0, 0)),
            pl.BlockSpec((1, d), lambda bi, ti, j: (0, 0)),
            pl.BlockSpec((d, IN_TN), lambda bi, ti, j: (0, j)),
            pl.BlockSpec((4, HEAD_DIM), lambda bi, ti, j: (0, 0)),
            pl.BlockSpec((tm, HEAD_DIM), lambda bi, ti, j: (ti, 0)),
            pl.BlockSpec((tm, HEAD_DIM), lambda bi, ti, j: (ti, 0)),
            pl.BlockSpec((tm, HEAD_DIM), lambda bi, ti, j: (ti, 0)),
        ],
        out_specs=pl.BlockSpec((1, tm, IN_TN), lambda bi, ti, j: (bi, ti, j)),
        out_shape=jax.ShapeDtypeStruct((b, t, IN_COLS), BF16),
        scratch_shapes=[pltpu.VMEM((tm, d), BF16)],
        compiler_params=_cparams(("arbitrary", "arbitrary", "arbitrary")),
        name="inproj",
    )(x, mod3, mod3, g_norm, w_in_bf, gains, cos_t, sa_t, sb_t)


def _attn_a_kernel(q_ref, k_ref, v_ref, bias_ref, o_ref, qf, kf, vf, acc, mm, ll, *, t):
    zpad = jnp.zeros((A_PAD, HEAD_DIM), F32)
    kf[0:A_PAD, :] = zpad
    vf[0:A_PAD, :] = zpad
    kf[A_PAD + t:A_PAD + t + A_PAD, :] = zpad
    vf[A_PAD + t:A_PAD + t + A_PAD, :] = zpad
    kf[A_PAD:A_PAD + t, :] = k_ref[0].astype(F32)
    vf[A_PAD:A_PAD + t, :] = v_ref[0].astype(F32)
    qf[...] = q_ref[0].astype(F32)
    ones = jnp.ones((A_KB, HEAD_DIM), BF16)

    for bi, (_, dil) in enumerate(DIL_CONFIGS):
        sub_len = t // dil
        nmb = sub_len // A_QB
        shift = int(math.log2(nmb))

        def rows(start, size, dil=dil):
            return pl.ds(start, size) if dil == 1 else pl.ds(start, size, stride=dil)

        def body(idx, carry, bi=bi, dil=dil, sub_len=sub_len, nmb=nmb, shift=shift, rows=rows):
            rho = lax.shift_right_logical(idx, shift)
            mb = lax.bitwise_and(idx, nmb - 1)
            qstart = rho + mb * (A_QB * dil)
            kstart = A_PAD + qstart - A_RADIUS * dil
            q = qf[rows(qstart, A_QB), :].astype(BF16)
            k = kf[rows(kstart, A_KB), :].astype(BF16)
            v = vf[rows(kstart, A_KB), :].astype(BF16)
            s = lax.dot_general(q, k, (((1,), (1,)), ((), ())), preferred_element_type=F32)
            s = s + bias_ref[bi, 0]
            kidx = mb * A_QB - A_RADIUS + lax.broadcasted_iota(jnp.int32, (1, A_KB), 1)
            s = jnp.where((kidx >= 0) & (kidx < sub_len), s, NEG_INF)
            mblk = jnp.max(s, axis=-1, keepdims=True)
            v1 = jnp.concatenate([v, ones], axis=1)
            if bi == 0:
                p = jnp.exp(s - mblk).astype(BF16)
                pv = jnp.dot(p, v1, preferred_element_type=F32)
                acc[rows(qstart, A_QB), :] = pv[:, :HEAD_DIM]
                ll[rows(qstart, A_QB), :] = pv[:, HEAD_DIM:]
                mm[rows(qstart, A_QB), :] = jnp.broadcast_to(mblk, (A_QB, HEAD_DIM))
            else:
                mold = mm[rows(qstart, A_QB), :]
                mnew = jnp.maximum(mold, mblk)
                alpha = jnp.exp(mold - mnew)
                p = jnp.exp(s - mnew[:, 0:1]).astype(BF16)
                pv = jnp.dot(p, v1, preferred_element_type=F32)
                acc[rows(qstart, A_QB), :] = alpha * acc[rows(qstart, A_QB), :] + pv[:, :HEAD_DIM]
                ll[rows(qstart, A_QB), :] = alpha * ll[rows(qstart, A_QB), :] + pv[:, HEAD_DIM:]
                mm[rows(qstart, A_QB), :] = mnew
            return carry

        lax.fori_loop(0, dil * nmb, body, 0)

    o_ref[0] = (acc[...] / ll[...]).astype(BF16)


def _attn_a(proj, bias_tiles):
    b, t, _ = proj.shape
    kern = functools.partial(_attn_a_kernel, t=t)
    return pl.pallas_call(
        kern,
        grid=(b, N_HEADS_A),
        in_specs=[
            pl.BlockSpec((1, t, HEAD_DIM), lambda bi, h: (bi, 0, COL_QA + h)),
            pl.BlockSpec((1, t, HEAD_DIM), lambda bi, h: (bi, 0, COL_KA + h)),
            pl.BlockSpec((1, t, HEAD_DIM), lambda bi, h: (bi, 0, COL_VA + h)),
            pl.BlockSpec((3, 1, A_QB, A_KB), lambda bi, h: (0, h, 0, 0)),
        ],
        out_specs=pl.BlockSpec((1, t, HEAD_DIM), lambda bi, h: (bi, 0, h)),
        out_shape=jax.ShapeDtypeStruct((b, t, WIDTH_A), BF16),
        scratch_shapes=[
            pltpu.VMEM((t, HEAD_DIM), F32),
            pltpu.VMEM((t + 2 * A_PAD, HEAD_DIM), F32),
            pltpu.VMEM((t + 2 * A_PAD, HEAD_DIM), F32),
            pltpu.VMEM((t, HEAD_DIM), F32),
            pltpu.VMEM((t, HEAD_DIM), F32),
            pltpu.VMEM((t, HEAD_DIM), F32),
        ],
        compiler_params=_cparams(("arbitrary", "arbitrary")),
        name="attn_dilated",
    )(proj, proj, proj, bias_tiles)


def _t5_bucket(rel):
    nb = NUM_BUCKETS // 2
    max_exact = nb // 2
    sign_off = jnp.where(rel > 0, nb, 0)
    n = jnp.abs(rel)
    nf = jnp.maximum(n, 1).astype(F32)
    large = max_exact + (jnp.log(nf / max_exact) / math.log(MAX_DISTANCE / max_exact)
                         * (nb - max_exact)).astype(jnp.int32)
    large = jnp.minimum(large, nb - 1)
    return sign_off + jnp.where(n < max_exact, n, large)


def _bias_tiles(rel_bias_table):
    qi = jnp.arange(A_QB, dtype=jnp.int32)[:, None]
    kj = jnp.arange(A_KB, dtype=jnp.int32)[None, :]
    rel = kj - A_RADIUS - qi
    tiles = []
    for _, dil in DIL_CONFIGS:
        onehot = (_t5_bucket(rel * dil)[..., None]
                  == jnp.arange(NUM_BUCKETS, dtype=jnp.int32)).astype(F32)
        bias = jnp.einsum('qkn,nh->hqk', onehot, rel_bias_table.astype(F32),
                          precision=lax.Precision.HIGHEST)
        tiles.append(jnp.where((jnp.abs(rel) <= A_RADIUS)[None], bias, NEG_INF))
    return jnp.stack(tiles, axis=0)


def _attn_b_kernel(q_ref, k_ref, v_ref, o_ref, v1_scr, *, tq):
    @pl.when(pl.program_id(2) == 0)
    def _():
        v1_scr[:, 0:HEAD_DIM] = v_ref[0]
        v1_scr[:, HEAD_DIM:2 * HEAD_DIM] = jnp.ones(v_ref.shape[1:], BF16)

    q = q_ref[0]
    qs = jnp.concatenate([q[:, i * HEAD_DIM:(i + 1) * HEAD_DIM] for i in range(GQA_GROUP)], axis=0)
    s = lax.dot_general(qs, k_ref[0], (((1,), (1,)), ((), ())), preferred_element_type=F32)
    m = jnp.max(s, axis=-1, keepdims=True)
    p = jnp.exp(s - m).astype(BF16)
    pv = jnp.dot(p, v1_scr[...], preferred_element_type=F32)
    o = pv[:, :HEAD_DIM] / pv[:, HEAD_DIM:]
    o_ref[0] = jnp.concatenate([o[i * tq:(i + 1) * tq] for i in range(GQA_GROUP)],
                               axis=1).astype(BF16)


def _attn_b(proj):
    b, t, _ = proj.shape
    tq = 128
    gw = GQA_GROUP * HEAD_DIM
    return pl.pallas_call(
        functools.partial(_attn_b_kernel, tq=tq),
        grid=(b, N_KV_B, t // tq),
        in_specs=[
            pl.BlockSpec((1, tq, gw), lambda bi, g, qi: (bi, qi, COL_QB // GQA_GROUP + g)),
            pl.BlockSpec((1, t, HEAD_DIM), lambda bi, g, qi: (bi, 0, COL_KB + g)),
            pl.BlockSpec((1, t, HEAD_DIM), lambda bi, g, qi: (bi, 0, COL_VB + g)),
        ],
        out_specs=pl.BlockSpec((1, tq, gw), lambda bi, g, qi: (bi, qi, g)),
        out_shape=jax.ShapeDtypeStruct((b, t, WIDTH_B), BF16),
        scratch_shapes=[pltpu.VMEM((t, 2 * HEAD_DIM), BF16)],
        compiler_params=_cparams(("arbitrary", "arbitrary", "arbitrary")),
        name="attn_gqa",
    )(proj, proj, proj)


def _outproj_kernel(oa_ref, ob_ref, x_ref, gt_ref, sc_ref, sh_ref, ga_ref, gb_ref, gn_ref,
                    w_ref, wr_ref, x1_ref, h2_ref, lg_ref):
    def wide_norm(o_ref_, g_ref_):
        o = o_ref_[0].astype(F32)
        ms = jnp.mean(o * o, axis=-1, keepdims=True)
        return (o * lax.rsqrt(ms + EPS) * g_ref_[...]).astype(BF16)

    na = wide_norm(oa_ref, ga_ref)
    nb = wide_norm(ob_ref, gb_ref)
    mix = (jnp.dot(na, w_ref[0:WIDTH_A, :], preferred_element_type=F32)
           + jnp.dot(nb, w_ref[WIDTH_A:WIDTH_A + WIDTH_B, :], preferred_element_type=F32))
    x1 = x_ref[0] + gt_ref[0] * mix
    x1_ref[0] = x1
    ms = jnp.mean(x1 * x1, axis=-1, keepdims=True)
    h2 = (x1 * lax.rsqrt(ms + EPS) * gn_ref[...]) * (1.0 + sc_ref[0]) + sh_ref[0]
    h2b = h2.astype(BF16)
    lg_ref[0] = jnp.dot(h2b, wr_ref[...], preferred_element_type=F32)
    tm, dd = h2.shape
    half = dd // 2
    rounded = h2b.astype(F32)
    lo = lax.bitcast_convert_type(rounded[:, :half], jnp.uint32)
    hi = lax.bitcast_convert_type(rounded[:, half:], jnp.uint32)
    packed = (hi & jnp.uint32(0xFFFF0000)) | (lo >> 16)
    for s in range(half // LANES):
        h2_ref[pl.ds(s, tm, stride=half // LANES), :] = packed[:, s * LANES:(s + 1) * LANES]


def _outproj(oa, ob, x, mod3, boff, g_out_a, g_out_b, g_norm_ffn, w_out_bf, w_router_pad):
    b, t, d = x.shape
    tm = 512
    nr = w_router_pad.shape[1]
    slab = d // 2 // LANES
    row = lambda k: (lambda bi, ti: ((bi + boff) * 6 + k, 0, 0))
    return pl.pallas_call(
        _outproj_kernel,
        grid=(b, t // tm),
        in_specs=[
            pl.BlockSpec((1, tm, WIDTH_A), lambda bi, ti: (bi, ti, 0)),
            pl.BlockSpec((1, tm, WIDTH_B), lambda bi, ti: (bi, ti, 0)),
            pl.BlockSpec((1, tm, d), lambda bi, ti: (bi, ti, 0)),
            pl.BlockSpec((1, 1, d), row(2)),
            pl.BlockSpec((1, 1, d), row(4)),
            pl.BlockSpec((1, 1, d), row(3)),
            pl.BlockSpec((1, WIDTH_A), lambda bi, ti: (0, 0)),
            pl.BlockSpec((1, WIDTH_B), lambda bi, ti: (0, 0)),
            pl.BlockSpec((1, d), lambda bi, ti: (0, 0)),
            pl.BlockSpec((WIDTH_A + WIDTH_B, d), lambda bi, ti: (0, 0)),
            pl.BlockSpec((d, nr), lambda bi, ti: (0, 0)),
        ],
        out_specs=[
            pl.BlockSpec((1, tm, d), lambda bi, ti: (bi, ti, 0)),
            pl.BlockSpec((tm * slab, LANES), lambda bi, ti: (bi * (t // tm) + ti, 0)),
            pl.BlockSpec((1, tm, nr), lambda bi, ti: (bi, ti, 0)),
        ],
        out_shape=[jax.ShapeDtypeStruct((b, t, d), F32),
                   jax.ShapeDtypeStruct((b * t * slab, LANES), jnp.uint32),
                   jax.ShapeDtypeStruct((b, t, nr), F32)],
        compiler_params=_cparams(("arbitrary", "arbitrary")),
        name="outproj",
    )(oa, ob, x, mod3, mod3, mod3, g_out_a, g_out_b, g_norm_ffn, w_out_bf, w_router_pad)


def _ffn_kernel(idx_ref, idx_next_ref, h_ref, wg_ref, wu_ref, wd_ref, g_ref, o_ref,
                xslab, xb, sems, *, tm, slab):
    f = pl.program_id(2)
    k = pl.program_id(0) * pl.num_programs(1) + pl.program_id(1)
    nk = pl.num_programs(0) * pl.num_programs(1)
    slot = lax.rem(k, 2)

    def issue_gather(ids_ref, sl):
        def body(r, carry):
            src0 = pl.multiple_of(ids_ref[0, 0, r] * slab, slab)
            dst0 = pl.multiple_of(r * slab, slab)
            pltpu.make_async_copy(h_ref.at[pl.ds(src0, slab), :],
                                  xslab.at[sl, pl.ds(dst0, slab), :], sems.at[sl]).start()
            return carry
        lax.fori_loop(0, tm, body, 0, unroll=8)

    @pl.when(f == 0)
    def _():
        @pl.when(k == 0)
        def _():
            issue_gather(idx_ref, slot)

        @pl.when(k + 1 < nk)
        def _():
            issue_gather(idx_next_ref, 1 - slot)

        pltpu.make_async_copy(h_ref.at[pl.ds(0, tm * slab), :], xslab.at[slot],
                              sems.at[slot]).wait()
        half = slab * LANES
        for s in range(slab):
            w = xslab[slot, pl.ds(s, tm, stride=slab), :]
            xb[:, s * LANES:(s + 1) * LANES] = lax.bitcast_convert_type(w << 16, F32).astype(BF16)
            xb[:, half + s * LANES:half + (s + 1) * LANES] = lax.bitcast_convert_type(
                w & jnp.uint32(0xFFFF0000), F32).astype(BF16)

    x = xb[...]
    a = jnp.dot(x, wg_ref[0], preferred_element_type=F32)
    u = jnp.dot(x, wu_ref[0], preferred_element_type=F32)
    hmid = (a * (1.0 / (1.0 + jnp.exp(-a))) * u).astype(BF16)
    y = jnp.dot(hmid, wd_ref[0], preferred_element_type=F32)

    @pl.when(f == 0)
    def _():
        o_ref[0] = y

    @pl.when(f > 0)
    def _():
        o_ref[0] += y

    @pl.when(f == pl.num_programs(2) - 1)
    def _():
        o_ref[0] = o_ref[0] * g_ref[0]


def _ffn(idx, h2slab, wg, wu, wd, g):
    e, cap = idx.shape
    d = wg.shape[1]
    slab = d // 2 // LANES
    fdim = wg.shape[2]
    tm = min(cap, 1024)
    tf = 512
    mt = cap // tm
    nk = e * mt
    idx3 = idx.reshape(nk, 1, tm)
    return pl.pallas_call(
        functools.partial(_ffn_kernel, tm=tm, slab=slab),
        grid=(e, mt, fdim // tf),
        in_specs=[
            pl.BlockSpec((1, 1, tm), lambda ei, mi, fi: (ei * mt + mi, 0, 0),
                         memory_space=pltpu.SMEM),
            pl.BlockSpec((1, 1, tm), lambda ei, mi, fi: (jnp.minimum(ei * mt + mi + 1, nk - 1), 0, 0),
                         memory_space=pltpu.SMEM),
            pl.BlockSpec(memory_space=pl.ANY),
            pl.BlockSpec((1, d, tf), lambda ei, mi, fi: (ei, 0, fi)),
            pl.BlockSpec((1, d, tf), lambda ei, mi, fi: (ei, 0, fi)),
            pl.BlockSpec((1, tf, d), lambda ei, mi, fi: (ei, fi, 0)),
            pl.BlockSpec((1, tm, 1), lambda ei, mi, fi: (ei, mi, 0)),
        ],
        out_specs=pl.BlockSpec((1, tm, d), lambda ei, mi, fi: (ei, mi, 0)),
        out_shape=jax.ShapeDtypeStruct((e, cap, d), F32),
        scratch_shapes=[pltpu.VMEM((2, tm * slab, LANES), jnp.uint32),
                        pltpu.VMEM((tm, d), BF16),
                        pltpu.SemaphoreType.DMA((2,))],
        compiler_params=_cparams(("arbitrary", "arbitrary", "arbitrary")),
        name="expert_ffn",
    )(idx3, idx3, h2slab, wg, wu, wd, g.reshape(e, cap, 1))


def _prefix_counts(mask_f32, upper, lower):
    within = jnp.dot(mask_f32.astype(BF16), upper, preferred_element_type=F32)
    tot = jnp.broadcast_to(within[:, LANES - 1:LANES], within.shape)
    offs = jnp.dot(lower, tot.astype(BF16), preferred_element_type=F32)
    return within, offs


def _route_kernel(lt_ref, idx_ref, g_ref, aff_scr, *, cap):
    e = pl.program_id(0)
    nchunk = lt_ref.shape[1]

    @pl.when(e == 0)
    def _():
        l = lt_ref[...]
        ex = jnp.exp(l - jnp.max(l, axis=0, keepdims=True))
        aff_scr[...] = ex / jnp.sum(ex, axis=0, keepdims=True)

    a = aff_scr[e]
    bits = lax.bitcast_convert_type(a, jnp.int32)
    capf = jnp.float32(cap)

    def count(m):
        return jnp.sum(m.astype(F32), axis=(0, 1), keepdims=True)

    def bit_step(i, thr):
        cand = thr | lax.shift_left(jnp.int32(1), 30 - i)
        return jnp.where(count(bits >= cand) >= capf, cand, thr)

    thr = lax.fori_loop(0, 31, bit_step, jnp.zeros((1, 1), jnp.int32))

    ri = lax.broadcasted_iota(jnp.int32, (LANES, LANES), 0)
    ci = lax.broadcasted_iota(jnp.int32, (LANES, LANES), 1)
    upper = (ri <= ci).astype(BF16)
    rc = lax.broadcasted_iota(jnp.int32, (nchunk, nchunk), 0)
    cc = lax.broadcasted_iota(jnp.int32, (nchunk, nchunk), 1)
    lower = (cc < rc).astype(BF16)

    gt = bits > thr
    eq = bits == thr
    need = capf - count(gt)
    eq_within, eq_offs = _prefix_counts(eq.astype(F32), upper, lower)
    sel = gt | (eq & ((eq_within + eq_offs) <= need))
    within, offs = _prefix_counts(sel.astype(F32), upper, lower)

    offs_col = offs[:, 0:1]
    ends_col = offs_col + within[:, LANES - 1:LANES]
    slot = lax.broadcasted_iota(jnp.int32, (1, cap), 1).astype(F32)
    cstar = jnp.sum((ends_col <= slot).astype(F32), axis=0, keepdims=True)
    chunk_oh = lax.broadcasted_iota(jnp.int32, (nchunk, cap), 0).astype(F32) == cstar
    rank = slot - jnp.sum(jnp.where(chunk_oh, offs_col, 0.0), axis=0, keepdims=True)
    oh = chunk_oh.astype(BF16)
    tdot = functools.partial(lax.dot_general, dimension_numbers=(((0,), (0,)), ((), ())),
                             preferred_element_type=F32)
    wsel = tdot(within.astype(BF16), oh)
    lstar = jnp.sum((wsel <= rank).astype(F32), axis=0, keepdims=True)
    idx_ref[0] = (cstar * LANES + lstar).astype(jnp.int32)

    a1 = a.astype(BF16)
    r1 = a - a1.astype(F32)
    a2 = r1.astype(BF16)
    a3 = (r1 - a2.astype(F32)).astype(BF16)
    asel = (tdot(a1, oh) + tdot(a2, oh)) + tdot(a3, oh)
    lane_oh = lax.broadcasted_iota(jnp.int32, (LANES, cap), 0).astype(F32) == lstar
    g_ref[0] = jnp.sum(jnp.where(lane_oh, asel, 0.0), axis=0, keepdims=True)


def _route(lt, cap):
    e, nchunk, _ = lt.shape
    return pl.pallas_call(
        functools.partial(_route_kernel, cap=cap),
        grid=(e,),
        in_specs=[pl.BlockSpec((e, nchunk, LANES), lambda ei: (0, 0, 0))],
        out_specs=[pl.BlockSpec((1, 1, cap), lambda ei: (ei, 0, 0)),
                   pl.BlockSpec((1, 1, cap), lambda ei: (ei, 0, 0))],
        out_shape=[jax.ShapeDtypeStruct((e, 1, cap), jnp.int32),
                   jax.ShapeDtypeStruct((e, 1, cap), F32)],
        scratch_shapes=[pltpu.VMEM((e, nchunk, LANES), F32)],
        compiler_params=_cparams(("arbitrary",)),
        name="route",
    )(lt)


SCATTER_ROWS = 512
SCATTER_BUFS = 3


def _scatter_kernel(idx_ref, idx_next_ref, ye_ref, idxv_ref, gate_ref, x1_ref, out_ref,
                    obuf, gsem, ssem, *, tiles_per_expert, tok_shift, boff):
    del x1_ref
    k = pl.program_id(0)
    nk = pl.num_programs(0)
    first = lax.rem(k, tiles_per_expert) == 0
    prev_first = lax.rem(k - 1, tiles_per_expert) == 0
    next_first = lax.rem(k + 1, tiles_per_expert) == 0
    slot = lax.rem(k, SCATTER_BUFS)

    def gather_copy(ref_idx, i, j, sl):
        return pltpu.make_async_copy(out_ref.at[pl.ds(ref_idx[0, 0, i * SUBLANES + j], 1), :],
                                     obuf.at[sl, i, pl.ds(j, 1), :], gsem.at[sl])

    def scatter_copy(ref_idx, i, j, sl):
        return pltpu.make_async_copy(obuf.at[sl, i, pl.ds(j, 1), :],
                                     out_ref.at[pl.ds(ref_idx[0, 0, i * SUBLANES + j], 1), :],
                                     ssem.at[sl])

    def wait_all(sem, sl):
        pltpu.make_async_copy(obuf.at[sl], obuf.at[sl], sem.at[sl]).wait()

    def wait_scatter(step):
        wait_all(ssem, lax.rem(step, SCATTER_BUFS))

    @pl.when(first & (k >= 1))
    def _():
        wait_scatter(k - 1)

    @pl.when((k >= 2) & (first | jnp.logical_not(prev_first)))
    def _():
        wait_scatter(k - 2)

    def issue_gather(step_idx_ref, sl):
        def body(i, carry):
            for j in range(SUBLANES):
                gather_copy(step_idx_ref, i, j, sl).start()
            return carry
        lax.fori_loop(0, SCATTER_ROWS // SUBLANES, body, 0, unroll=2)

    @pl.when(first)
    def _():
        issue_gather(idx_ref, slot)

    @pl.when((k + 1 < nk) & jnp.logical_not(next_first))
    def _():
        issue_gather(idx_next_ref, lax.rem(k + 1, SCATTER_BUFS))

    wait_all(gsem, slot)

    bid = lax.shift_right_logical(idxv_ref[0], tok_shift) + boff
    oh = (bid == lax.broadcasted_iota(jnp.int32, (1, gate_ref.shape[0]), 1)).astype(BF16)
    gt = gate_ref[...]
    g1 = gt.astype(BF16)
    r1 = gt - g1.astype(F32)
    g2 = r1.astype(BF16)
    g3 = (r1 - g2.astype(F32)).astype(BF16)
    dot = functools.partial(jnp.dot, preferred_element_type=F32)
    gate_rows = (dot(oh, g1) + dot(oh, g2)) + dot(oh, g3)
    upd = obuf[slot].reshape(ye_ref.shape) + gate_rows * ye_ref[...]
    obuf[slot] = upd.reshape(obuf.shape[1:])

    def issue_scatter(i, carry):
        for j in range(SUBLANES):
            scatter_copy(idx_ref, i, j, slot).start()
        return carry

    lax.fori_loop(0, SCATTER_ROWS // SUBLANES, issue_scatter, 0, unroll=2)

    @pl.when(k == nk - 1)
    def _():
        wait_scatter(k)
        wait_scatter(k - 1)


def _scatter_add(idx_flat, ye, gate_rows, x1, *, cap, tok_shift, boff):
    n_rows, d = ye.shape
    steps = n_rows // SCATTER_ROWS
    tiles_per_expert = cap // SCATTER_ROWS
    assert tiles_per_expert >= 2
    kern = functools.partial(_scatter_kernel, tiles_per_expert=tiles_per_expert,
                             tok_shift=tok_shift, boff=boff)
    idx3 = idx_flat.reshape(steps, 1, SCATTER_ROWS)
    return pl.pallas_call(
        kern,
        grid=(steps,),
        in_specs=[
            pl.BlockSpec((1, 1, SCATTER_ROWS), lambda k: (k, 0, 0), memory_space=pltpu.SMEM),
            pl.BlockSpec((1, 1, SCATTER_ROWS), lambda k: (jnp.minimum(k + 1, steps - 1), 0, 0),
                         memory_space=pltpu.SMEM),
            pl.BlockSpec((SCATTER_ROWS, d), lambda k: (k, 0)),
            pl.BlockSpec((1, SCATTER_ROWS, 1), lambda k: (k, 0, 0)),
            pl.BlockSpec(gate_rows.shape, lambda k: (0, 0)),
            pl.BlockSpec(memory_space=pl.ANY),
        ],
        out_specs=pl.BlockSpec(memory_space=pl.ANY),
        out_shape=jax.ShapeDtypeStruct(x1.shape, x1.dtype),
        scratch_shapes=[pltpu.VMEM((SCATTER_BUFS, SCATTER_ROWS // SUBLANES, SUBLANES, d), F32),
                        pltpu.SemaphoreType.DMA((SCATTER_BUFS,)),
                        pltpu.SemaphoreType.DMA((SCATTER_BUFS,))],
        input_output_aliases={5: 0},
        compiler_params=_cparams(("arbitrary",)),
        name="scatter_add",
    )(idx3, idx3, ye, idx_flat.reshape(steps, SCATTER_ROWS, 1), gate_rows, x1)


def _rope_tables(t):
    half = HEAD_DIM // 2
    quarter = half // 2
    freqs = ROPE_THETA ** (-(jnp.arange(quarter, dtype=F32) / quarter))
    rows = t // GRID_W
    row_ids = jnp.repeat(jnp.arange(rows, dtype=jnp.int32), GRID_W).astype(F32)
    col_ids = jnp.tile(jnp.arange(GRID_W, dtype=jnp.int32), rows).astype(F32)
    ang_r = row_ids[:, None] * freqs[None, :]
    ang_c = col_ids[:, None] * freqs[None, :]
    z = jnp.zeros_like(ang_r)
    cos_t = jnp.concatenate([jnp.cos(ang_r)] * 2 + [jnp.cos(ang_c)] * 2, axis=1)
    sa_t = jnp.concatenate([-jnp.sin(ang_r), z, -jnp.sin(ang_c), z], axis=1)
    sb_t = jnp.concatenate([z, jnp.sin(ang_r), z, jnp.sin(ang_c)], axis=1)
    return cos_t, sa_t, sb_t


def kernel(x_prompt, x_sample, c_prompt, c_sample, rel_bias_table, w_ada, b_ada, g_norm_mix, g_norm_ffn, w_in, g_q_a, g_k_a, g_q_b, g_k_b, g_out_a, g_out_b, w_out, w_router, w_gate, w_up, w_down):
    d = D_MODEL
    nbp, nbs = c_prompt.shape[0], c_sample.shape[0]
    c_all = jnp.concatenate([c_prompt, c_sample, jnp.zeros((16 - nbp - nbs, d), F32)], axis=0)
    mod = _ada(c_all, w_ada[0], b_ada[0])
    mod3 = mod.reshape(16 * 6, 1, d)
    gt2_all = mod[:, 5 * d:6 * d]

    scale = HEAD_DIM ** -0.5
    gains = jnp.stack([g_q_a[0] * scale, g_k_a[0], g_q_b[0] * scale, g_k_b[0]], axis=0)
    w_in_bf = w_in[0].astype(BF16)
    w_out_bf = w_out[0].astype(BF16)
    w_router_pad = jnp.pad(w_router[0], ((0, 0), (0, HEAD_DIM - N_EXPERTS))).astype(BF16)
    wg, wu, wd = w_gate[0].astype(BF16), w_up[0].astype(BF16), w_down[0].astype(BF16)
    bias_tiles = _bias_tiles(rel_bias_table)
    gn1 = g_norm_mix[0].reshape(1, d)
    gn2 = g_norm_ffn[0].reshape(1, d)
    ga = g_out_a[0].reshape(1, WIDTH_A)
    gb = g_out_b[0].reshape(1, WIDTH_B)

    def run(x, boff):
        b, t, _ = x.shape
        cos_t, sa_t, sb_t = _rope_tables(t)
        proj = _inproj(x, mod3, boff, gn1, w_in_bf, gains, cos_t, sa_t, sb_t)
        oa = _attn_a(proj, bias_tiles)
        ob = _attn_b(proj)
        x1, h2, logits = _outproj(oa, ob, x, mod3, boff, ga, gb, gn2, w_out_bf, w_router_pad)
        n = b * t
        cap = EC_CAPACITY_FACTOR * n // N_EXPERTS
        lt = logits.reshape(n, -1)[:, :N_EXPERTS].T.reshape(N_EXPERTS, n // LANES, LANES)
        idx, g = _route(lt, cap)
        idx_flat = idx.reshape(N_EXPERTS * cap)
        ye = _ffn(idx.reshape(N_EXPERTS, cap), h2, wg, wu, wd, g.reshape(N_EXPERTS, cap))
        out = _scatter_add(idx_flat, ye.reshape(N_EXPERTS * cap, d), gt2_all, x1.reshape(n, d),
                           cap=cap, tok_shift=int(math.log2(t)), boff=boff)
        return out.reshape(b, t, d)

    return (run(x_prompt, 0), run(x_sample, nbp))
```

```python
import functools
import math

import jax
import jax.numpy as jnp
from jax import lax
from jax.experimental import pallas as pl
from jax.experimental.pallas import tpu as pltpu

F32 = jnp.float32
BF16 = jnp.bfloat16

D_MODEL = 2048
HEAD_DIM = 128
N_HEADS_A = 8
N_HEADS_B = 8
N_KV_B = 2
GQA_GROUP = N_HEADS_B // N_KV_B
WIDTH_A = N_HEADS_A * HEAD_DIM
WIDTH_B = N_HEADS_B * HEAD_DIM
KV_WIDTH_B = N_KV_B * HEAD_DIM
IN_COLS = 3 * WIDTH_A + WIDTH_B + 2 * KV_WIDTH_B
DIL_CONFIGS = ((128, 1), (512, 4), (2048, 16))
NUM_BUCKETS = 32
MAX_DISTANCE = 1024
GRID_W = 64
ROPE_THETA = 10000.0
N_EXPERTS = 16
EC_CAPACITY_FACTOR = 2
D_EXPERT = 2048
EPS = 1e-6
NEG_INF = -1e30

VMEM_LIMIT_V7X = 56 * 1024 * 1024
LANES = 128
SUBLANES = 8

COL_QA, COL_KA, COL_VA = 0, 8, 16
COL_QB, COL_KB, COL_VB = 24, 32, 34

IN_TN = 512
A_QB = 128
A_KB = 256
A_RADIUS = 64
A_PAD = A_RADIUS * 16
A_UNROLL = 8


def _cparams(sem):
    return pltpu.CompilerParams(dimension_semantics=sem, vmem_limit_bytes=VMEM_LIMIT_V7X)


def _ada_kernel(c_ref, w_ref, b_ref, o_ref):
    c = c_ref[...]
    s = c * (1.0 / (1.0 + jnp.exp(-c)))
    o_ref[...] = jnp.dot(s.astype(BF16), w_ref[...].astype(BF16),
                         preferred_element_type=F32) + b_ref[...]


def _ada(c_all, w_ada, b_ada):
    rows, d = c_all.shape
    n = w_ada.shape[1]
    tn = 1024
    return pl.pallas_call(
        _ada_kernel,
        grid=(n // tn,),
        in_specs=[pl.BlockSpec((rows, d), lambda j: (0, 0)),
                  pl.BlockSpec((d, tn), lambda j: (0, j)),
                  pl.BlockSpec((1, tn), lambda j: (0, j))],
        out_specs=pl.BlockSpec((rows, tn), lambda j: (0, j)),
        out_shape=jax.ShapeDtypeStruct((rows, n), F32),
        compiler_params=_cparams(("arbitrary",)),
        name="ada_mod",
    )(c_all, w_ada, b_ada.reshape(1, n))


def _head_norm(a, g):
    ms = jnp.mean(a * a, axis=-1, keepdims=True)
    return a * lax.rsqrt(ms + EPS) * g


def _inproj_kernel(x_ref, sc_ref, sh_ref, gn_ref, w_ref, gains_ref, cos_ref, sa_ref, sb_ref,
                   o_ref, h_scr):
    j = pl.program_id(2)

    @pl.when(j == 0)
    def _():
        x = x_ref[0]
        ms = jnp.mean(x * x, axis=-1, keepdims=True)
        y = x * lax.rsqrt(ms + EPS) * gn_ref[...]
        h_scr[...] = (y * (1.0 + sc_ref[0]) + sh_ref[0]).astype(BF16)

    acc = jnp.dot(h_scr[...], w_ref[...], preferred_element_type=F32)

    def rope(a):
        return (a * cos_ref[...] + pltpu.roll(a, 96, 1) * sa_ref[...]
                + pltpu.roll(a, 32, 1) * sb_ref[...])

    def store_heads(fn, first, count):
        for hh in range(first, first + count):
            sl = slice(hh * HEAD_DIM, (hh + 1) * HEAD_DIM)
            o_ref[0, :, sl] = fn(acc[:, sl]).astype(BF16)

    @pl.when(j < 2)
    def _():
        store_heads(lambda a: _head_norm(a, gains_ref[0:1, :]), 0, 4)

    @pl.when((j >= 2) & (j < 4))
    def _():
        store_heads(lambda a: _head_norm(a, gains_ref[1:2, :]), 0, 4)

    @pl.when((j >= 4) & (j < 6))
    def _():
        o_ref[0] = acc.astype(BF16)

    @pl.when((j >= 6) & (j < 8))
    def _():
        store_heads(lambda a: rope(_head_norm(a, gains_ref[2:3, :])), 0, 4)

    @pl.when(j == 8)
    def _():
        store_heads(lambda a: rope(_head_norm(a, gains_ref[3:4, :])), 0, 2)
        store_heads(lambda a: a, 2, 2)


def _inproj(x, mod3, boff, g_norm, w_in_bf, gains, cos_t, sa_t, sb_t):
    b, t, d = x.shape
    tm = 1024
    nj = IN_COLS // IN_TN
    return pl.pallas_call(
        _inproj_kernel,
        grid=(b, t // tm, nj),
        in_specs=[
            pl.BlockSpec((1, tm, d), lambda bi, ti, j: (bi, ti, 0)),
            pl.BlockSpec((1, 1, d), lambda bi, ti, j: ((bi + boff) * 6 + 1, 0, 0)),
            pl.BlockSpec((1, 1, d), lambda bi, ti, j: ((bi + boff) * 6 + 0, 0, 0)),
            pl.BlockSpec((1, d), lambda bi, ti, j: (0, 0)),
            pl.BlockSpec((d, IN_TN), lambda bi, ti, j: (0, j)),
            pl.BlockSpec((4, HEAD_DIM), lambda bi, ti, j: (0, 0)),
            pl.BlockSpec((tm, HEAD_DIM), lambda bi, ti, j: (ti, 0)),
            pl.BlockSpec((tm, HEAD_DIM), lambda bi, ti, j: (ti, 0)),
            pl.BlockSpec((tm, HEAD_DIM), lambda bi, ti, j: (ti, 0)),
        ],
        out_specs=pl.BlockSpec((1, tm, IN_TN), lambda bi, ti, j: (bi, ti, j)),
        out_shape=jax.ShapeDtypeStruct((b, t, IN_COLS), BF16),
        scratch_shapes=[pltpu.VMEM((tm, d), BF16)],
        compiler_params=_cparams(("arbitrary", "arbitrary", "arbitrary")),
        name="inproj",
    )(x, mod3, mod3, g_norm, w_in_bf, gains, cos_t, sa_t, sb_t)


def _attn_a_kernel(q_ref, k_ref, v_ref, bias_ref, o_ref, qf, kf, vf, acc, mm, ll, *, t):
    zpad = jnp.zeros((A_PAD, HEAD_DIM), F32)
    kf[0:A_PAD, :] = zpad
    vf[0:A_PAD, :] = zpad
    kf[A_PAD + t:A_PAD + t + A_PAD, :] = zpad
    vf[A_PAD + t:A_PAD + t + A_PAD, :] = zpad
    kf[A_PAD:A_PAD + t, :] = k_ref[0].astype(F32)
    vf[A_PAD:A_PAD + t, :] = v_ref[0].astype(F32)
    qf[...] = q_ref[0].astype(F32)
    ones = jnp.ones((A_KB, HEAD_DIM), BF16)

    for bi, (_, dil) in enumerate(DIL_CONFIGS):
        sub_len = t // dil
        nmb = sub_len // A_QB
        shift = int(math.log2(nmb))

        def rows(start, size, dil=dil):
            return pl.ds(start, size) if dil == 1 else pl.ds(start, size, stride=dil)

        def body(idx, carry, bi=bi, dil=dil, sub_len=sub_len, nmb=nmb, shift=shift, rows=rows):
            rho = lax.shift_right_logical(idx, shift)
            mb = lax.bitwise_and(idx, nmb - 1)
            qstart = rho + mb * (A_QB * dil)
            kstart = A_PAD + qstart - A_RADIUS * dil
            q = qf[rows(qstart, A_QB), :].astype(BF16)
            k = kf[rows(kstart, A_KB), :].astype(BF16)
            v = vf[rows(kstart, A_KB), :].astype(BF16)
            s = lax.dot_general(q, k, (((1,), (1,)), ((), ())), preferred_element_type=F32)
            s = s + bias_ref[bi, 0]
            kidx = mb * A_QB - A_RADIUS + lax.broadcasted_iota(jnp.int32, (1, A_KB), 1)
            s = jnp.where((kidx >= 0) & (kidx < sub_len), s, NEG_INF)
            mblk = jnp.max(s, axis=-1, keepdims=True)
            v1 = jnp.concatenate([v, ones], axis=1)
            p = jnp.exp(s - mblk).astype(BF16)
            pv = jnp.dot(p, v1, preferred_element_type=F32)
            acc[bi, rows(qstart, A_QB), :] = pv[:, :HEAD_DIM]
            ll[bi, rows(qstart, A_QB), :] = pv[:, HEAD_DIM:]
            mm[bi, rows(qstart, A_QB), :] = jnp.broadcast_to(mblk, (A_QB, HEAD_DIM))
            return carry

        lax.fori_loop(0, dil * nmb, body, 0, unroll=A_UNROLL)

    m0, m1, m2 = mm[0], mm[1], mm[2]
    mtop = jnp.maximum(m0, jnp.maximum(m1, m2))
    w0, w1, w2 = jnp.exp(m0 - mtop), jnp.exp(m1 - mtop), jnp.exp(m2 - mtop)
    num = w0 * acc[0] + w1 * acc[1] + w2 * acc[2]
    den = w0 * ll[0] + w1 * ll[1] + w2 * ll[2]
    o_ref[0] = (num / den).astype(BF16)


def _attn_a(proj, bias_tiles):
    b, t, _ = proj.shape
    kern = functools.partial(_attn_a_kernel, t=t)
    return pl.pallas_call(
        kern,
        grid=(b, N_HEADS_A),
        in_specs=[
            pl.BlockSpec((1, t, HEAD_DIM), lambda bi, h: (bi, 0, COL_QA + h)),
            pl.BlockSpec((1, t, HEAD_DIM), lambda bi, h: (bi, 0, COL_KA + h)),
            pl.BlockSpec((1, t, HEAD_DIM), lambda bi, h: (bi, 0, COL_VA + h)),
            pl.BlockSpec((3, 1, A_QB, A_KB), lambda bi, h: (0, h, 0, 0)),
        ],
        out_specs=pl.BlockSpec((1, t, HEAD_DIM), lambda bi, h: (bi, 0, h)),
        out_shape=jax.ShapeDtypeStruct((b, t, WIDTH_A), BF16),
        scratch_shapes=[
            pltpu.VMEM((t, HEAD_DIM), F32),
            pltpu.VMEM((t + 2 * A_PAD, HEAD_DIM), F32),
            pltpu.VMEM((t + 2 * A_PAD, HEAD_DIM), F32),
            pltpu.VMEM((len(DIL_CONFIGS), t, HEAD_DIM), F32),
            pltpu.VMEM((len(DIL_CONFIGS), t, HEAD_DIM), F32),
            pltpu.VMEM((len(DIL_CONFIGS), t, HEAD_DIM), F32),
        ],
        compiler_params=_cparams(("arbitrary", "arbitrary")),
        name="attn_dilated",
    )(proj, proj, proj, bias_tiles)


def _t5_bucket(rel):
    nb = NUM_BUCKETS // 2
    max_exact = nb // 2
    sign_off = jnp.where(rel > 0, nb, 0)
    n = jnp.abs(rel)
    nf = jnp.maximum(n, 1).astype(F32)
    large = max_exact + (jnp.log(nf / max_exact) / math.log(MAX_DISTANCE / max_exact)
                         * (nb - max_exact)).astype(jnp.int32)
    large = jnp.minimum(large, nb - 1)
    return sign_off + jnp.where(n < max_exact, n, large)


def _bias_tiles(rel_bias_table):
    qi = jnp.arange(A_QB, dtype=jnp.int32)[:, None]
    kj = jnp.arange(A_KB, dtype=jnp.int32)[None, :]
    rel = kj - A_RADIUS - qi
    tiles = []
    for _, dil in DIL_CONFIGS:
        onehot = (_t5_bucket(rel * dil)[..., None]
                  == jnp.arange(NUM_BUCKETS, dtype=jnp.int32)).astype(F32)
        bias = jnp.einsum('qkn,nh->hqk', onehot, rel_bias_table.astype(F32),
                          precision=lax.Precision.HIGHEST)
        tiles.append(jnp.where((jnp.abs(rel) <= A_RADIUS)[None], bias, NEG_INF))
    return jnp.stack(tiles, axis=0)


def _attn_b_kernel(q_ref, k_ref, v_ref, o_ref, v1_scr, *, tq):
    @pl.when(pl.program_id(2) == 0)
    def _():
        v1_scr[:, 0:HEAD_DIM] = v_ref[0]
        v1_scr[:, HEAD_DIM:2 * HEAD_DIM] = jnp.ones(v_ref.shape[1:], BF16)

    q = q_ref[0]
    qs = jnp.concatenate([q[:, i * HEAD_DIM:(i + 1) * HEAD_DIM] for i in range(GQA_GROUP)], axis=0)
    s = lax.dot_general(qs, k_ref[0], (((1,), (1,)), ((), ())), preferred_element_type=F32)
    m = jnp.max(s, axis=-1, keepdims=True)
    p = jnp.exp(s - m).astype(BF16)
    pv = jnp.dot(p, v1_scr[...], preferred_element_type=F32)
    o = pv[:, :HEAD_DIM] / pv[:, HEAD_DIM:]
    o_ref[0] = jnp.concatenate([o[i * tq:(i + 1) * tq] for i in range(GQA_GROUP)],
                               axis=1).astype(BF16)


def _attn_b(proj):
    b, t, _ = proj.shape
    tq = 128
    gw = GQA_GROUP * HEAD_DIM
    return pl.pallas_call(
        functools.partial(_attn_b_kernel, tq=tq),
        grid=(b, N_KV_B, t // tq),
        in_specs=[
            pl.BlockSpec((1, tq, gw), lambda bi, g, qi: (bi, qi, COL_QB // GQA_GROUP + g)),
            pl.BlockSpec((1, t, HEAD_DIM), lambda bi, g, qi: (bi, 0, COL_KB + g)),
            pl.BlockSpec((1, t, HEAD_DIM), lambda bi, g, qi: (bi, 0, COL_VB + g)),
        ],
        out_specs=pl.BlockSpec((1, tq, gw), lambda bi, g, qi: (bi, qi, g)),
        out_shape=jax.ShapeDtypeStruct((b, t, WIDTH_B), BF16),
        scratch_shapes=[pltpu.VMEM((t, 2 * HEAD_DIM), BF16)],
        compiler_params=_cparams(("arbitrary", "arbitrary", "arbitrary")),
        name="attn_gqa",
    )(proj, proj, proj)


def _outproj_kernel(oa_ref, ob_ref, x_ref, gt_ref, sc_ref, sh_ref, ga_ref, gb_ref, gn_ref,
                    w_ref, wr_ref, x1_ref, h2_ref, lg_ref):
    def wide_norm(o_ref_, g_ref_):
        o = o_ref_[0].astype(F32)
        ms = jnp.mean(o * o, axis=-1, keepdims=True)
        return (o * lax.rsqrt(ms + EPS) * g_ref_[...]).astype(BF16)

    na = wide_norm(oa_ref, ga_ref)
    nb = wide_norm(ob_ref, gb_ref)
    mix = (jnp.dot(na, w_ref[0:WIDTH_A, :], preferred_element_type=F32)
           + jnp.dot(nb, w_ref[WIDTH_A:WIDTH_A + WIDTH_B, :], preferred_element_type=F32))
    x1 = x_ref[0] + gt_ref[0] * mix
    x1_ref[0] = x1
    ms = jnp.mean(x1 * x1, axis=-1, keepdims=True)
    h2 = (x1 * lax.rsqrt(ms + EPS) * gn_ref[...]) * (1.0 + sc_ref[0]) + sh_ref[0]
    h2b = h2.astype(BF16)
    lg_ref[0] = jnp.dot(h2b, wr_ref[...], preferred_element_type=F32)
    tm, dd = h2.shape
    half = dd // 2
    rounded = h2b.astype(F32)
    lo = lax.bitcast_convert_type(rounded[:, :half], jnp.uint32)
    hi = lax.bitcast_convert_type(rounded[:, half:], jnp.uint32)
    packed = (hi & jnp.uint32(0xFFFF0000)) | (lo >> 16)
    for s in range(half // LANES):
        h2_ref[pl.ds(s, tm, stride=half // LANES), :] = packed[:, s * LANES:(s + 1) * LANES]


def _outproj(oa, ob, x, mod3, boff, g_out_a, g_out_b, g_norm_ffn, w_out_bf, w_router_pad):
    b, t, d = x.shape
    tm = 512
    nr = w_router_pad.shape[1]
    slab = d // 2 // LANES
    row = lambda k: (lambda bi, ti: ((bi + boff) * 6 + k, 0, 0))
    return pl.pallas_call(
        _outproj_kernel,
        grid=(b, t // tm),
        in_specs=[
            pl.BlockSpec((1, tm, WIDTH_A), lambda bi, ti: (bi, ti, 0)),
            pl.BlockSpec((1, tm, WIDTH_B), lambda bi, ti: (bi, ti, 0)),
            pl.BlockSpec((1, tm, d), lambda bi, ti: (bi, ti, 0)),
            pl.BlockSpec((1, 1, d), row(2)),
            pl.BlockSpec((1, 1, d), row(4)),
            pl.BlockSpec((1, 1, d), row(3)),
            pl.BlockSpec((1, WIDTH_A), lambda bi, ti: (0, 0)),
            pl.BlockSpec((1, WIDTH_B), lambda bi, ti: (0, 0)),
            pl.BlockSpec((1, d), lambda bi, ti: (0, 0)),
            pl.BlockSpec((WIDTH_A + WIDTH_B, d), lambda bi, ti: (0, 0)),
            pl.BlockSpec((d, nr), lambda bi, ti: (0, 0)),
        ],
        out_specs=[
            pl.BlockSpec((1, tm, d), lambda bi, ti: (bi, ti, 0)),
            pl.BlockSpec((tm * slab, LANES), lambda bi, ti: (bi * (t // tm) + ti, 0)),
            pl.BlockSpec((1, tm, nr), lambda bi, ti: (bi, ti, 0)),
        ],
        out_shape=[jax.ShapeDtypeStruct((b, t, d), F32),
                   jax.ShapeDtypeStruct((b * t * slab, LANES), jnp.uint32),
                   jax.ShapeDtypeStruct((b, t, nr), F32)],
        compiler_params=_cparams(("arbitrary", "arbitrary")),
        name="outproj",
    )(oa, ob, x, mod3, mod3, mod3, g_out_a, g_out_b, g_norm_ffn, w_out_bf, w_router_pad)


def _ffn_kernel(idx_ref, idx_next_ref, h_ref, wg_ref, wu_ref, wd_ref, g_ref, o_ref,
                xslab, xb, sems, *, tm, slab):
    f = pl.program_id(2)
    k = pl.program_id(0) * pl.num_programs(1) + pl.program_id(1)
    nk = pl.num_programs(0) * pl.num_programs(1)
    slot = lax.rem(k, 2)

    def issue_gather(ids_ref, sl):
        def body(r, carry):
            src0 = pl.multiple_of(ids_ref[0, 0, r] * slab, slab)
            dst0 = pl.multiple_of(r * slab, slab)
            pltpu.make_async_copy(h_ref.at[pl.ds(src0, slab), :],
                                  xslab.at[sl, pl.ds(dst0, slab), :], sems.at[sl]).start()
            return carry
        lax.fori_loop(0, tm, body, 0, unroll=8)

    @pl.when(f == 0)
    def _():
        @pl.when(k == 0)
        def _():
            issue_gather(idx_ref, slot)

        @pl.when(k + 1 < nk)
        def _():
            issue_gather(idx_next_ref, 1 - slot)

        pltpu.make_async_copy(h_ref.at[pl.ds(0, tm * slab), :], xslab.at[slot],
                              sems.at[slot]).wait()
        half = slab * LANES
        for s in range(slab):
            w = xslab[slot, pl.ds(s, tm, stride=slab), :]
            xb[:, s * LANES:(s + 1) * LANES] = lax.bitcast_convert_type(w << 16, F32).astype(BF16)
            xb[:, half + s * LANES:half + (s + 1) * LANES] = lax.bitcast_convert_type(
                w & jnp.uint32(0xFFFF0000), F32).astype(BF16)

    x = xb[...]
    a = jnp.dot(x, wg_ref[0], preferred_element_type=F32)
    u = jnp.dot(x, wu_ref[0], preferred_element_type=F32)
    hmid = (a * (1.0 / (1.0 + jnp.exp(-a))) * u).astype(BF16)
    y = jnp.dot(hmid, wd_ref[0], preferred_element_type=F32)

    @pl.when(f == 0)
    def _():
        o_ref[0] = y

    @pl.when(f > 0)
    def _():
        o_ref[0] += y

    @pl.when(f == pl.num_programs(2) - 1)
    def _():
        o_ref[0] = o_ref[0] * g_ref[0]


def _ffn(idx, h2slab, wg, wu, wd, g):
    e, cap = idx.shape
    d = wg.shape[1]
    slab = d // 2 // LANES
    fdim = wg.shape[2]
    tm = min(cap, 1024)
    tf = 512
    mt = cap // tm
    nk = e * mt
    idx3 = idx.reshape(nk, 1, tm)
    return pl.pallas_call(
        functools.partial(_ffn_kernel, tm=tm, slab=slab),
        grid=(e, mt, fdim // tf),
        in_specs=[
            pl.BlockSpec((1, 1, tm), lambda ei, mi, fi: (ei * mt + mi, 0, 0),
                         memory_space=pltpu.SMEM),
            pl.BlockSpec((1, 1, tm), lambda ei, mi, fi: (jnp.minimum(ei * mt + mi + 1, nk - 1), 0, 0),
                         memory_space=pltpu.SMEM),
            pl.BlockSpec(memory_space=pl.ANY),
            pl.BlockSpec((1, d, tf), lambda ei, mi, fi: (ei, 0, fi)),
            pl.BlockSpec((1, d, tf), lambda ei, mi, fi: (ei, 0, fi)),
            pl.BlockSpec((1, tf, d), lambda ei, mi, fi: (ei, fi, 0)),
            pl.BlockSpec((1, tm, 1), lambda ei, mi, fi: (ei, mi, 0)),
        ],
        out_specs=pl.BlockSpec((1, tm, d), lambda ei, mi, fi: (ei, mi, 0)),
        out_shape=jax.ShapeDtypeStruct((e, cap, d), F32),
        scratch_shapes=[pltpu.VMEM((2, tm * slab, LANES), jnp.uint32),
                        pltpu.VMEM((tm, d), BF16),
                        pltpu.SemaphoreType.DMA((2,))],
        compiler_params=_cparams(("arbitrary", "arbitrary", "arbitrary")),
        name="expert_ffn",
    )(idx3, idx3, h2slab, wg, wu, wd, g.reshape(e, cap, 1))


def _prefix_counts(mask_f32, upper, lower):
    within = jnp.dot(mask_f32.astype(BF16), upper, preferred_element_type=F32)
    tot = jnp.broadcast_to(within[:, LANES - 1:LANES], within.shape)
    offs = jnp.dot(lower, tot.astype(BF16), preferred_element_type=F32)
    return within, offs


def _route_kernel(lt_ref, idx_ref, g_ref, aff_scr, *, cap):
    e = pl.program_id(0)
    nchunk = lt_ref.shape[1]

    @pl.when(e == 0)
    def _():
        l = lt_ref[...]
        ex = jnp.exp(l - jnp.max(l, axis=0, keepdims=True))
        aff_scr[...] = ex / jnp.sum(ex, axis=0, keepdims=True)

    a = aff_scr[e]
    bits = lax.bitcast_convert_type(a, jnp.int32)
    capf = jnp.float32(cap)

    def count(m):
        return jnp.sum(m.astype(F32), axis=(0, 1), keepdims=True)

    def bit_step(i, thr):
        cand = thr | lax.shift_left(jnp.int32(1), 30 - i)
        return jnp.where(count(bits >= cand) >= capf, cand, thr)

    thr = lax.fori_loop(0, 31, bit_step, jnp.zeros((1, 1), jnp.int32))

    ri = lax.broadcasted_iota(jnp.int32, (LANES, LANES), 0)
    ci = lax.broadcasted_iota(jnp.int32, (LANES, LANES), 1)
    upper = (ri <= ci).astype(BF16)
    rc = lax.broadcasted_iota(jnp.int32, (nchunk, nchunk), 0)
    cc = lax.broadcasted_iota(jnp.int32, (nchunk, nchunk), 1)
    lower = (cc < rc).astype(BF16)

    gt = bits > thr
    eq = bits == thr
    need = capf - count(gt)
    eq_within, eq_offs = _prefix_counts(eq.astype(F32), upper, lower)
    sel = gt | (eq & ((eq_within + eq_offs) <= need))
    within, offs = _prefix_counts(sel.astype(F32), upper, lower)

    offs_col = offs[:, 0:1]
    ends_col = offs_col + within[:, LANES - 1:LANES]
    slot = lax.broadcasted_iota(jnp.int32, (1, cap), 1).astype(F32)
    cstar = jnp.sum((ends_col <= slot).astype(F32), axis=0, keepdims=True)
    chunk_oh = lax.broadcasted_iota(jnp.int32, (nchunk, cap), 0).astype(F32) == cstar
    rank = slot - jnp.sum(jnp.where(chunk_oh, offs_col, 0.0), axis=0, keepdims=True)
    oh = chunk_oh.astype(BF16)
    tdot = functools.partial(lax.dot_general, dimension_numbers=(((0,), (0,)), ((), ())),
                             preferred_element_type=F32)
    wsel = tdot(within.astype(BF16), oh)
    lstar = jnp.sum((wsel <= rank).astype(F32), axis=0, keepdims=True)
    idx_ref[0] = (cstar * LANES + lstar).astype(jnp.int32)

    a1 = a.astype(BF16)
    r1 = a - a1.astype(F32)
    a2 = r1.astype(BF16)
    a3 = (r1 - a2.astype(F32)).astype(BF16)
    asel = (tdot(a1, oh) + tdot(a2, oh)) + tdot(a3, oh)
    lane_oh = lax.broadcasted_iota(jnp.int32, (LANES, cap), 0).astype(F32) == lstar
    g_ref[0] = jnp.sum(jnp.where(lane_oh, asel, 0.0), axis=0, keepdims=True)


def _route(lt, cap):
    e, nchunk, _ = lt.shape
    return pl.pallas_call(
        functools.partial(_route_kernel, cap=cap),
        grid=(e,),
        in_specs=[pl.BlockSpec((e, nchunk, LANES), lambda ei: (0, 0, 0))],
        out_specs=[pl.BlockSpec((1, 1, cap), lambda ei: (ei, 0, 0)),
                   pl.BlockSpec((1, 1, cap), lambda ei: (ei, 0, 0))],
        out_shape=[jax.ShapeDtypeStruct((e, 1, cap), jnp.int32),
                   jax.ShapeDtypeStruct((e, 1, cap), F32)],
        scratch_shapes=[pltpu.VMEM((e, nchunk, LANES), F32)],
        compiler_params=_cparams(("arbitrary",)),
        name="route",
    )(lt)


SCATTER_ROWS = 512
SCATTER_BUFS = 3


def _scatter_kernel(idx_ref, idx_next_ref, ye_ref, idxv_ref, gate_ref, x1_ref, out_ref,
                    obuf, gsem, ssem, *, tiles_per_expert, tok_shift, boff):
    del x1_ref
    k = pl.program_id(0)
    nk = pl.num_programs(0)
    first = lax.rem(k, tiles_per_expert) == 0
    prev_first = lax.rem(k - 1, tiles_per_expert) == 0
    next_first = lax.rem(k + 1, tiles_per_expert) == 0
    slot = lax.rem(k, SCATTER_BUFS)

    def gather_copy(ref_idx, i, j, sl):
        return pltpu.make_async_copy(out_ref.at[pl.ds(ref_idx[0, 0, i * SUBLANES + j], 1), :],
                                     obuf.at[sl, i, pl.ds(j, 1), :], gsem.at[sl])

    def scatter_copy(ref_idx, i, j, sl):
        return pltpu.make_async_copy(obuf.at[sl, i, pl.ds(j, 1), :],
                                     out_ref.at[pl.ds(ref_idx[0, 0, i * SUBLANES + j], 1), :],
                                     ssem.at[sl])

    def wait_all(sem, sl):
        pltpu.make_async_copy(obuf.at[sl], obuf.at[sl], sem.at[sl]).wait()

    def wait_scatter(step):
        wait_all(ssem, lax.rem(step, SCATTER_BUFS))

    @pl.when(first & (k >= 1))
    def _():
        wait_scatter(k - 1)

    @pl.when((k >= 2) & (first | jnp.logical_not(prev_first)))
    def _():
        wait_scatter(k - 2)

    def issue_gather(step_idx_ref, sl):
        def body(i, carry):
            for j in range(SUBLANES):
                gather_copy(step_idx_ref, i, j, sl).start()
            return carry
        lax.fori_loop(0, SCATTER_ROWS // SUBLANES, body, 0, unroll=2)

    @pl.when(first)
    def _():
        issue_gather(idx_ref, slot)

    @pl.when((k + 1 < nk) & jnp.logical_not(next_first))
    def _():
        issue_gather(idx_next_ref, lax.rem(k + 1, SCATTER_BUFS))

    wait_all(gsem, slot)

    bid = lax.shift_right_logical(idxv_ref[0], tok_shift) + boff
    oh = (bid == lax.broadcasted_iota(jnp.int32, (1, gate_ref.shape[0]), 1)).astype(BF16)
    gt = gate_ref[...]
    g1 = gt.astype(BF16)
    r1 = gt - g1.astype(F32)
    g2 = r1.astype(BF16)
    g3 = (r1 - g2.astype(F32)).astype(BF16)
    dot = functools.partial(jnp.dot, preferred_element_type=F32)
    gate_rows = (dot(oh, g1) + dot(oh, g2)) + dot(oh, g3)
    upd = obuf[slot].reshape(ye_ref.shape) + gate_rows * ye_ref[...]
    obuf[slot] = upd.reshape(obuf.shape[1:])

    def issue_scatter(i, carry):
        for j in range(SUBLANES):
            scatter_copy(idx_ref, i, j, slot).start()
        return carry

    lax.fori_loop(0, SCATTER_ROWS // SUBLANES, issue_scatter, 0, unroll=2)

    @pl.when(k == nk - 1)
    def _():
        wait_scatter(k)
        wait_scatter(k - 1)


def _scatter_add(idx_flat, ye, gate_rows, x1, *, cap, tok_shift, boff):
    n_rows, d = ye.shape
    steps = n_rows // SCATTER_ROWS
    tiles_per_expert = cap // SCATTER_ROWS
    assert tiles_per_expert >= 2
    kern = functools.partial(_scatter_kernel, tiles_per_expert=tiles_per_expert,
                             tok_shift=tok_shift, boff=boff)
    idx3 = idx_flat.reshape(steps, 1, SCATTER_ROWS)
    return pl.pallas_call(
        kern,
        grid=(steps,),
        in_specs=[
            pl.BlockSpec((1, 1, SCATTER_ROWS), lambda k: (k, 0, 0), memory_space=pltpu.SMEM),
            pl.BlockSpec((1, 1, SCATTER_ROWS), lambda k: (jnp.minimum(k + 1, steps - 1), 0, 0),
                         memory_space=pltpu.SMEM),
            pl.BlockSpec((SCATTER_ROWS, d), lambda k: (k, 0)),
            pl.BlockSpec((1, SCATTER_ROWS, 1), lambda k: (k, 0, 0)),
            pl.BlockSpec(gate_rows.shape, lambda k: (0, 0)),
            pl.BlockSpec(memory_space=pl.ANY),
        ],
        out_specs=pl.BlockSpec(memory_space=pl.ANY),
        out_shape=jax.ShapeDtypeStruct(x1.shape, x1.dtype),
        scratch_shapes=[pltpu.VMEM((SCATTER_BUFS, SCATTER_ROWS // SUBLANES, SUBLANES, d), F32),
                        pltpu.SemaphoreType.DMA((SCATTER_BUFS,)),
                        pltpu.SemaphoreType.DMA((SCATTER_BUFS,))],
        input_output_aliases={5: 0},
        compiler_params=_cparams(("arbitrary",)),
        name="scatter_add",
    )(idx3, idx3, ye, idx_flat.reshape(steps, SCATTER_ROWS, 1), gate_rows, x1)


def _rope_tables(t):
    half = HEAD_DIM // 2
    quarter = half // 2
    freqs = ROPE_THETA ** (-(jnp.arange(quarter, dtype=F32) / quarter))
    rows = t // GRID_W
    row_ids = jnp.repeat(jnp.arange(rows, dtype=jnp.int32), GRID_W).astype(F32)
    col_ids = jnp.tile(jnp.arange(GRID_W, dtype=jnp.int32), rows).astype(F32)
    ang_r = row_ids[:, None] * freqs[None, :]
    ang_c = col_ids[:, None] * freqs[None, :]
    z = jnp.zeros_like(ang_r)
    cos_t = jnp.concatenate([jnp.cos(ang_r)] * 2 + [jnp.cos(ang_c)] * 2, axis=1)
    sa_t = jnp.concatenate([-jnp.sin(ang_r), z, -jnp.sin(ang_c), z], axis=1)
    sb_t = jnp.concatenate([z, jnp.sin(ang_r), z, jnp.sin(ang_c)], axis=1)
    return cos_t, sa_t, sb_t


def kernel(x_prompt, x_sample, c_prompt, c_sample, rel_bias_table, w_ada, b_ada, g_norm_mix, g_norm_ffn, w_in, g_q_a, g_k_a, g_q_b, g_k_b, g_out_a, g_out_b, w_out, w_router, w_gate, w_up, w_down):
    d = D_MODEL
    nbp, nbs = c_prompt.shape[0], c_sample.shape[0]
    c_all = jnp.concatenate([c_prompt, c_sample, jnp.zeros((16 - nbp - nbs, d), F32)], axis=0)
    mod = _ada(c_all, w_ada[0], b_ada[0])
    mod3 = mod.reshape(16 * 6, 1, d)
    gt2_all = mod[:, 5 * d:6 * d]

    scale = HEAD_DIM ** -0.5
    gains = jnp.stack([g_q_a[0] * scale, g_k_a[0], g_q_b[0] * scale, g_k_b[0]], axis=0)
    w_in_bf = w_in[0].astype(BF16)
    w_out_bf = w_out[0].astype(BF16)
    w_router_pad = jnp.pad(w_router[0], ((0, 0), (0, HEAD_DIM - N_EXPERTS))).astype(BF16)
    wg, wu, wd = w_gate[0].astype(BF16), w_up[0].astype(BF16), w_down[0].astype(BF16)
    bias_tiles = _bias_tiles(rel_bias_table)
    gn1 = g_norm_mix[0].reshape(1, d)
    gn2 = g_norm_ffn[0].reshape(1, d)
    ga = g_out_a[0].reshape(1, WIDTH_A)
    gb = g_out_b[0].reshape(1, WIDTH_B)

    def run(x, boff):
        b, t, _ = x.shape
        cos_t, sa_t, sb_t = _rope_tables(t)
        proj = _inproj(x, mod3, boff, gn1, w_in_bf, gains, cos_t, sa_t, sb_t)
        oa = _attn_a(proj, bias_tiles)
        ob = _attn_b(proj)
        x1, h2, logits = _outproj(oa, ob, x, mod3, boff, ga, gb, gn2, w_out_bf, w_router_pad)
        n = b * t
        cap = EC_CAPACITY_FACTOR * n // N_EXPERTS
        lt = logits.reshape(n, -1)[:, :N_EXPERTS].T.reshape(N_EXPERTS, n // LANES, LANES)
        idx, g = _route(lt, cap)
        idx_flat = idx.reshape(N_EXPERTS * cap)
        ye = _ffn(idx.reshape(N_EXPERTS, cap), h2, wg, wu, wd, g.reshape(N_EXPERTS, cap))
        out = _scatter_add(idx_flat, ye.reshape(N_EXPERTS * cap, d), gt2_all, x1.reshape(n, d),
                           cap=cap, tok_shift=int(math.log2(t)), boff=boff)
        return out.reshape(b, t, d)

    return (run(x_prompt, 0), run(x_sample, nbp))
```

```python
import functools
import math

import jax
import jax.numpy as jnp
from jax import lax
from jax.experimental import pallas as pl
from jax.experimental.pallas import tpu as pltpu

F32 = jnp.float32
BF16 = jnp.bfloat16

D_MODEL = 2048
HEAD_DIM = 128
N_HEADS_A = 8
N_HEADS_B = 8
N_KV_B = 2
GQA_GROUP = N_HEADS_B // N_KV_B
WIDTH_A = N_HEADS_A * HEAD_DIM
WIDTH_B = N_HEADS_B * HEAD_DIM
KV_WIDTH_B = N_KV_B * HEAD_DIM
IN_COLS = 3 * WIDTH_A + WIDTH_B + 2 * KV_WIDTH_B
DIL_CONFIGS = ((128, 1), (512, 4), (2048, 16))
NUM_BUCKETS = 32
MAX_DISTANCE = 1024
GRID_W = 64
ROPE_THETA = 10000.0
N_EXPERTS = 16
EC_CAPACITY_FACTOR = 2
D_EXPERT = 2048
EPS = 1e-6
NEG_INF = -1e30

VMEM_LIMIT_V7X = 56 * 1024 * 1024
LANES = 128
SUBLANES = 8

COL_QA, COL_KA, COL_VA = 0, 8, 16
COL_QB, COL_KB, COL_VB = 24, 32, 34

IN_TN = 512
A_QB = 128
A_KB = 256
A_RADIUS = 64
A_PAD = A_RADIUS * 16
A_UNROLL = 8
B_SUB = 128


def _cparams(sem):
    return pltpu.CompilerParams(dimension_semantics=sem, vmem_limit_bytes=VMEM_LIMIT_V7X)


def _ada_kernel(c_ref, w_ref, b_ref, o_ref):
    c = c_ref[...]
    s = c * (1.0 / (1.0 + jnp.exp(-c)))
    o_ref[...] = jnp.dot(s.astype(BF16), w_ref[...].astype(BF16),
                         preferred_element_type=F32) + b_ref[...]


def _ada(c_all, w_ada, b_ada):
    rows, d = c_all.shape
    n = w_ada.shape[1]
    tn = 1024
    return pl.pallas_call(
        _ada_kernel,
        grid=(n // tn,),
        in_specs=[pl.BlockSpec((rows, d), lambda j: (0, 0)),
                  pl.BlockSpec((d, tn), lambda j: (0, j)),
                  pl.BlockSpec((1, tn), lambda j: (0, j))],
        out_specs=pl.BlockSpec((rows, tn), lambda j: (0, j)),
        out_shape=jax.ShapeDtypeStruct((rows, n), F32),
        compiler_params=_cparams(("arbitrary",)),
        name="ada_mod",
    )(c_all, w_ada, b_ada.reshape(1, n))


def _head_norm(a, g):
    ms = jnp.mean(a * a, axis=-1, keepdims=True)
    return a * lax.rsqrt(ms + EPS) * g


def _inproj_kernel(x_ref, sc_ref, sh_ref, gn_ref, w_ref, gains_ref, cos_ref, sa_ref, sb_ref,
                   o_ref, h_scr):
    j = pl.program_id(2)

    @pl.when(j == 0)
    def _():
        x = x_ref[0]
        ms = jnp.mean(x * x, axis=-1, keepdims=True)
        y = x * lax.rsqrt(ms + EPS) * gn_ref[...]
        h_scr[...] = (y * (1.0 + sc_ref[0]) + sh_ref[0]).astype(BF16)

    acc = jnp.dot(h_scr[...], w_ref[...], preferred_element_type=F32)

    def rope(a):
        return (a * cos_ref[...] + pltpu.roll(a, 96, 1) * sa_ref[...]
                + pltpu.roll(a, 32, 1) * sb_ref[...])

    def store_heads(fn, first, count):
        for hh in range(first, first + count):
            sl = slice(hh * HEAD_DIM, (hh + 1) * HEAD_DIM)
            o_ref[0, :, sl] = fn(acc[:, sl]).astype(BF16)

    @pl.when(j < 2)
    def _():
        store_heads(lambda a: _head_norm(a, gains_ref[0:1, :]), 0, 4)

    @pl.when((j >= 2) & (j < 4))
    def _():
        store_heads(lambda a: _head_norm(a, gains_ref[1:2, :]), 0, 4)

    @pl.when((j >= 4) & (j < 6))
    def _():
        o_ref[0] = acc.astype(BF16)

    @pl.when((j >= 6) & (j < 8))
    def _():
        store_heads(lambda a: rope(_head_norm(a, gains_ref[2:3, :])), 0, 4)

    @pl.when(j == 8)
    def _():
        store_heads(lambda a: rope(_head_norm(a, gains_ref[3:4, :])), 0, 2)
        store_heads(lambda a: a, 2, 2)


def _inproj(x, mod3, boff, g_norm, w_in_bf, gains, cos_t, sa_t, sb_t):
    b, t, d = x.shape
    tm = 1024
    nj = IN_COLS // IN_TN
    return pl.pallas_call(
        _inproj_kernel,
        grid=(b, t // tm, nj),
        in_specs=[
            pl.BlockSpec((1, tm, d), lambda bi, ti, j: (bi, ti, 0)),
            pl.BlockSpec((1, 1, d), lambda bi, ti, j: ((bi + boff) * 6 + 1, 0, 0)),
            pl.BlockSpec((1, 1, d), lambda bi, ti, j: ((bi + boff) * 6 + 0, 0, 0)),
            pl.BlockSpec((1, d), lambda bi, ti, j: (0, 0)),
            pl.BlockSpec((d, IN_TN), lambda bi, ti, j: (0, j)),
            pl.BlockSpec((4, HEAD_DIM), lambda bi, ti, j: (0, 0)),
            pl.BlockSpec((tm, HEAD_DIM), lambda bi, ti, j: (ti, 0)),
            pl.BlockSpec((tm, HEAD_DIM), lambda bi, ti, j: (ti, 0)),
            pl.BlockSpec((tm, HEAD_DIM), lambda bi, ti, j: (ti, 0)),
        ],
        out_specs=pl.BlockSpec((1, tm, IN_TN), lambda bi, ti, j: (bi, ti, j)),
        out_shape=jax.ShapeDtypeStruct((b, t, IN_COLS), BF16),
        scratch_shapes=[pltpu.VMEM((tm, d), BF16)],
        compiler_params=_cparams(("arbitrary", "arbitrary", "arbitrary")),
        name="inproj",
    )(x, mod3, mod3, g_norm, w_in_bf, gains, cos_t, sa_t, sb_t)


def _attn_a_kernel(q_ref, k_ref, v_ref, bias_ref, o_ref, qf, kf, vf, acc, mm, ll, *, t):
    zpad = jnp.zeros((A_PAD, HEAD_DIM), F32)
    kf[0:A_PAD, :] = zpad
    vf[0:A_PAD, :] = zpad
    kf[A_PAD + t:A_PAD + t + A_PAD, :] = zpad
    vf[A_PAD + t:A_PAD + t + A_PAD, :] = zpad
    kf[A_PAD:A_PAD + t, :] = k_ref[0].astype(F32)
    vf[A_PAD:A_PAD + t, :] = v_ref[0].astype(F32)
    qf[...] = q_ref[0].astype(F32)
    ones = jnp.ones((A_KB, HEAD_DIM), BF16)

    for bi, (_, dil) in enumerate(DIL_CONFIGS):
        sub_len = t // dil
        nmb = sub_len // A_QB
        shift = int(math.log2(nmb))

        def rows(start, size, dil=dil):
            return pl.ds(start, size) if dil == 1 else pl.ds(start, size, stride=dil)

        def body(idx, carry, bi=bi, dil=dil, sub_len=sub_len, nmb=nmb, shift=shift, rows=rows):
            rho = lax.shift_right_logical(idx, shift)
            mb = lax.bitwise_and(idx, nmb - 1)
            qstart = rho + mb * (A_QB * dil)
            kstart = A_PAD + qstart - A_RADIUS * dil
            q = qf[rows(qstart, A_QB), :].astype(BF16)
            k = kf[rows(kstart, A_KB), :].astype(BF16)
            v = vf[rows(kstart, A_KB), :].astype(BF16)
            s = lax.dot_general(q, k, (((1,), (1,)), ((), ())), preferred_element_type=F32)
            s = s + bias_ref[bi, 0]
            kidx = mb * A_QB - A_RADIUS + lax.broadcasted_iota(jnp.int32, (1, A_KB), 1)
            s = jnp.where((kidx >= 0) & (kidx < sub_len), s, NEG_INF)
            mblk = jnp.max(s, axis=-1, keepdims=True)
            v1 = jnp.concatenate([v, ones], axis=1)
            p = jnp.exp(s - mblk).astype(BF16)
            pv = jnp.dot(p, v1, preferred_element_type=F32)
            acc[bi, rows(qstart, A_QB), :] = pv[:, :HEAD_DIM]
            ll[bi, rows(qstart, A_QB), :] = pv[:, HEAD_DIM:]
            mm[bi, rows(qstart, A_QB), :] = jnp.broadcast_to(mblk, (A_QB, HEAD_DIM))
            return carry

        lax.fori_loop(0, dil * nmb, body, 0, unroll=A_UNROLL)

    m0, m1, m2 = mm[0], mm[1], mm[2]
    mtop = jnp.maximum(m0, jnp.maximum(m1, m2))
    w0, w1, w2 = jnp.exp(m0 - mtop), jnp.exp(m1 - mtop), jnp.exp(m2 - mtop)
    num = w0 * acc[0] + w1 * acc[1] + w2 * acc[2]
    den = w0 * ll[0] + w1 * ll[1] + w2 * ll[2]
    o_ref[0] = (num / den).astype(BF16)


def _attn_a(proj, bias_tiles):
    b, t, _ = proj.shape
    kern = functools.partial(_attn_a_kernel, t=t)
    return pl.pallas_call(
        kern,
        grid=(b, N_HEADS_A),
        in_specs=[
            pl.BlockSpec((1, t, HEAD_DIM), lambda bi, h: (bi, 0, COL_QA + h)),
            pl.BlockSpec((1, t, HEAD_DIM), lambda bi, h: (bi, 0, COL_KA + h)),
            pl.BlockSpec((1, t, HEAD_DIM), lambda bi, h: (bi, 0, COL_VA + h)),
            pl.BlockSpec((3, 1, A_QB, A_KB), lambda bi, h: (0, h, 0, 0)),
        ],
        out_specs=pl.BlockSpec((1, t, HEAD_DIM), lambda bi, h: (bi, 0, h)),
        out_shape=jax.ShapeDtypeStruct((b, t, WIDTH_A), BF16),
        scratch_shapes=[
            pltpu.VMEM((t, HEAD_DIM), F32),
            pltpu.VMEM((t + 2 * A_PAD, HEAD_DIM), F32),
            pltpu.VMEM((t + 2 * A_PAD, HEAD_DIM), F32),
            pltpu.VMEM((len(DIL_CONFIGS), t, HEAD_DIM), F32),
            pltpu.VMEM((len(DIL_CONFIGS), t, HEAD_DIM), F32),
            pltpu.VMEM((len(DIL_CONFIGS), t, HEAD_DIM), F32),
        ],
        compiler_params=_cparams(("arbitrary", "arbitrary")),
        name="attn_dilated",
    )(proj, proj, proj, bias_tiles)


def _t5_bucket(rel):
    nb = NUM_BUCKETS // 2
    max_exact = nb // 2
    sign_off = jnp.where(rel > 0, nb, 0)
    n = jnp.abs(rel)
    nf = jnp.maximum(n, 1).astype(F32)
    large = max_exact + (jnp.log(nf / max_exact) / math.log(MAX_DISTANCE / max_exact)
                         * (nb - max_exact)).astype(jnp.int32)
    large = jnp.minimum(large, nb - 1)
    return sign_off + jnp.where(n < max_exact, n, large)


def _bias_tiles(rel_bias_table):
    qi = jnp.arange(A_QB, dtype=jnp.int32)[:, None]
    kj = jnp.arange(A_KB, dtype=jnp.int32)[None, :]
    rel = kj - A_RADIUS - qi
    tiles = []
    for _, dil in DIL_CONFIGS:
        onehot = (_t5_bucket(rel * dil)[..., None]
                  == jnp.arange(NUM_BUCKETS, dtype=jnp.int32)).astype(F32)
        bias = jnp.einsum('qkn,nh->hqk', onehot, rel_bias_table.astype(F32),
                          precision=lax.Precision.HIGHEST)
        tiles.append(jnp.where((jnp.abs(rel) <= A_RADIUS)[None], bias, NEG_INF))
    return jnp.stack(tiles, axis=0)


def _attn_b_kernel(q_ref, k_ref, v_ref, o_ref, v1_scr, *, tq):
    @pl.when(pl.program_id(2) == 0)
    def _():
        v1_scr[:, 0:HEAD_DIM] = v_ref[0]
        v1_scr[:, HEAD_DIM:2 * HEAD_DIM] = jnp.ones(v_ref.shape[1:], BF16)

    for c in range(tq // B_SUB):
        q = q_ref[0, c * B_SUB:(c + 1) * B_SUB, :]
        qs = jnp.concatenate([q[:, i * HEAD_DIM:(i + 1) * HEAD_DIM] for i in range(GQA_GROUP)],
                             axis=0)
        s = lax.dot_general(qs, k_ref[0], (((1,), (1,)), ((), ())), preferred_element_type=F32)
        m = jnp.max(s, axis=-1, keepdims=True)
        p = jnp.exp(s - m).astype(BF16)
        pv = jnp.dot(p, v1_scr[...], preferred_element_type=F32)
        o = pv[:, :HEAD_DIM] / pv[:, HEAD_DIM:]
        o_ref[0, c * B_SUB:(c + 1) * B_SUB, :] = jnp.concatenate(
            [o[i * B_SUB:(i + 1) * B_SUB] for i in range(GQA_GROUP)], axis=1).astype(BF16)


def _attn_b(proj):
    b, t, _ = proj.shape
    tq = 4 * B_SUB
    gw = GQA_GROUP * HEAD_DIM
    return pl.pallas_call(
        functools.partial(_attn_b_kernel, tq=tq),
        grid=(b, N_KV_B, t // tq),
        in_specs=[
            pl.BlockSpec((1, tq, gw), lambda bi, g, qi: (bi, qi, COL_QB // GQA_GROUP + g)),
            pl.BlockSpec((1, t, HEAD_DIM), lambda bi, g, qi: (bi, 0, COL_KB + g)),
            pl.BlockSpec((1, t, HEAD_DIM), lambda bi, g, qi: (bi, 0, COL_VB + g)),
        ],
        out_specs=pl.BlockSpec((1, tq, gw), lambda bi, g, qi: (bi, qi, g)),
        out_shape=jax.ShapeDtypeStruct((b, t, WIDTH_B), BF16),
        scratch_shapes=[pltpu.VMEM((t, 2 * HEAD_DIM), BF16)],
        compiler_params=_cparams(("arbitrary", "arbitrary", "arbitrary")),
        name="attn_gqa",
    )(proj, proj, proj)


def _outproj_kernel(oa_ref, ob_ref, x_ref, gt_ref, sc_ref, sh_ref, ga_ref, gb_ref, gn_ref,
                    w_ref, wr_ref, x1_ref, h2_ref, lg_ref):
    def wide_norm(o_ref_, g_ref_):
        o = o_ref_[0].astype(F32)
        ms = jnp.mean(o * o, axis=-1, keepdims=True)
        return (o * lax.rsqrt(ms + EPS) * g_ref_[...]).astype(BF16)

    na = wide_norm(oa_ref, ga_ref)
    nb = wide_norm(ob_ref, gb_ref)
    mix = (jnp.dot(na, w_ref[0:WIDTH_A, :], preferred_element_type=F32)
           + jnp.dot(nb, w_ref[WIDTH_A:WIDTH_A + WIDTH_B, :], preferred_element_type=F32))
    x1 = x_ref[0] + gt_ref[0] * mix
    x1_ref[0] = x1
    ms = jnp.mean(x1 * x1, axis=-1, keepdims=True)
    h2 = (x1 * lax.rsqrt(ms + EPS) * gn_ref[...]) * (1.0 + sc_ref[0]) + sh_ref[0]
    h2b = h2.astype(BF16)
    lg_ref[0] = jnp.dot(h2b, wr_ref[...], preferred_element_type=F32)
    tm, dd = h2.shape
    half = dd // 2
    rounded = h2b.astype(F32)
    lo = lax.bitcast_convert_type(rounded[:, :half], jnp.uint32)
    hi = lax.bitcast_convert_type(rounded[:, half:], jnp.uint32)
    packed = (hi & jnp.uint32(0xFFFF0000)) | (lo >> 16)
    for s in range(half // LANES):
        h2_ref[pl.ds(s, tm, stride=half // LANES), :] = packed[:, s * LANES:(s + 1) * LANES]


def _outproj(oa, ob, x, mod3, boff, g_out_a, g_out_b, g_norm_ffn, w_out_bf, w_router_pad):
    b, t, d = x.shape
    tm = 512
    nr = w_router_pad.shape[1]
    slab = d // 2 // LANES
    row = lambda k: (lambda bi, ti: ((bi + boff) * 6 + k, 0, 0))
    return pl.pallas_call(
        _outproj_kernel,
        grid=(b, t // tm),
        in_specs=[
            pl.BlockSpec((1, tm, WIDTH_A), lambda bi, ti: (bi, ti, 0)),
            pl.BlockSpec((1, tm, WIDTH_B), lambda bi, ti: (bi, ti, 0)),
            pl.BlockSpec((1, tm, d), lambda bi, ti: (bi, ti, 0)),
            pl.BlockSpec((1, 1, d), row(2)),
            pl.BlockSpec((1, 1, d), row(4)),
            pl.BlockSpec((1, 1, d), row(3)),
            pl.BlockSpec((1, WIDTH_A), lambda bi, ti: (0, 0)),
            pl.BlockSpec((1, WIDTH_B), lambda bi, ti: (0, 0)),
            pl.BlockSpec((1, d), lambda bi, ti: (0, 0)),
            pl.BlockSpec((WIDTH_A + WIDTH_B, d), lambda bi, ti: (0, 0)),
            pl.BlockSpec((d, nr), lambda bi, ti: (0, 0)),
        ],
        out_specs=[
            pl.BlockSpec((1, tm, d), lambda bi, ti: (bi, ti, 0)),
            pl.BlockSpec((tm * slab, LANES), lambda bi, ti: (bi * (t // tm) + ti, 0)),
            pl.BlockSpec((1, tm, nr), lambda bi, ti: (bi, ti, 0)),
        ],
        out_shape=[jax.ShapeDtypeStruct((b, t, d), F32),
                   jax.ShapeDtypeStruct((b * t * slab, LANES), jnp.uint32),
                   jax.ShapeDtypeStruct((b, t, nr), F32)],
        compiler_params=_cparams(("arbitrary", "arbitrary")),
        name="outproj",
    )(oa, ob, x, mod3, mod3, mod3, g_out_a, g_out_b, g_norm_ffn, w_out_bf, w_router_pad)


FFN_OUT_CHUNK = 512


def _ffn_kernel(idx_ref, idx_next_ref, h_ref, wg_ref, wu_ref, wd_ref, g_ref, o_ref,
                xslab, xb, hm, sems, *, tm, slab):
    f = pl.program_id(2)
    k = pl.program_id(0) * pl.num_programs(1) + pl.program_id(1)
    nk = pl.num_programs(0) * pl.num_programs(1)
    slot = lax.rem(k, 2)

    def issue_gather(ids_ref, sl):
        def body(r, carry):
            src0 = pl.multiple_of(ids_ref[0, 0, r] * slab, slab)
            dst0 = pl.multiple_of(r * slab, slab)
            pltpu.make_async_copy(h_ref.at[pl.ds(src0, slab), :],
                                  xslab.at[sl, pl.ds(dst0, slab), :], sems.at[sl]).start()
            return carry
        lax.fori_loop(0, tm, body, 0, unroll=8)

    @pl.when(f == 0)
    def _():
        @pl.when(k == 0)
        def _():
            issue_gather(idx_ref, slot)

        @pl.when(k + 1 < nk)
        def _():
            issue_gather(idx_next_ref, 1 - slot)

        pltpu.make_async_copy(h_ref.at[pl.ds(0, tm * slab), :], xslab.at[slot],
                              sems.at[slot]).wait()
        half = slab * LANES
        for s in range(slab):
            w = xslab[slot, pl.ds(s, tm, stride=slab), :]
            xb[:, s * LANES:(s + 1) * LANES] = lax.bitcast_convert_type(w << 16, F32).astype(BF16)
            xb[:, half + s * LANES:half + (s + 1) * LANES] = lax.bitcast_convert_type(
                w & jnp.uint32(0xFFFF0000), F32).astype(BF16)

    x = xb[...]
    a = jnp.dot(x, wg_ref[0], preferred_element_type=F32)
    u = jnp.dot(x, wu_ref[0], preferred_element_type=F32)
    hm[f] = (a * (1.0 / (1.0 + jnp.exp(-a))) * u).astype(BF16)

    @pl.when(f == pl.num_programs(2) - 1)
    def _():
        nf, _, tf = hm.shape
        for c in range(o_ref.shape[2] // FFN_OUT_CHUNK):
            cols = slice(c * FFN_OUT_CHUNK, (c + 1) * FFN_OUT_CHUNK)
            y = jnp.dot(hm[0], wd_ref[0, 0:tf, cols], preferred_element_type=F32)
            for j in range(1, nf):
                y += jnp.dot(hm[j], wd_ref[0, j * tf:(j + 1) * tf, cols],
                             preferred_element_type=F32)
            o_ref[0, :, cols] = (y * g_ref[0]).astype(o_ref.dtype)


def _ffn(idx, h2slab, wg, wu, wd, g):
    e, cap = idx.shape
    d = wg.shape[1]
    slab = d // 2 // LANES
    fdim = wg.shape[2]
    tm = min(cap, 1024)
    tf = 512
    mt = cap // tm
    nk = e * mt
    idx3 = idx.reshape(nk, 1, tm)
    return pl.pallas_call(
        functools.partial(_ffn_kernel, tm=tm, slab=slab),
        grid=(e, mt, fdim // tf),
        in_specs=[
            pl.BlockSpec((1, 1, tm), lambda ei, mi, fi: (ei * mt + mi, 0, 0),
                         memory_space=pltpu.SMEM),
            pl.BlockSpec((1, 1, tm), lambda ei, mi, fi: (jnp.minimum(ei * mt + mi + 1, nk - 1), 0, 0),
                         memory_space=pltpu.SMEM),
            pl.BlockSpec(memory_space=pl.ANY),
            pl.BlockSpec((1, d, tf), lambda ei, mi, fi: (ei, 0, fi)),
            pl.BlockSpec((1, d, tf), lambda ei, mi, fi: (ei, 0, fi)),
            pl.BlockSpec((1, fdim, d), lambda ei, mi, fi: (ei, 0, 0)),
            pl.BlockSpec((1, tm, 1), lambda ei, mi, fi: (ei, mi, 0)),
        ],
        out_specs=pl.BlockSpec((1, tm, d), lambda ei, mi, fi: (ei, mi, 0)),
        out_shape=jax.ShapeDtypeStruct((e, cap, d), BF16),
        scratch_shapes=[pltpu.VMEM((2, tm * slab, LANES), jnp.uint32),
                        pltpu.VMEM((tm, d), BF16),
                        pltpu.VMEM((fdim // tf, tm, tf), BF16),
                        pltpu.SemaphoreType.DMA((2,))],
        compiler_params=_cparams(("arbitrary", "arbitrary", "arbitrary")),
        name="expert_ffn",
    )(idx3, idx3, h2slab, wg, wu, wd, g.reshape(e, cap, 1))


def _prefix_counts(mask_f32, upper, lower):
    within = jnp.dot(mask_f32.astype(BF16), upper, preferred_element_type=F32)
    tot = jnp.broadcast_to(within[:, LANES - 1:LANES], within.shape)
    offs = jnp.dot(lower, tot.astype(BF16), preferred_element_type=F32)
    return within, offs


def _route_kernel(lt_ref, idx_ref, g_ref, aff_scr, *, cap):
    e = pl.program_id(0)
    nchunk = lt_ref.shape[1]

    @pl.when(e == 0)
    def _():
        l = lt_ref[...]
        ex = jnp.exp(l - jnp.max(l, axis=0, keepdims=True))
        aff_scr[...] = ex / jnp.sum(ex, axis=0, keepdims=True)

    a = aff_scr[e]
    bits = lax.bitcast_convert_type(a, jnp.int32)
    capf = jnp.float32(cap)

    def count(m):
        return jnp.sum(m.astype(F32), axis=(0, 1), keepdims=True)

    def bit_step(i, thr):
        cand = thr | lax.shift_left(jnp.int32(1), 30 - i)
        return jnp.where(count(bits >= cand) >= capf, cand, thr)

    thr = lax.fori_loop(0, 31, bit_step, jnp.zeros((1, 1), jnp.int32))

    ri = lax.broadcasted_iota(jnp.int32, (LANES, LANES), 0)
    ci = lax.broadcasted_iota(jnp.int32, (LANES, LANES), 1)
    upper = (ri <= ci).astype(BF16)
    rc = lax.broadcasted_iota(jnp.int32, (nchunk, nchunk), 0)
    cc = lax.broadcasted_iota(jnp.int32, (nchunk, nchunk), 1)
    lower = (cc < rc).astype(BF16)

    gt = bits > thr
    eq = bits == thr
    need = capf - count(gt)
    eq_within, eq_offs = _prefix_counts(eq.astype(F32), upper, lower)
    sel = gt | (eq & ((eq_within + eq_offs) <= need))
    within, offs = _prefix_counts(sel.astype(F32), upper, lower)

    offs_col = offs[:, 0:1]
    ends_col = offs_col + within[:, LANES - 1:LANES]
    slot = lax.broadcasted_iota(jnp.int32, (1, cap), 1).astype(F32)
    cstar = jnp.sum((ends_col <= slot).astype(F32), axis=0, keepdims=True)
    chunk_oh = lax.broadcasted_iota(jnp.int32, (nchunk, cap), 0).astype(F32) == cstar
    rank = slot - jnp.sum(jnp.where(chunk_oh, offs_col, 0.0), axis=0, keepdims=True)
    oh = chunk_oh.astype(BF16)
    tdot = functools.partial(lax.dot_general, dimension_numbers=(((0,), (0,)), ((), ())),
                             preferred_element_type=F32)
    wsel = tdot(within.astype(BF16), oh)
    lstar = jnp.sum((wsel <= rank).astype(F32), axis=0, keepdims=True)
    idx_ref[0] = (cstar * LANES + lstar).astype(jnp.int32)

    a1 = a.astype(BF16)
    r1 = a - a1.astype(F32)
    a2 = r1.astype(BF16)
    a3 = (r1 - a2.astype(F32)).astype(BF16)
    asel = (tdot(a1, oh) + tdot(a2, oh)) + tdot(a3, oh)
    lane_oh = lax.broadcasted_iota(jnp.int32, (LANES, cap), 0).astype(F32) == lstar
    g_ref[0] = jnp.sum(jnp.where(lane_oh, asel, 0.0), axis=0, keepdims=True)


def _route(lt, cap):
    e, nchunk, _ = lt.shape
    return pl.pallas_call(
        functools.partial(_route_kernel, cap=cap),
        grid=(e,),
        in_specs=[pl.BlockSpec((e, nchunk, LANES), lambda ei: (0, 0, 0))],
        out_specs=[pl.BlockSpec((1, 1, cap), lambda ei: (ei, 0, 0)),
                   pl.BlockSpec((1, 1, cap), lambda ei: (ei, 0, 0))],
        out_shape=[jax.ShapeDtypeStruct((e, 1, cap), jnp.int32),
                   jax.ShapeDtypeStruct((e, 1, cap), F32)],
        scratch_shapes=[pltpu.VMEM((e, nchunk, LANES), F32)],
        compiler_params=_cparams(("arbitrary",)),
        name="route",
    )(lt)


SCATTER_ROWS = 512
SCATTER_BUFS = 3


def _scatter_kernel(idx_ref, idx_next_ref, ye_ref, idxv_ref, gate_ref, x1_ref, out_ref,
                    obuf, gsem, ssem, *, tiles_per_expert, tok_shift, boff):
    del x1_ref
    k = pl.program_id(0)
    nk = pl.num_programs(0)
    first = lax.rem(k, tiles_per_expert) == 0
    prev_first = lax.rem(k - 1, tiles_per_expert) == 0
    next_first = lax.rem(k + 1, tiles_per_expert) == 0
    slot = lax.rem(k, SCATTER_BUFS)

    def gather_copy(ref_idx, i, j, sl):
        return pltpu.make_async_copy(out_ref.at[pl.ds(ref_idx[0, 0, i * SUBLANES + j], 1), :],
                                     obuf.at[sl, i, pl.ds(j, 1), :], gsem.at[sl])

    def scatter_copy(ref_idx, i, j, sl):
        return pltpu.make_async_copy(obuf.at[sl, i, pl.ds(j, 1), :],
                                     out_ref.at[pl.ds(ref_idx[0, 0, i * SUBLANES + j], 1), :],
                                     ssem.at[sl])

    def wait_all(sem, sl):
        pltpu.make_async_copy(obuf.at[sl], obuf.at[sl], sem.at[sl]).wait()

    def wait_scatter(step):
        wait_all(ssem, lax.rem(step, SCATTER_BUFS))

    @pl.when(first & (k >= 1))
    def _():
        wait_scatter(k - 1)

    @pl.when((k >= 2) & (first | jnp.logical_not(prev_first)))
    def _():
        wait_scatter(k - 2)

    def issue_gather(step_idx_ref, sl):
        def body(i, carry):
            for j in range(SUBLANES):
                gather_copy(step_idx_ref, i, j, sl).start()
            return carry
        lax.fori_loop(0, SCATTER_ROWS // SUBLANES, body, 0, unroll=2)

    @pl.when(first)
    def _():
        issue_gather(idx_ref, slot)

    @pl.when((k + 1 < nk) & jnp.logical_not(next_first))
    def _():
        issue_gather(idx_next_ref, lax.rem(k + 1, SCATTER_BUFS))

    wait_all(gsem, slot)

    bid = lax.shift_right_logical(idxv_ref[0], tok_shift) + boff
    oh = (bid == lax.broadcasted_iota(jnp.int32, (1, gate_ref.shape[0]), 1)).astype(BF16)
    gt = gate_ref[...]
    g1 = gt.astype(BF16)
    r1 = gt - g1.astype(F32)
    g2 = r1.astype(BF16)
    g3 = (r1 - g2.astype(F32)).astype(BF16)
    dot = functools.partial(jnp.dot, preferred_element_type=F32)
    gate_rows = (dot(oh, g1) + dot(oh, g2)) + dot(oh, g3)
    upd = obuf[slot].reshape(ye_ref.shape) + gate_rows * ye_ref[...].astype(F32)
    obuf[slot] = upd.reshape(obuf.shape[1:])

    def issue_scatter(i, carry):
        for j in range(SUBLANES):
            scatter_copy(idx_ref, i, j, slot).start()
        return carry

    lax.fori_loop(0, SCATTER_ROWS // SUBLANES, issue_scatter, 0, unroll=2)

    @pl.when(k == nk - 1)
    def _():
        wait_scatter(k)
        wait_scatter(k - 1)


def _scatter_add(idx_flat, ye, gate_rows, x1, *, cap, tok_shift, boff):
    n_rows, d = ye.shape
    steps = n_rows // SCATTER_ROWS
    tiles_per_expert = cap // SCATTER_ROWS
    assert tiles_per_expert >= 2
    kern = functools.partial(_scatter_kernel, tiles_per_expert=tiles_per_expert,
                             tok_shift=tok_shift, boff=boff)
    idx3 = idx_flat.reshape(steps, 1, SCATTER_ROWS)
    return pl.pallas_call(
        kern,
        grid=(steps,),
        in_specs=[
            pl.BlockSpec((1, 1, SCATTER_ROWS), lambda k: (k, 0, 0), memory_space=pltpu.SMEM),
            pl.BlockSpec((1, 1, SCATTER_ROWS), lambda k: (jnp.minimum(k + 1, steps - 1), 0, 0),
                         memory_space=pltpu.SMEM),
            pl.BlockSpec((SCATTER_ROWS, d), lambda k: (k, 0)),
            pl.BlockSpec((1, SCATTER_ROWS, 1), lambda k: (k, 0, 0)),
            pl.BlockSpec(gate_rows.shape, lambda k: (0, 0)),
            pl.BlockSpec(memory_space=pl.ANY),
        ],
        out_specs=pl.BlockSpec(memory_space=pl.ANY),
        out_shape=jax.ShapeDtypeStruct(x1.shape, x1.dtype),
        scratch_shapes=[pltpu.VMEM((SCATTER_BUFS, SCATTER_ROWS // SUBLANES, SUBLANES, d), F32),
                        pltpu.SemaphoreType.DMA((SCATTER_BUFS,)),
                        pltpu.SemaphoreType.DMA((SCATTER_BUFS,))],
        input_output_aliases={5: 0},
        compiler_params=_cparams(("arbitrary",)),
        name="scatter_add",
    )(idx3, idx3, ye, idx_flat.reshape(steps, SCATTER_ROWS, 1), gate_rows, x1)


def _rope_tables(t):
    half = HEAD_DIM // 2
    quarter = half // 2
    freqs = ROPE_THETA ** (-(jnp.arange(quarter, dtype=F32) / quarter))
    rows = t // GRID_W
    row_ids = jnp.repeat(jnp.arange(rows, dtype=jnp.int32), GRID_W).astype(F32)
    col_ids = jnp.tile(jnp.arange(GRID_W, dtype=jnp.int32), rows).astype(F32)
    ang_r = row_ids[:, None] * freqs[None, :]
    ang_c = col_ids[:, None] * freqs[None, :]
    z = jnp.zeros_like(ang_r)
    cos_t = jnp.concatenate([jnp.cos(ang_r)] * 2 + [jnp.cos(ang_c)] * 2, axis=1)
    sa_t = jnp.concatenate([-jnp.sin(ang_r), z, -jnp.sin(ang_c), z], axis=1)
    sb_t = jnp.concatenate([z, jnp.sin(ang_r), z, jnp.sin(ang_c)], axis=1)
    return cos_t, sa_t, sb_t


def kernel(x_prompt, x_sample, c_prompt, c_sample, rel_bias_table, w_ada, b_ada, g_norm_mix, g_norm_ffn, w_in, g_q_a, g_k_a, g_q_b, g_k_b, g_out_a, g_out_b, w_out, w_router, w_gate, w_up, w_down):
    d = D_MODEL
    nbp, nbs = c_prompt.shape[0], c_sample.shape[0]
    c_all = jnp.concatenate([c_prompt, c_sample, jnp.zeros((16 - nbp - nbs, d), F32)], axis=0)
    mod = _ada(c_all, w_ada[0], b_ada[0])
    mod3 = mod.reshape(16 * 6, 1, d)
    gt2_all = mod[:, 5 * d:6 * d]

    scale = HEAD_DIM ** -0.5
    gains = jnp.stack([g_q_a[0] * scale, g_k_a[0], g_q_b[0] * scale, g_k_b[0]], axis=0)
    w_in_bf = w_in[0].astype(BF16)
    w_out_bf = w_out[0].astype(BF16)
    w_router_pad = jnp.pad(w_router[0], ((0, 0), (0, HEAD_DIM - N_EXPERTS))).astype(BF16)
    wg, wu, wd = w_gate[0].astype(BF16), w_up[0].astype(BF16), w_down[0].astype(BF16)
    bias_tiles = _bias_tiles(rel_bias_table)
    gn1 = g_norm_mix[0].reshape(1, d)
    gn2 = g_norm_ffn[0].reshape(1, d)
    ga = g_out_a[0].reshape(1, WIDTH_A)
    gb = g_out_b[0].reshape(1, WIDTH_B)

    def run(x, boff):
        b, t, _ = x.shape
        cos_t, sa_t, sb_t = _rope_tables(t)
        proj = _inproj(x, mod3, boff, gn1, w_in_bf, gains, cos_t, sa_t, sb_t)
        oa = _attn_a(proj, bias_tiles)
        ob = _attn_b(proj)
        x1, h2, logits = _outproj(oa, ob, x, mod3, boff, ga, gb, gn2, w_out_bf, w_router_pad)
        n = b * t
        cap = EC_CAPACITY_FACTOR * n // N_EXPERTS
        lt = logits.reshape(n, -1)[:, :N_EXPERTS].T.reshape(N_EXPERTS, n // LANES, LANES)
        idx, g = _route(lt, cap)
        idx_flat = idx.reshape(N_EXPERTS * cap)
        ye = _ffn(idx.reshape(N_EXPERTS, cap), h2, wg, wu, wd, g.reshape(N_EXPERTS, cap))
        out = _scatter_add(idx_flat, ye.reshape(N_EXPERTS * cap, d), gt2_all, x1.reshape(n, d),
                           cap=cap, tok_shift=int(math.log2(t)), boff=boff)
        return out.reshape(b, t, d)

    return (run(x_prompt, 0), run(x_sample, nbp))
```

```python
import functools
import math

import jax
import jax.numpy as jnp
from jax import lax
from jax.experimental import pallas as pl
from jax.experimental.pallas import tpu as pltpu

F32 = jnp.float32
BF16 = jnp.bfloat16

D_MODEL = 2048
HEAD_DIM = 128
N_HEADS_A = 8
N_HEADS_B = 8
N_KV_B = 2
GQA_GROUP = N_HEADS_B // N_KV_B
WIDTH_A = N_HEADS_A * HEAD_DIM
WIDTH_B = N_HEADS_B * HEAD_DIM
KV_WIDTH_B = N_KV_B * HEAD_DIM
IN_COLS = 3 * WIDTH_A + WIDTH_B + 2 * KV_WIDTH_B
DIL_CONFIGS = ((128, 1), (512, 4), (2048, 16))
NUM_BUCKETS = 32
MAX_DISTANCE = 1024
GRID_W = 64
ROPE_THETA = 10000.0
N_EXPERTS = 16
EC_CAPACITY_FACTOR = 2
D_EXPERT = 2048
EPS = 1e-6
NEG_INF = -1e30

VMEM_LIMIT_V7X = 56 * 1024 * 1024
LANES = 128
SUBLANES = 8

COL_QA, COL_KA, COL_VA = 0, 8, 16
COL_QB, COL_KB, COL_VB = 24, 32, 34

IN_TN = 512
IN_CHAINS = 4
OUT_CHAINS = 2
A_QB = 128
A_KB = 256
A_RADIUS = 64
A_PAD = A_RADIUS * 16
A_UNROLL = 8
B_SUB = 128


def _cparams(sem):
    return pltpu.CompilerParams(dimension_semantics=sem, vmem_limit_bytes=VMEM_LIMIT_V7X)


def _ada_kernel(c_ref, w_ref, b_ref, o_ref):
    c = c_ref[...]
    s = c * (1.0 / (1.0 + jnp.exp(-c)))
    o_ref[...] = jnp.dot(s.astype(BF16), w_ref[...].astype(BF16),
                         preferred_element_type=F32) + b_ref[...]


def _ada(c_all, w_ada, b_ada):
    rows, d = c_all.shape
    n = w_ada.shape[1]
    tn = 1024
    return pl.pallas_call(
        _ada_kernel,
        grid=(n // tn,),
        in_specs=[pl.BlockSpec((rows, d), lambda j: (0, 0)),
                  pl.BlockSpec((d, tn), lambda j: (0, j)),
                  pl.BlockSpec((1, tn), lambda j: (0, j))],
        out_specs=pl.BlockSpec((rows, tn), lambda j: (0, j)),
        out_shape=jax.ShapeDtypeStruct((rows, n), F32),
        compiler_params=_cparams(("arbitrary",)),
        name="ada_mod",
    )(c_all, w_ada, b_ada.reshape(1, n))


def _head_norm(a, g):
    ms = jnp.mean(a * a, axis=-1, keepdims=True)
    return a * lax.rsqrt(ms + EPS) * g


def _inproj_kernel(x_ref, sc_ref, sh_ref, gn_ref, w_ref, gains_ref, cos_ref, sa_ref, sb_ref,
                   o_ref, h_scr):
    j = pl.program_id(2)

    @pl.when(j == 0)
    def _():
        x = x_ref[0]
        ms = jnp.mean(x * x, axis=-1, keepdims=True)
        y = x * lax.rsqrt(ms + EPS) * gn_ref[...]
        h_scr[...] = (y * (1.0 + sc_ref[0]) + sh_ref[0]).astype(BF16)

    def rope(a, rows):
        return (a * cos_ref[rows, :] + pltpu.roll(a, 96, 1) * sa_ref[rows, :]
                + pltpu.roll(a, 32, 1) * sb_ref[rows, :])

    def plain(a, rows):
        return a

    def norm(g_row):
        return lambda a, rows: _head_norm(a, gains_ref[g_row:g_row + 1, :])

    def norm_rope(g_row):
        return lambda a, rows: rope(_head_norm(a, gains_ref[g_row:g_row + 1, :]), rows)

    def tile(head_fns):
        sub = h_scr.shape[0] // IN_CHAINS
        for c in range(IN_CHAINS):
            rows = slice(c * sub, (c + 1) * sub)
            acc = jnp.dot(h_scr[rows, :], w_ref[...], preferred_element_type=F32)
            for hh, fn in enumerate(head_fns):
                sl = slice(hh * HEAD_DIM, (hh + 1) * HEAD_DIM)
                o_ref[0, rows, sl] = fn(acc[:, sl], rows).astype(BF16)

    @pl.when(j < 2)
    def _():
        tile([norm(0)] * 4)

    @pl.when((j >= 2) & (j < 4))
    def _():
        tile([norm(1)] * 4)

    @pl.when((j >= 4) & (j < 6))
    def _():
        tile([plain] * 4)

    @pl.when((j >= 6) & (j < 8))
    def _():
        tile([norm_rope(2)] * 4)

    @pl.when(j == 8)
    def _():
        tile([norm_rope(3)] * 2 + [plain] * 2)


def _inproj(x, mod3, boff, g_norm, w_in_bf, gains, cos_t, sa_t, sb_t):
    b, t, d = x.shape
    tm = 1024
    nj = IN_COLS // IN_TN
    return pl.pallas_call(
        _inproj_kernel,
        grid=(b, t // tm, nj),
        in_specs=[
            pl.BlockSpec((1, tm, d), lambda bi, ti, j: (bi, ti, 0)),
            pl.BlockSpec((1, 1, d), lambda bi, ti, j: ((bi + boff) * 6 + 1, 0, 0)),
            pl.BlockSpec((1, 1, d), lambda bi, ti, j: ((bi + boff) * 6 + 0, 0, 0)),
            pl.BlockSpec((1, d), lambda bi, ti, j: (0, 0)),
            pl.BlockSpec((d, IN_TN), lambda bi, ti, j: (0, j)),
            pl.BlockSpec((4, HEAD_DIM), lambda bi, ti, j: (0, 0)),
            pl.BlockSpec((tm, HEAD_DIM), lambda bi, ti, j: (ti, 0)),
            pl.BlockSpec((tm, HEAD_DIM), lambda bi, ti, j: (ti, 0)),
            pl.BlockSpec((tm, HEAD_DIM), lambda bi, ti, j: (ti, 0)),
        ],
        out_specs=pl.BlockSpec((1, tm, IN_TN), lambda bi, ti, j: (bi, ti, j)),
        out_shape=jax.ShapeDtypeStruct((b, t, IN_COLS), BF16),
        scratch_shapes=[pltpu.VMEM((tm, d), BF16)],
        compiler_params=_cparams(("arbitrary", "arbitrary", "arbitrary")),
        name="inproj",
    )(x, mod3, mod3, g_norm, w_in_bf, gains, cos_t, sa_t, sb_t)


def _attn_a_kernel(q_ref, k_ref, v_ref, bias_ref, o_ref, qf, kf, vf, acc, mm, ll, *, t):
    zpad = jnp.zeros((A_PAD, HEAD_DIM), F32)
    kf[0:A_PAD, :] = zpad
    vf[0:A_PAD, :] = zpad
    kf[A_PAD + t:A_PAD + t + A_PAD, :] = zpad
    vf[A_PAD + t:A_PAD + t + A_PAD, :] = zpad
    kf[A_PAD:A_PAD + t, :] = k_ref[0].astype(F32)
    vf[A_PAD:A_PAD + t, :] = v_ref[0].astype(F32)
    qf[...] = q_ref[0].astype(F32)
    ones = jnp.ones((A_KB, HEAD_DIM), BF16)

    for bi, (_, dil) in enumerate(DIL_CONFIGS):
        sub_len = t // dil
        nmb = sub_len // A_QB
        shift = int(math.log2(nmb))

        def rows(start, size, dil=dil):
            return pl.ds(start, size) if dil == 1 else pl.ds(start, size, stride=dil)

        def body(idx, carry, bi=bi, dil=dil, sub_len=sub_len, nmb=nmb, shift=shift, rows=rows):
            rho = lax.shift_right_logical(idx, shift)
            mb = lax.bitwise_and(idx, nmb - 1)
            qstart = rho + mb * (A_QB * dil)
            kstart = A_PAD + qstart - A_RADIUS * dil
            q = qf[rows(qstart, A_QB), :].astype(BF16)
            k = kf[rows(kstart, A_KB), :].astype(BF16)
            v = vf[rows(kstart, A_KB), :].astype(BF16)
            s = lax.dot_general(q, k, (((1,), (1,)), ((), ())), preferred_element_type=F32)
            s = s + bias_ref[bi, 0]
            kidx = mb * A_QB - A_RADIUS + lax.broadcasted_iota(jnp.int32, (1, A_KB), 1)
            s = jnp.where((kidx >= 0) & (kidx < sub_len), s, NEG_INF)
            mblk = jnp.max(s, axis=-1, keepdims=True)
            v1 = jnp.concatenate([v, ones], axis=1)
            p = jnp.exp(s - mblk).astype(BF16)
            pv = jnp.dot(p, v1, preferred_element_type=F32)
            acc[bi, rows(qstart, A_QB), :] = pv[:, :HEAD_DIM]
            ll[bi, rows(qstart, A_QB), :] = pv[:, HEAD_DIM:]
            mm[bi, rows(qstart, A_QB), :] = jnp.broadcast_to(mblk, (A_QB, HEAD_DIM))
            return carry

        lax.fori_loop(0, dil * nmb, body, 0, unroll=A_UNROLL)

    m0, m1, m2 = mm[0], mm[1], mm[2]
    mtop = jnp.maximum(m0, jnp.maximum(m1, m2))
    w0, w1, w2 = jnp.exp(m0 - mtop), jnp.exp(m1 - mtop), jnp.exp(m2 - mtop)
    num = w0 * acc[0] + w1 * acc[1] + w2 * acc[2]
    den = w0 * ll[0] + w1 * ll[1] + w2 * ll[2]
    o_ref[0] = (num / den).astype(BF16)


def _attn_a(proj, bias_tiles):
    b, t, _ = proj.shape
    kern = functools.partial(_attn_a_kernel, t=t)
    return pl.pallas_call(
        kern,
        grid=(b, N_HEADS_A),
        in_specs=[
            pl.BlockSpec((1, t, HEAD_DIM), lambda bi, h: (bi, 0, COL_QA + h)),
            pl.BlockSpec((1, t, HEAD_DIM), lambda bi, h: (bi, 0, COL_KA + h)),
            pl.BlockSpec((1, t, HEAD_DIM), lambda bi, h: (bi, 0, COL_VA + h)),
            pl.BlockSpec((3, 1, A_QB, A_KB), lambda bi, h: (0, h, 0, 0)),
        ],
        out_specs=pl.BlockSpec((1, t, HEAD_DIM), lambda bi, h: (bi, 0, h)),
        out_shape=jax.ShapeDtypeStruct((b, t, WIDTH_A), BF16),
        scratch_shapes=[
            pltpu.VMEM((t, HEAD_DIM), F32),
            pltpu.VMEM((t + 2 * A_PAD, HEAD_DIM), F32),
            pltpu.VMEM((t + 2 * A_PAD, HEAD_DIM), F32),
            pltpu.VMEM((len(DIL_CONFIGS), t, HEAD_DIM), F32),
            pltpu.VMEM((len(DIL_CONFIGS), t, HEAD_DIM), F32),
            pltpu.VMEM((len(DIL_CONFIGS), t, HEAD_DIM), F32),
        ],
        compiler_params=_cparams(("arbitrary", "arbitrary")),
        name="attn_dilated",
    )(proj, proj, proj, bias_tiles)


def _t5_bucket(rel):
    nb = NUM_BUCKETS // 2
    max_exact = nb // 2
    sign_off = jnp.where(rel > 0, nb, 0)
    n = jnp.abs(rel)
    nf = jnp.maximum(n, 1).astype(F32)
    large = max_exact + (jnp.log(nf / max_exact) / math.log(MAX_DISTANCE / max_exact)
                         * (nb - max_exact)).astype(jnp.int32)
    large = jnp.minimum(large, nb - 1)
    return sign_off + jnp.where(n < max_exact, n, large)


def _bias_tiles(rel_bias_table):
    qi = jnp.arange(A_QB, dtype=jnp.int32)[:, None]
    kj = jnp.arange(A_KB, dtype=jnp.int32)[None, :]
    rel = kj - A_RADIUS - qi
    tiles = []
    for _, dil in DIL_CONFIGS:
        onehot = (_t5_bucket(rel * dil)[..., None]
                  == jnp.arange(NUM_BUCKETS, dtype=jnp.int32)).astype(F32)
        bias = jnp.einsum('qkn,nh->hqk', onehot, rel_bias_table.astype(F32),
                          precision=lax.Precision.HIGHEST)
        tiles.append(jnp.where((jnp.abs(rel) <= A_RADIUS)[None], bias, NEG_INF))
    return jnp.stack(tiles, axis=0)


def _attn_b_kernel(q_ref, k_ref, v_ref, o_ref, v1_scr, *, tq):
    @pl.when(pl.program_id(2) == 0)
    def _():
        v1_scr[:, 0:HEAD_DIM] = v_ref[0]
        v1_scr[:, HEAD_DIM:2 * HEAD_DIM] = jnp.ones(v_ref.shape[1:], BF16)

    for c in range(tq // B_SUB):
        q = q_ref[0, c * B_SUB:(c + 1) * B_SUB, :]
        qs = jnp.concatenate([q[:, i * HEAD_DIM:(i + 1) * HEAD_DIM] for i in range(GQA_GROUP)],
                             axis=0)
        s = lax.dot_general(qs, k_ref[0], (((1,), (1,)), ((), ())), preferred_element_type=F32)
        m = jnp.max(s, axis=-1, keepdims=True)
        p = jnp.exp(s - m).astype(BF16)
        pv = jnp.dot(p, v1_scr[...], preferred_element_type=F32)
        o = pv[:, :HEAD_DIM] / pv[:, HEAD_DIM:]
        o_ref[0, c * B_SUB:(c + 1) * B_SUB, :] = jnp.concatenate(
            [o[i * B_SUB:(i + 1) * B_SUB] for i in range(GQA_GROUP)], axis=1).astype(BF16)


def _attn_b(proj):
    b, t, _ = proj.shape
    tq = 4 * B_SUB
    gw = GQA_GROUP * HEAD_DIM
    return pl.pallas_call(
        functools.partial(_attn_b_kernel, tq=tq),
        grid=(b, N_KV_B, t // tq),
        in_specs=[
            pl.BlockSpec((1, tq, gw), lambda bi, g, qi: (bi, qi, COL_QB // GQA_GROUP + g)),
            pl.BlockSpec((1, t, HEAD_DIM), lambda bi, g, qi: (bi, 0, COL_KB + g)),
            pl.BlockSpec((1, t, HEAD_DIM), lambda bi, g, qi: (bi, 0, COL_VB + g)),
        ],
        out_specs=pl.BlockSpec((1, tq, gw), lambda bi, g, qi: (bi, qi, g)),
        out_shape=jax.ShapeDtypeStruct((b, t, WIDTH_B), BF16),
        scratch_shapes=[pltpu.VMEM((t, 2 * HEAD_DIM), BF16)],
        compiler_params=_cparams(("arbitrary", "arbitrary", "arbitrary")),
        name="attn_gqa",
    )(proj, proj, proj)


def _outproj_kernel(oa_ref, ob_ref, x_ref, gt_ref, sc_ref, sh_ref, ga_ref, gb_ref, gn_ref,
                    w_ref, wr_ref, x1_ref, h2_ref, lg_ref):
    def wide_norm(o, g_ref_):
        o = o.astype(F32)
        ms = jnp.mean(o * o, axis=-1, keepdims=True)
        return (o * lax.rsqrt(ms + EPS) * g_ref_[...]).astype(BF16)

    tm, dd = x_ref.shape[1:]
    half = dd // 2
    slab = half // LANES
    sub = tm // OUT_CHAINS
    for c in range(OUT_CHAINS):
        rows = slice(c * sub, (c + 1) * sub)
        na = wide_norm(oa_ref[0, rows, :], ga_ref)
        nb = wide_norm(ob_ref[0, rows, :], gb_ref)
        mix = (jnp.dot(na, w_ref[0:WIDTH_A, :], preferred_element_type=F32)
               + jnp.dot(nb, w_ref[WIDTH_A:WIDTH_A + WIDTH_B, :], preferred_element_type=F32))
        x1 = x_ref[0, rows, :] + gt_ref[0] * mix
        x1_ref[0, rows, :] = x1
        ms = jnp.mean(x1 * x1, axis=-1, keepdims=True)
        h2 = (x1 * lax.rsqrt(ms + EPS) * gn_ref[...]) * (1.0 + sc_ref[0]) + sh_ref[0]
        h2b = h2.astype(BF16)
        lg_ref[0, rows, :] = jnp.dot(h2b, wr_ref[...], preferred_element_type=F32)
        rounded = h2b.astype(F32)
        lo = lax.bitcast_convert_type(rounded[:, :half], jnp.uint32)
        hi = lax.bitcast_convert_type(rounded[:, half:], jnp.uint32)
        packed = (hi & jnp.uint32(0xFFFF0000)) | (lo >> 16)
        for s in range(slab):
            h2_ref[pl.ds(c * sub * slab + s, sub, stride=slab), :] = packed[:, s * LANES:(s + 1) * LANES]


def _outproj(oa, ob, x, mod3, boff, g_out_a, g_out_b, g_norm_ffn, w_out_bf, w_router_pad):
    b, t, d = x.shape
    tm = 512
    nr = w_router_pad.shape[1]
    slab = d // 2 // LANES
    row = lambda k: (lambda bi, ti: ((bi + boff) * 6 + k, 0, 0))
    return pl.pallas_call(
        _outproj_kernel,
        grid=(b, t // tm),
        in_specs=[
            pl.BlockSpec((1, tm, WIDTH_A), lambda bi, ti: (bi, ti, 0)),
            pl.BlockSpec((1, tm, WIDTH_B), lambda bi, ti: (bi, ti, 0)),
            pl.BlockSpec((1, tm, d), lambda bi, ti: (bi, ti, 0)),
            pl.BlockSpec((1, 1, d), row(2)),
            pl.BlockSpec((1, 1, d), row(4)),
            pl.BlockSpec((1, 1, d), row(3)),
            pl.BlockSpec((1, WIDTH_A), lambda bi, ti: (0, 0)),
            pl.BlockSpec((1, WIDTH_B), lambda bi, ti: (0, 0)),
            pl.BlockSpec((1, d), lambda bi, ti: (0, 0)),
            pl.BlockSpec((WIDTH_A + WIDTH_B, d), lambda bi, ti: (0, 0)),
            pl.BlockSpec((d, nr), lambda bi, ti: (0, 0)),
        ],
        out_specs=[
            pl.BlockSpec((1, tm, d), lambda bi, ti: (bi, ti, 0)),
            pl.BlockSpec((tm * slab, LANES), lambda bi, ti: (bi * (t // tm) + ti, 0)),
            pl.BlockSpec((1, tm, nr), lambda bi, ti: (bi, ti, 0)),
        ],
        out_shape=[jax.ShapeDtypeStruct((b, t, d), F32),
                   jax.ShapeDtypeStruct((b * t * slab, LANES), jnp.uint32),
                   jax.ShapeDtypeStruct((b, t, nr), F32)],
        compiler_params=_cparams(("arbitrary", "arbitrary")),
        name="outproj",
    )(oa, ob, x, mod3, mod3, mod3, g_out_a, g_out_b, g_norm_ffn, w_out_bf, w_router_pad)


FFN_OUT_CHUNK = 512


def _ffn_kernel(idx_ref, idx_next_ref, h_ref, wg_ref, wu_ref, wd_ref, g_ref, o_ref,
                xslab, xb, hm, wdb, sems, *, tm, slab):
    f = pl.program_id(2)
    k = pl.program_id(0) * pl.num_programs(1) + pl.program_id(1)
    nk = pl.num_programs(0) * pl.num_programs(1)
    slot = lax.rem(k, 2)

    def issue_gather(ids_ref, sl):
        def body(r, carry):
            src0 = pl.multiple_of(ids_ref[0, 0, r] * slab, slab)
            dst0 = pl.multiple_of(r * slab, slab)
            pltpu.make_async_copy(h_ref.at[pl.ds(src0, slab), :],
                                  xslab.at[sl, pl.ds(dst0, slab), :], sems.at[sl]).start()
            return carry
        lax.fori_loop(0, tm, body, 0, unroll=8)

    @pl.when(f == 0)
    def _():
        @pl.when(k == 0)
        def _():
            issue_gather(idx_ref, slot)

        @pl.when(k + 1 < nk)
        def _():
            issue_gather(idx_next_ref, 1 - slot)

        pltpu.make_async_copy(h_ref.at[pl.ds(0, tm * slab), :], xslab.at[slot],
                              sems.at[slot]).wait()
        half = slab * LANES
        for s in range(slab):
            w = xslab[slot, pl.ds(s, tm, stride=slab), :]
            xb[:, s * LANES:(s + 1) * LANES] = lax.bitcast_convert_type(w << 16, F32).astype(BF16)
            xb[:, half + s * LANES:half + (s + 1) * LANES] = lax.bitcast_convert_type(
                w & jnp.uint32(0xFFFF0000), F32).astype(BF16)

    @pl.when(pl.program_id(1) == 0)
    def _():
        wdb[f] = wd_ref[0].astype(BF16)

    x = xb[...]
    a = jnp.dot(x, wg_ref[0].astype(BF16), preferred_element_type=F32)
    u = jnp.dot(x, wu_ref[0].astype(BF16), preferred_element_type=F32)
    hm[f] = (a * (1.0 / (1.0 + jnp.exp(-a))) * u).astype(BF16)

    @pl.when(f == pl.num_programs(2) - 1)
    def _():
        nf = hm.shape[0]
        for c in range(o_ref.shape[2] // FFN_OUT_CHUNK):
            cols = slice(c * FFN_OUT_CHUNK, (c + 1) * FFN_OUT_CHUNK)
            y = jnp.dot(hm[0], wdb[0, :, cols], preferred_element_type=F32)
            for j in range(1, nf):
                y += jnp.dot(hm[j], wdb[j, :, cols], preferred_element_type=F32)
            o_ref[0, :, cols] = (y * g_ref[0]).astype(o_ref.dtype)


def _ffn(idx, h2slab, wg, wu, wd, g):
    e, cap = idx.shape
    d = wg.shape[1]
    slab = d // 2 // LANES
    fdim = wg.shape[2]
    tm = min(cap, 1024)
    tf = 256
    nf = fdim // tf
    mt = cap // tm
    nk = e * mt
    idx3 = idx.reshape(nk, 1, tm)
    return pl.pallas_call(
        functools.partial(_ffn_kernel, tm=tm, slab=slab),
        grid=(e, mt, fdim // tf),
        in_specs=[
            pl.BlockSpec((1, 1, tm), lambda ei, mi, fi: (ei * mt + mi, 0, 0),
                         memory_space=pltpu.SMEM),
            pl.BlockSpec((1, 1, tm), lambda ei, mi, fi: (jnp.minimum(ei * mt + mi + 1, nk - 1), 0, 0),
                         memory_space=pltpu.SMEM),
            pl.BlockSpec(memory_space=pl.ANY),
            pl.BlockSpec((1, d, tf), lambda ei, mi, fi: (ei, 0, fi)),
            pl.BlockSpec((1, d, tf), lambda ei, mi, fi: (ei, 0, fi)),
            pl.BlockSpec((1, tf, d), lambda ei, mi, fi: (ei, jnp.where(mi == 0, fi, nf - 1), 0)),
            pl.BlockSpec((1, tm, 1), lambda ei, mi, fi: (ei, mi, 0)),
        ],
        out_specs=pl.BlockSpec((1, tm, d), lambda ei, mi, fi: (ei, mi, 0)),
        out_shape=jax.ShapeDtypeStruct((e, cap, d), BF16),
        scratch_shapes=[pltpu.VMEM((2, tm * slab, LANES), jnp.uint32),
                        pltpu.VMEM((tm, d), BF16),
                        pltpu.VMEM((nf, tm, tf), BF16),
                        pltpu.VMEM((nf, tf, d), BF16),
                        pltpu.SemaphoreType.DMA((2,))],
        compiler_params=_cparams(("arbitrary", "arbitrary", "arbitrary")),
        name="expert_ffn",
    )(idx3, idx3, h2slab, wg, wu, wd, g.reshape(e, cap, 1))


def _prefix_counts(mask_f32, upper, lower):
    within = jnp.dot(mask_f32.astype(BF16), upper, preferred_element_type=F32)
    tot = jnp.broadcast_to(within[:, LANES - 1:LANES], within.shape)
    offs = jnp.dot(lower, tot.astype(BF16), preferred_element_type=F32)
    return within, offs


def _route_kernel(lt_ref, idx_ref, g_ref, aff_scr, *, cap):
    e = pl.program_id(0)
    nchunk = lt_ref.shape[1]

    @pl.when(e == 0)
    def _():
        l = lt_ref[...]
        ex = jnp.exp(l - jnp.max(l, axis=0, keepdims=True))
        aff_scr[...] = ex / jnp.sum(ex, axis=0, keepdims=True)

    a = aff_scr[e]
    bits = lax.bitcast_convert_type(a, jnp.int32)
    capf = jnp.float32(cap)

    def count(m):
        return jnp.sum(m.astype(F32), axis=(0, 1), keepdims=True)

    def bit_step(i, thr):
        cand = thr | lax.shift_left(jnp.int32(1), 30 - i)
        return jnp.where(count(bits >= cand) >= capf, cand, thr)

    thr = lax.fori_loop(0, 31, bit_step, jnp.zeros((1, 1), jnp.int32))

    ri = lax.broadcasted_iota(jnp.int32, (LANES, LANES), 0)
    ci = lax.broadcasted_iota(jnp.int32, (LANES, LANES), 1)
    upper = (ri <= ci).astype(BF16)
    rc = lax.broadcasted_iota(jnp.int32, (nchunk, nchunk), 0)
    cc = lax.broadcasted_iota(jnp.int32, (nchunk, nchunk), 1)
    lower = (cc < rc).astype(BF16)

    gt = bits > thr
    eq = bits == thr
    need = capf - count(gt)
    eq_within, eq_offs = _prefix_counts(eq.astype(F32), upper, lower)
    sel = gt | (eq & ((eq_within + eq_offs) <= need))
    within, offs = _prefix_counts(sel.astype(F32), upper, lower)

    offs_col = offs[:, 0:1]
    ends_col = offs_col + within[:, LANES - 1:LANES]
    slot = lax.broadcasted_iota(jnp.int32, (1, cap), 1).astype(F32)
    cstar = jnp.sum((ends_col <= slot).astype(F32), axis=0, keepdims=True)
    chunk_oh = lax.broadcasted_iota(jnp.int32, (nchunk, cap), 0).astype(F32) == cstar
    rank = slot - jnp.sum(jnp.where(chunk_oh, offs_col, 0.0), axis=0, keepdims=True)
    oh = chunk_oh.astype(BF16)
    tdot = functools.partial(lax.dot_general, dimension_numbers=(((0,), (0,)), ((), ())),
                             preferred_element_type=F32)
    wsel = tdot(within.astype(BF16), oh)
    lstar = jnp.sum((wsel <= rank).astype(F32), axis=0, keepdims=True)
    idx_ref[0] = (cstar * LANES + lstar).astype(jnp.int32)

    a1 = a.astype(BF16)
    r1 = a - a1.astype(F32)
    a2 = r1.astype(BF16)
    a3 = (r1 - a2.astype(F32)).astype(BF16)
    asel = (tdot(a1, oh) + tdot(a2, oh)) + tdot(a3, oh)
    lane_oh = lax.broadcasted_iota(jnp.int32, (LANES, cap), 0).astype(F32) == lstar
    g_ref[0] = jnp.sum(jnp.where(lane_oh, asel, 0.0), axis=0, keepdims=True)


def _route(lt, cap):
    e, nchunk, _ = lt.shape
    return pl.pallas_call(
        functools.partial(_route_kernel, cap=cap),
        grid=(e,),
        in_specs=[pl.BlockSpec((e, nchunk, LANES), lambda ei: (0, 0, 0))],
        out_specs=[pl.BlockSpec((1, 1, cap), lambda ei: (ei, 0, 0)),
                   pl.BlockSpec((1, 1, cap), lambda ei: (ei, 0, 0))],
        out_shape=[jax.ShapeDtypeStruct((e, 1, cap), jnp.int32),
                   jax.ShapeDtypeStruct((e, 1, cap), F32)],
        scratch_shapes=[pltpu.VMEM((e, nchunk, LANES), F32)],
        compiler_params=_cparams(("arbitrary",)),
        name="route",
    )(lt)


SCATTER_ROWS = 512
SCATTER_BUFS = 3


def _scatter_kernel(idx_ref, idx_next_ref, ye_ref, idxv_ref, gate_ref, x1_ref, out_ref,
                    obuf, gsem, ssem, *, tiles_per_expert, tok_shift, boff):
    del x1_ref
    k = pl.program_id(0)
    nk = pl.num_programs(0)
    first = lax.rem(k, tiles_per_expert) == 0
    prev_first = lax.rem(k - 1, tiles_per_expert) == 0
    next_first = lax.rem(k + 1, tiles_per_expert) == 0
    slot = lax.rem(k, SCATTER_BUFS)

    def gather_copy(ref_idx, i, j, sl):
        return pltpu.make_async_copy(out_ref.at[pl.ds(ref_idx[0, 0, i * SUBLANES + j], 1), :],
                                     obuf.at[sl, i, pl.ds(j, 1), :], gsem.at[sl])

    def scatter_copy(ref_idx, i, j, sl):
        return pltpu.make_async_copy(obuf.at[sl, i, pl.ds(j, 1), :],
                                     out_ref.at[pl.ds(ref_idx[0, 0, i * SUBLANES + j], 1), :],
                                     ssem.at[sl])

    def wait_all(sem, sl):
        pltpu.make_async_copy(obuf.at[sl], obuf.at[sl], sem.at[sl]).wait()

    def wait_scatter(step):
        wait_all(ssem, lax.rem(step, SCATTER_BUFS))

    @pl.when(first & (k >= 1))
    def _():
        wait_scatter(k - 1)

    @pl.when((k >= 2) & (first | jnp.logical_not(prev_first)))
    def _():
        wait_scatter(k - 2)

    def issue_gather(step_idx_ref, sl):
        def body(i, carry):
            for j in range(SUBLANES):
                gather_copy(step_idx_ref, i, j, sl).start()
            return carry
        lax.fori_loop(0, SCATTER_ROWS // SUBLANES, body, 0, unroll=2)

    @pl.when(first)
    def _():
        issue_gather(idx_ref, slot)

    @pl.when((k + 1 < nk) & jnp.logical_not(next_first))
    def _():
        issue_gather(idx_next_ref, lax.rem(k + 1, SCATTER_BUFS))

    wait_all(gsem, slot)

    bid = lax.shift_right_logical(idxv_ref[0], tok_shift) + boff
    oh = (bid == lax.broadcasted_iota(jnp.int32, (1, gate_ref.shape[0]), 1)).astype(BF16)
    gt = gate_ref[...]
    g1 = gt.astype(BF16)
    r1 = gt - g1.astype(F32)
    g2 = r1.astype(BF16)
    g3 = (r1 - g2.astype(F32)).astype(BF16)
    dot = functools.partial(jnp.dot, preferred_element_type=F32)
    gate_rows = (dot(oh, g1) + dot(oh, g2)) + dot(oh, g3)
    upd = obuf[slot].reshape(ye_ref.shape) + gate_rows * ye_ref[...].astype(F32)
    obuf[slot] = upd.reshape(obuf.shape[1:])

    def issue_scatter(i, carry):
        for j in range(SUBLANES):
            scatter_copy(idx_ref, i, j, slot).start()
        return carry

    lax.fori_loop(0, SCATTER_ROWS // SUBLANES, issue_scatter, 0, unroll=2)

    @pl.when(k == nk - 1)
    def _():
        wait_scatter(k)
        wait_scatter(k - 1)


def _scatter_add(idx_flat, ye, gate_rows, x1, *, cap, tok_shift, boff):
    n_rows, d = ye.shape
    steps = n_rows // SCATTER_ROWS
    tiles_per_expert = cap // SCATTER_ROWS
    assert tiles_per_expert >= 2
    kern = functools.partial(_scatter_kernel, tiles_per_expert=tiles_per_expert,
                             tok_shift=tok_shift, boff=boff)
    idx3 = idx_flat.reshape(steps, 1, SCATTER_ROWS)
    return pl.pallas_call(
        kern,
        grid=(steps,),
        in_specs=[
            pl.BlockSpec((1, 1, SCATTER_ROWS), lambda k: (k, 0, 0), memory_space=pltpu.SMEM),
            pl.BlockSpec((1, 1, SCATTER_ROWS), lambda k: (jnp.minimum(k + 1, steps - 1), 0, 0),
                         memory_space=pltpu.SMEM),
            pl.BlockSpec((SCATTER_ROWS, d), lambda k: (k, 0)),
            pl.BlockSpec((1, SCATTER_ROWS, 1), lambda k: (k, 0, 0)),
            pl.BlockSpec(gate_rows.shape, lambda k: (0, 0)),
            pl.BlockSpec(memory_space=pl.ANY),
        ],
        out_specs=pl.BlockSpec(memory_space=pl.ANY),
        out_shape=jax.ShapeDtypeStruct(x1.shape, x1.dtype),
        scratch_shapes=[pltpu.VMEM((SCATTER_BUFS, SCATTER_ROWS // SUBLANES, SUBLANES, d), F32),
                        pltpu.SemaphoreType.DMA((SCATTER_BUFS,)),
                        pltpu.SemaphoreType.DMA((SCATTER_BUFS,))],
        input_output_aliases={5: 0},
        compiler_params=_cparams(("arbitrary",)),
        name="scatter_add",
    )(idx3, idx3, ye, idx_flat.reshape(steps, SCATTER_ROWS, 1), gate_rows, x1)


def _rope_tables(t):
    half = HEAD_DIM // 2
    quarter = half // 2
    freqs = ROPE_THETA ** (-(jnp.arange(quarter, dtype=F32) / quarter))
    rows = t // GRID_W
    row_ids = jnp.repeat(jnp.arange(rows, dtype=jnp.int32), GRID_W).astype(F32)
    col_ids = jnp.tile(jnp.arange(GRID_W, dtype=jnp.int32), rows).astype(F32)
    ang_r = row_ids[:, None] * freqs[None, :]
    ang_c = col_ids[:, None] * freqs[None, :]
    z = jnp.zeros_like(ang_r)
    cos_t = jnp.concatenate([jnp.cos(ang_r)] * 2 + [jnp.cos(ang_c)] * 2, axis=1)
    sa_t = jnp.concatenate([-jnp.sin(ang_r), z, -jnp.sin(ang_c), z], axis=1)
    sb_t = jnp.concatenate([z, jnp.sin(ang_r), z, jnp.sin(ang_c)], axis=1)
    return cos_t, sa_t, sb_t


def kernel(x_prompt, x_sample, c_prompt, c_sample, rel_bias_table, w_ada, b_ada, g_norm_mix, g_norm_ffn, w_in, g_q_a, g_k_a, g_q_b, g_k_b, g_out_a, g_out_b, w_out, w_router, w_gate, w_up, w_down):
    d = D_MODEL
    nbp, nbs = c_prompt.shape[0], c_sample.shape[0]
    c_all = jnp.concatenate([c_prompt, c_sample, jnp.zeros((16 - nbp - nbs, d), F32)], axis=0)
    mod = _ada(c_all, w_ada[0], b_ada[0])
    mod3 = mod.reshape(16 * 6, 1, d)
    gt2_all = mod[:, 5 * d:6 * d]

    scale = HEAD_DIM ** -0.5
    gains = jnp.stack([g_q_a[0] * scale, g_k_a[0], g_q_b[0] * scale, g_k_b[0]], axis=0)
    w_in_bf = w_in[0].astype(BF16)
    w_out_bf = w_out[0].astype(BF16)
    w_router_pad = jnp.pad(w_router[0], ((0, 0), (0, HEAD_DIM - N_EXPERTS))).astype(BF16)
    wg, wu, wd = w_gate[0], w_up[0], w_down[0]
    bias_tiles = _bias_tiles(rel_bias_table)
    gn1 = g_norm_mix[0].reshape(1, d)
    gn2 = g_norm_ffn[0].reshape(1, d)
    ga = g_out_a[0].reshape(1, WIDTH_A)
    gb = g_out_b[0].reshape(1, WIDTH_B)

    def run(x, boff):
        b, t, _ = x.shape
        cos_t, sa_t, sb_t = _rope_tables(t)
        proj = _inproj(x, mod3, boff, gn1, w_in_bf, gains, cos_t, sa_t, sb_t)
        oa = _attn_a(proj, bias_tiles)
        ob = _attn_b(proj)
        x1, h2, logits = _outproj(oa, ob, x, mod3, boff, ga, gb, gn2, w_out_bf, w_router_pad)
        n = b * t
        cap = EC_CAPACITY_FACTOR * n // N_EXPERTS
        lt = logits.reshape(n, -1)[:, :N_EXPERTS].T.reshape(N_EXPERTS, n // LANES, LANES)
        idx, g = _route(lt, cap)
        idx_flat = idx.reshape(N_EXPERTS * cap)
        ye = _ffn(idx.reshape(N_EXPERTS, cap), h2, wg, wu, wd, g.reshape(N_EXPERTS, cap))
        out = _scatter_add(idx_flat, ye.reshape(N_EXPERTS * cap, d), gt2_all, x1.reshape(n, d),
                           cap=cap, tok_shift=int(math.log2(t)), boff=boff)
        return out.reshape(b, t, d)

    return (run(x_prompt, 0), run(x_sample, nbp))
```

```python
import functools
import math

import jax
import jax.numpy as jnp
import numpy as np
from jax import lax
from jax.experimental import pallas as pl
from jax.experimental.pallas import tpu as pltpu

F32 = jnp.float32
BF16 = jnp.bfloat16

D_MODEL = 2048
HEAD_DIM = 128
N_HEADS_A = 8
N_HEADS_B = 8
N_KV_B = 2
GQA_GROUP = N_HEADS_B // N_KV_B
WIDTH_A = N_HEADS_A * HEAD_DIM
WIDTH_B = N_HEADS_B * HEAD_DIM
KV_WIDTH_B = N_KV_B * HEAD_DIM
IN_COLS = 3 * WIDTH_A + WIDTH_B + 2 * KV_WIDTH_B
DIL_CONFIGS = ((128, 1), (512, 4), (2048, 16))
NUM_BUCKETS = 32
MAX_DISTANCE = 1024
GRID_W = 64
ROPE_THETA = 10000.0
N_EXPERTS = 16
EC_CAPACITY_FACTOR = 2
D_EXPERT = 2048
EPS = 1e-6
NEG_INF = -1e30

VMEM_LIMIT_V7X = 56 * 1024 * 1024
LANES = 128
SUBLANES = 8
TOK_PITCH = 20

COL_QA, COL_KA, COL_VA = 0, 8, 16
COL_QB, COL_KB, COL_VB = 24, 32, 34

IN_TN = 512
IN_CHAINS = 4
OUT_CHAINS = 2
A_QB = 128
A_KB = 256
A_RADIUS = 64
A_PAD = A_RADIUS * 16
A_UNROLL = 8
B_SUB = 128


def _cparams(sem):
    return pltpu.CompilerParams(dimension_semantics=sem, vmem_limit_bytes=VMEM_LIMIT_V7X)


def _ada_kernel(c_ref, w_ref, b_ref, o_ref):
    c = c_ref[...]
    s = c * (1.0 / (1.0 + jnp.exp(-c)))
    o_ref[...] = jnp.dot(s.astype(BF16), w_ref[...].astype(BF16),
                         preferred_element_type=F32) + b_ref[...]


def _ada(c_all, w_ada, b_ada):
    rows, d = c_all.shape
    n = w_ada.shape[1]
    tn = 1024
    return pl.pallas_call(
        _ada_kernel,
        grid=(n // tn,),
        in_specs=[pl.BlockSpec((rows, d), lambda j: (0, 0)),
                  pl.BlockSpec((d, tn), lambda j: (0, j)),
                  pl.BlockSpec((1, tn), lambda j: (0, j))],
        out_specs=pl.BlockSpec((rows, tn), lambda j: (0, j)),
        out_shape=jax.ShapeDtypeStruct((rows, n), F32),
        compiler_params=_cparams(("arbitrary",)),
        name="ada_mod",
    )(c_all, w_ada, b_ada.reshape(1, n))


def _head_norm(a, g):
    ms = jnp.mean(a * a, axis=-1, keepdims=True)
    return a * lax.rsqrt(ms + EPS) * g


def _inproj_kernel(x_ref, sc_ref, sh_ref, gn_ref, w_ref, gains_ref, cos_ref, sa_ref, sb_ref,
                   o_ref, h_scr):
    j = pl.program_id(2)

    @pl.when(j == 0)
    def _():
        x = x_ref[0]
        ms = jnp.mean(x * x, axis=-1, keepdims=True)
        y = x * lax.rsqrt(ms + EPS) * gn_ref[...]
        h_scr[...] = (y * (1.0 + sc_ref[0]) + sh_ref[0]).astype(BF16)

    def rope(a, rows):
        return (a * cos_ref[rows, :] + pltpu.roll(a, 96, 1) * sa_ref[rows, :]
                + pltpu.roll(a, 32, 1) * sb_ref[rows, :])

    def plain(a, rows):
        return a

    def norm(g_row):
        return lambda a, rows: _head_norm(a, gains_ref[g_row:g_row + 1, :])

    def norm_rope(g_row):
        return lambda a, rows: rope(_head_norm(a, gains_ref[g_row:g_row + 1, :]), rows)

    def tile(head_fns):
        sub = h_scr.shape[0] // IN_CHAINS
        for c in range(IN_CHAINS):
            rows = slice(c * sub, (c + 1) * sub)
            acc = jnp.dot(h_scr[rows, :], w_ref[...], preferred_element_type=F32)
            for hh, fn in enumerate(head_fns):
                sl = slice(hh * HEAD_DIM, (hh + 1) * HEAD_DIM)
                o_ref[0, rows, sl] = fn(acc[:, sl], rows).astype(BF16)

    @pl.when(j < 2)
    def _():
        tile([norm(0)] * 4)

    @pl.when((j >= 2) & (j < 4))
    def _():
        tile([norm(1)] * 4)

    @pl.when((j >= 4) & (j < 6))
    def _():
        tile([plain] * 4)

    @pl.when((j >= 6) & (j < 8))
    def _():
        tile([norm_rope(2)] * 4)

    @pl.when(j == 8)
    def _():
        tile([norm_rope(3)] * 2 + [plain] * 2)


def _inproj(x, mod3, boff, g_norm, w_in_bf, gains, cos_t, sa_t, sb_t):
    b, t, d = x.shape
    tm = 1024
    nj = IN_COLS // IN_TN
    return pl.pallas_call(
        _inproj_kernel,
        grid=(b, t // tm, nj),
        in_specs=[
            pl.BlockSpec((1, tm, d), lambda bi, ti, j: (bi, ti, 0)),
            pl.BlockSpec((1, 1, d), lambda bi, ti, j: ((bi + boff) * 6 + 1, 0, 0)),
            pl.BlockSpec((1, 1, d), lambda bi, ti, j: ((bi + boff) * 6 + 0, 0, 0)),
            pl.BlockSpec((1, d), lambda bi, ti, j: (0, 0)),
            pl.BlockSpec((d, IN_TN), lambda bi, ti, j: (0, j)),
            pl.BlockSpec((4, HEAD_DIM), lambda bi, ti, j: (0, 0)),
            pl.BlockSpec((tm, HEAD_DIM), lambda bi, ti, j: (ti, 0)),
            pl.BlockSpec((tm, HEAD_DIM), lambda bi, ti, j: (ti, 0)),
            pl.BlockSpec((tm, HEAD_DIM), lambda bi, ti, j: (ti, 0)),
        ],
        out_specs=pl.BlockSpec((1, tm, IN_TN), lambda bi, ti, j: (bi, ti, j)),
        out_shape=jax.ShapeDtypeStruct((b, t, IN_COLS), BF16),
        scratch_shapes=[pltpu.VMEM((tm, d), BF16)],
        compiler_params=_cparams(("arbitrary", "arbitrary", "arbitrary")),
        name="inproj",
    )(x, mod3, mod3, g_norm, w_in_bf, gains, cos_t, sa_t, sb_t)


def _attn_a_kernel(q_ref, k_ref, v_ref, bias_ref, o_ref, qf, kf, vf, acc, mm, ll, *, t):
    zpad = jnp.zeros((A_PAD, HEAD_DIM), F32)
    kf[0:A_PAD, :] = zpad
    vf[0:A_PAD, :] = zpad
    kf[A_PAD + t:A_PAD + t + A_PAD, :] = zpad
    vf[A_PAD + t:A_PAD + t + A_PAD, :] = zpad
    kf[A_PAD:A_PAD + t, :] = k_ref[0].astype(F32)
    vf[A_PAD:A_PAD + t, :] = v_ref[0].astype(F32)
    qf[...] = q_ref[0].astype(F32)
    ones = jnp.ones((A_KB, HEAD_DIM), BF16)

    for bi, (_, dil) in enumerate(DIL_CONFIGS):
        sub_len = t // dil
        nmb = sub_len // A_QB
        shift = int(math.log2(nmb))

        def rows(start, size, dil=dil):
            return pl.ds(start, size) if dil == 1 else pl.ds(start, size, stride=dil)

        def body(idx, carry, bi=bi, dil=dil, sub_len=sub_len, nmb=nmb, shift=shift, rows=rows):
            rho = lax.shift_right_logical(idx, shift)
            mb = lax.bitwise_and(idx, nmb - 1)
            qstart = rho + mb * (A_QB * dil)
            kstart = A_PAD + qstart - A_RADIUS * dil
            q = qf[rows(qstart, A_QB), :].astype(BF16)
            k = kf[rows(kstart, A_KB), :].astype(BF16)
            v = vf[rows(kstart, A_KB), :].astype(BF16)
            s = lax.dot_general(q, k, (((1,), (1,)), ((), ())), preferred_element_type=F32)
            s = s + bias_ref[bi, 0]
            kidx = mb * A_QB - A_RADIUS + lax.broadcasted_iota(jnp.int32, (1, A_KB), 1)
            s = jnp.where((kidx >= 0) & (kidx < sub_len), s, NEG_INF)
            mblk = jnp.max(s, axis=-1, keepdims=True)
            v1 = jnp.concatenate([v, ones], axis=1)
            p = jnp.exp(s - mblk).astype(BF16)
            pv = jnp.dot(p, v1, preferred_element_type=F32)
            acc[bi, rows(qstart, A_QB), :] = pv[:, :HEAD_DIM]
            ll[bi, rows(qstart, A_QB), :] = pv[:, HEAD_DIM:]
            mm[bi, rows(qstart, A_QB), :] = jnp.broadcast_to(mblk, (A_QB, HEAD_DIM))
            return carry

        lax.fori_loop(0, dil * nmb, body, 0, unroll=A_UNROLL)

    m0, m1, m2 = mm[0], mm[1], mm[2]
    mtop = jnp.maximum(m0, jnp.maximum(m1, m2))
    w0, w1, w2 = jnp.exp(m0 - mtop), jnp.exp(m1 - mtop), jnp.exp(m2 - mtop)
    num = w0 * acc[0] + w1 * acc[1] + w2 * acc[2]
    den = w0 * ll[0] + w1 * ll[1] + w2 * ll[2]
    o_ref[0] = (num / den).astype(BF16)


def _attn_a(proj, bias_tiles):
    b, t, _ = proj.shape
    kern = functools.partial(_attn_a_kernel, t=t)
    return pl.pallas_call(
        kern,
        grid=(b, N_HEADS_A),
        in_specs=[
            pl.BlockSpec((1, t, HEAD_DIM), lambda bi, h: (bi, 0, COL_QA + h)),
            pl.BlockSpec((1, t, HEAD_DIM), lambda bi, h: (bi, 0, COL_KA + h)),
            pl.BlockSpec((1, t, HEAD_DIM), lambda bi, h: (bi, 0, COL_VA + h)),
            pl.BlockSpec((3, 1, A_QB, A_KB), lambda bi, h: (0, h, 0, 0)),
        ],
        out_specs=pl.BlockSpec((1, t, HEAD_DIM), lambda bi, h: (bi, 0, h)),
        out_shape=jax.ShapeDtypeStruct((b, t, WIDTH_A), BF16),
        scratch_shapes=[
            pltpu.VMEM((t, HEAD_DIM), F32),
            pltpu.VMEM((t + 2 * A_PAD, HEAD_DIM), F32),
            pltpu.VMEM((t + 2 * A_PAD, HEAD_DIM), F32),
            pltpu.VMEM((len(DIL_CONFIGS), t, HEAD_DIM), F32),
            pltpu.VMEM((len(DIL_CONFIGS), t, HEAD_DIM), F32),
            pltpu.VMEM((len(DIL_CONFIGS), t, HEAD_DIM), F32),
        ],
        compiler_params=_cparams(("arbitrary", "arbitrary")),
        name="attn_dilated",
    )(proj, proj, proj, bias_tiles)


def _t5_bucket(rel):
    nb = NUM_BUCKETS // 2
    max_exact = nb // 2
    sign_off = np.where(rel > 0, nb, 0)
    n = np.abs(rel)
    nf = np.maximum(n, 1).astype(np.float32)
    large = max_exact + (np.log(nf / np.float32(max_exact))
                         / np.float32(math.log(MAX_DISTANCE / max_exact))
                         * np.float32(nb - max_exact)).astype(np.int32)
    large = np.minimum(large, nb - 1)
    return (sign_off + np.where(n < max_exact, n, large)).astype(np.int32)


def _bias_tiles(rel_bias_table):
    qi = np.arange(A_QB, dtype=np.int32)[:, None]
    kj = np.arange(A_KB, dtype=np.int32)[None, :]
    rel = kj - A_RADIUS - qi
    in_band = jnp.asarray(np.abs(rel) <= A_RADIUS)
    tiles = []
    for _, dil in DIL_CONFIGS:
        onehot = jnp.asarray(_t5_bucket(rel * dil)[..., None]
                             == np.arange(NUM_BUCKETS, dtype=np.int32), F32)
        bias = jnp.einsum('qkn,nh->hqk', onehot, rel_bias_table.astype(F32),
                          precision=lax.Precision.HIGHEST)
        tiles.append(jnp.where(in_band[None], bias, NEG_INF))
    return jnp.stack(tiles, axis=0)


def _attn_b_kernel(q_ref, k_ref, v_ref, o_ref, v1_scr, *, tq):
    @pl.when(pl.program_id(2) == 0)
    def _():
        v1_scr[:, 0:HEAD_DIM] = v_ref[0]
        v1_scr[:, HEAD_DIM:2 * HEAD_DIM] = jnp.ones(v_ref.shape[1:], BF16)

    for c in range(tq // B_SUB):
        q = q_ref[0, c * B_SUB:(c + 1) * B_SUB, :]
        qs = jnp.concatenate([q[:, i * HEAD_DIM:(i + 1) * HEAD_DIM] for i in range(GQA_GROUP)],
                             axis=0)
        s = lax.dot_general(qs, k_ref[0], (((1,), (1,)), ((), ())), preferred_element_type=F32)
        m = jnp.max(s, axis=-1, keepdims=True)
        p = jnp.exp(s - m).astype(BF16)
        pv = jnp.dot(p, v1_scr[...], preferred_element_type=F32)
        o = pv[:, :HEAD_DIM] / pv[:, HEAD_DIM:]
        o_ref[0, c * B_SUB:(c + 1) * B_SUB, :] = jnp.concatenate(
            [o[i * B_SUB:(i + 1) * B_SUB] for i in range(GQA_GROUP)], axis=1).astype(BF16)


def _attn_b(proj):
    b, t, _ = proj.shape
    tq = 4 * B_SUB
    gw = GQA_GROUP * HEAD_DIM
    return pl.pallas_call(
        functools.partial(_attn_b_kernel, tq=tq),
        grid=(b, N_KV_B, t // tq),
        in_specs=[
            pl.BlockSpec((1, tq, gw), lambda bi, g, qi: (bi, qi, COL_QB // GQA_GROUP + g)),
            pl.BlockSpec((1, t, HEAD_DIM), lambda bi, g, qi: (bi, 0, COL_KB + g)),
            pl.BlockSpec((1, t, HEAD_DIM), lambda bi, g, qi: (bi, 0, COL_VB + g)),
        ],
        out_specs=pl.BlockSpec((1, tq, gw), lambda bi, g, qi: (bi, qi, g)),
        out_shape=jax.ShapeDtypeStruct((b, t, WIDTH_B), BF16),
        scratch_shapes=[pltpu.VMEM((t, 2 * HEAD_DIM), BF16)],
        compiler_params=_cparams(("arbitrary", "arbitrary", "arbitrary")),
        name="attn_gqa",
    )(proj, proj, proj)


def _outproj_kernel(oa_ref, ob_ref, x_ref, gt_ref, sc_ref, sh_ref, ga_ref, gb_ref, gn_ref,
                    w_ref, wr_ref, x1_ref, h2_ref, lg_ref):
    def wide_norm(o, g_ref_):
        o = o.astype(F32)
        ms = jnp.mean(o * o, axis=-1, keepdims=True)
        return (o * lax.rsqrt(ms + EPS) * g_ref_[...]).astype(BF16)

    tm, dd = x_ref.shape[1:]
    slab = dd // LANES
    sub = tm // OUT_CHAINS
    for c in range(OUT_CHAINS):
        rows = slice(c * sub, (c + 1) * sub)
        na = wide_norm(oa_ref[0, rows, :], ga_ref)
        nb = wide_norm(ob_ref[0, rows, :], gb_ref)
        mix = (jnp.dot(na, w_ref[0:WIDTH_A, :], preferred_element_type=F32)
               + jnp.dot(nb, w_ref[WIDTH_A:WIDTH_A + WIDTH_B, :], preferred_element_type=F32))
        x1 = x_ref[0, rows, :] + gt_ref[0] * mix
        x1_ref[0, rows, :] = x1
        ms = jnp.mean(x1 * x1, axis=-1, keepdims=True)
        h2 = (x1 * lax.rsqrt(ms + EPS) * gn_ref[...]) * (1.0 + sc_ref[0]) + sh_ref[0]
        lg_ref[0, rows, :] = jnp.dot(h2.astype(BF16), wr_ref[...], preferred_element_type=F32)
        for s in range(TOK_PITCH):
            val = h2[:, s * LANES:(s + 1) * LANES] if s < slab else jnp.zeros((sub, LANES), F32)
            h2_ref[pl.ds(c * sub * TOK_PITCH + s, sub, stride=TOK_PITCH), :] = val


def _outproj(oa, ob, x, mod3, boff, g_out_a, g_out_b, g_norm_ffn, w_out_bf, w_router_pad):
    b, t, d = x.shape
    tm = 512
    nr = w_router_pad.shape[1]
    assert d // LANES <= TOK_PITCH
    row = lambda k: (lambda bi, ti: ((bi + boff) * 6 + k, 0, 0))
    return pl.pallas_call(
        _outproj_kernel,
        grid=(b, t // tm),
        in_specs=[
            pl.BlockSpec((1, tm, WIDTH_A), lambda bi, ti: (bi, ti, 0)),
            pl.BlockSpec((1, tm, WIDTH_B), lambda bi, ti: (bi, ti, 0)),
            pl.BlockSpec((1, tm, d), lambda bi, ti: (bi, ti, 0)),
            pl.BlockSpec((1, 1, d), row(2)),
            pl.BlockSpec((1, 1, d), row(4)),
            pl.BlockSpec((1, 1, d), row(3)),
            pl.BlockSpec((1, WIDTH_A), lambda bi, ti: (0, 0)),
            pl.BlockSpec((1, WIDTH_B), lambda bi, ti: (0, 0)),
            pl.BlockSpec((1, d), lambda bi, ti: (0, 0)),
            pl.BlockSpec((WIDTH_A + WIDTH_B, d), lambda bi, ti: (0, 0)),
            pl.BlockSpec((d, nr), lambda bi, ti: (0, 0)),
        ],
        out_specs=[
            pl.BlockSpec((1, tm, d), lambda bi, ti: (bi, ti, 0)),
            pl.BlockSpec((tm * TOK_PITCH, LANES), lambda bi, ti: (bi * (t // tm) + ti, 0)),
            pl.BlockSpec((1, tm, nr), lambda bi, ti: (bi, ti, 0)),
        ],
        out_shape=[jax.ShapeDtypeStruct((b, t, d), F32),
                   jax.ShapeDtypeStruct((b * t * TOK_PITCH, LANES), F32),
                   jax.ShapeDtypeStruct((b, t, nr), F32)],
        compiler_params=_cparams(("arbitrary", "arbitrary")),
        name="outproj",
    )(oa, ob, x, mod3, mod3, mod3, g_out_a, g_out_b, g_norm_ffn, w_out_bf, w_router_pad)


FFN_OUT_CHUNK = 512


def _ffn_kernel(idx_ref, idx_next_ref, h_ref, wg_ref, wu_ref, wd_ref, g_ref, o_ref,
                xslab, xb, hm, wdb, sem, *, tm, slab):
    f = pl.program_id(2)
    k = pl.program_id(0) * pl.num_programs(1) + pl.program_id(1)
    nk = pl.num_programs(0) * pl.num_programs(1)

    def issue_gather(ids_ref):
        def body(r, carry):
            pltpu.make_async_copy(h_ref.at[pl.ds(ids_ref[0, 0, r] * TOK_PITCH, slab), :],
                                  xslab.at[pl.ds(r * TOK_PITCH, slab), :], sem.at[0]).start()
            return carry
        lax.fori_loop(0, tm, body, 0, unroll=8)

    @pl.when(f == 0)
    def _():
        @pl.when(k == 0)
        def _():
            issue_gather(idx_ref)

        pltpu.make_async_copy(h_ref.at[pl.ds(0, tm * slab), :], xslab.at[pl.ds(0, tm * slab), :],
                              sem.at[0]).wait()
        for s in range(slab):
            xb[:, s * LANES:(s + 1) * LANES] = xslab[pl.ds(s, tm, stride=TOK_PITCH), :].astype(BF16)

        @pl.when(k + 1 < nk)
        def _():
            issue_gather(idx_next_ref)

    @pl.when(pl.program_id(1) == 0)
    def _():
        wdb[f] = wd_ref[0].astype(BF16)

    x = xb[...]
    a = jnp.dot(x, wg_ref[0].astype(BF16), preferred_element_type=F32)
    u = jnp.dot(x, wu_ref[0].astype(BF16), preferred_element_type=F32)
    hm[f] = (a * (1.0 / (1.0 + jnp.exp(-a))) * u).astype(BF16)

    @pl.when(f == pl.num_programs(2) - 1)
    def _():
        nf = hm.shape[0]
        for c in range(o_ref.shape[2] // FFN_OUT_CHUNK):
            cols = slice(c * FFN_OUT_CHUNK, (c + 1) * FFN_OUT_CHUNK)
            y = jnp.dot(hm[0], wdb[0, :, cols], preferred_element_type=F32)
            for j in range(1, nf):
                y += jnp.dot(hm[j], wdb[j, :, cols], preferred_element_type=F32)
            o_ref[0, :, cols] = (y * g_ref[0]).astype(o_ref.dtype)


def _ffn(idx, h2slab, wg, wu, wd, g):
    e, cap = idx.shape
    d = wg.shape[1]
    slab = d // LANES
    fdim = wg.shape[2]
    tm = min(cap, 1024)
    tf = 256
    nf = fdim // tf
    mt = cap // tm
    nk = e * mt
    idx3 = idx.reshape(nk, 1, tm)
    return pl.pallas_call(
        functools.partial(_ffn_kernel, tm=tm, slab=slab),
        grid=(e, mt, fdim // tf),
        in_specs=[
            pl.BlockSpec((1, 1, tm), lambda ei, mi, fi: (ei * mt + mi, 0, 0),
                         memory_space=pltpu.SMEM),
            pl.BlockSpec((1, 1, tm), lambda ei, mi, fi: (jnp.minimum(ei * mt + mi + 1, nk - 1), 0, 0),
                         memory_space=pltpu.SMEM),
            pl.BlockSpec(memory_space=pl.ANY),
            pl.BlockSpec((1, d, tf), lambda ei, mi, fi: (ei, 0, fi)),
            pl.BlockSpec((1, d, tf), lambda ei, mi, fi: (ei, 0, fi)),
            pl.BlockSpec((1, tf, d), lambda ei, mi, fi: (ei, jnp.where(mi == 0, fi, nf - 1), 0)),
            pl.BlockSpec((1, tm, 1), lambda ei, mi, fi: (ei, mi, 0)),
        ],
        out_specs=pl.BlockSpec((1, tm, d), lambda ei, mi, fi: (ei, mi, 0)),
        out_shape=jax.ShapeDtypeStruct((e, cap, d), BF16),
        scratch_shapes=[pltpu.VMEM((tm * TOK_PITCH, LANES), F32),
                        pltpu.VMEM((tm, d), BF16),
                        pltpu.VMEM((nf, tm, tf), BF16),
                        pltpu.VMEM((nf, tf, d), BF16),
                        pltpu.SemaphoreType.DMA((1,))],
        compiler_params=_cparams(("arbitrary", "arbitrary", "arbitrary")),
        name="expert_ffn",
    )(idx3, idx3, h2slab, wg, wu, wd, g.reshape(e, cap, 1))


def _prefix_counts(mask_f32, upper, lower):
    within = jnp.dot(mask_f32.astype(BF16), upper, preferred_element_type=F32)
    tot = jnp.broadcast_to(within[:, LANES - 1:LANES], within.shape)
    offs = jnp.dot(lower, tot.astype(BF16), preferred_element_type=F32)
    return within, offs


def _route_kernel(lt_ref, idx_ref, g_ref, pos_ref, offs_ref, aff_scr, thr_scr, *, cap):
    e = pl.program_id(0)
    nchunk = lt_ref.shape[1]
    capf = jnp.float32(cap)

    @pl.when(e == 0)
    def _():
        l = lt_ref[...]
        ex = jnp.exp(l - jnp.max(l, axis=0, keepdims=True))
        aff = ex / jnp.sum(ex, axis=0, keepdims=True)
        aff_scr[...] = aff

        def bit_step(i, thr_bits):
            cand = thr_bits | lax.shift_left(jnp.int32(1), 30 - i)
            cnt = jnp.sum((aff >= lax.bitcast_convert_type(cand, F32)).astype(F32),
                          axis=(1, 2), keepdims=True)
            return jnp.where(cnt >= capf, cand, thr_bits)

        thr_bits = lax.fori_loop(0, 31, bit_step, jnp.zeros((lt_ref.shape[0], 1, 1), jnp.int32))
        thr_scr[...] = jnp.broadcast_to(lax.bitcast_convert_type(thr_bits, F32), thr_scr.shape)

    a = aff_scr[e]
    thr = thr_scr[e][0:1, 0:1]

    def count(m):
        return jnp.sum(m.astype(F32), axis=(0, 1), keepdims=True)

    ri = lax.broadcasted_iota(jnp.int32, (LANES, LANES), 0)
    ci = lax.broadcasted_iota(jnp.int32, (LANES, LANES), 1)
    upper = (ri <= ci).astype(BF16)
    rc = lax.broadcasted_iota(jnp.int32, (nchunk, nchunk), 0)
    cc = lax.broadcasted_iota(jnp.int32, (nchunk, nchunk), 1)
    lower = (cc < rc).astype(BF16)

    gt = a > thr
    eq = a == thr
    need = capf - count(gt)
    eq_within, eq_offs = _prefix_counts(eq.astype(F32), upper, lower)
    sel = gt | (eq & ((eq_within + eq_offs) <= need))
    within, offs = _prefix_counts(sel.astype(F32), upper, lower)
    pos_ref[0] = jnp.where(sel, within + offs - 1.0, -1.0).astype(jnp.int32)
    offs_ref[0] = offs.astype(jnp.int32)

    offs_col = offs[:, 0:1]
    ends_col = offs_col + within[:, LANES - 1:LANES]
    slot = lax.broadcasted_iota(jnp.int32, (1, cap), 1).astype(F32)
    cstar = jnp.sum((ends_col <= slot).astype(F32), axis=0, keepdims=True)
    chunk_oh = lax.broadcasted_iota(jnp.int32, (nchunk, cap), 0).astype(F32) == cstar
    rank = slot - jnp.sum(jnp.where(chunk_oh, offs_col, 0.0), axis=0, keepdims=True)
    oh = chunk_oh.astype(BF16)
    tdot = functools.partial(lax.dot_general, dimension_numbers=(((0,), (0,)), ((), ())),
                             preferred_element_type=F32)
    wsel = tdot(within.astype(BF16), oh)
    lstar = jnp.sum((wsel <= rank).astype(F32), axis=0, keepdims=True)
    idx_ref[0] = (cstar * LANES + lstar).astype(jnp.int32)

    a1 = a.astype(BF16)
    r1 = a - a1.astype(F32)
    a2 = r1.astype(BF16)
    a3 = (r1 - a2.astype(F32)).astype(BF16)
    asel = (tdot(a1, oh) + tdot(a2, oh)) + tdot(a3, oh)
    lane_oh = lax.broadcasted_iota(jnp.int32, (LANES, cap), 0).astype(F32) == lstar
    g_ref[0] = jnp.sum(jnp.where(lane_oh, asel, 0.0), axis=0, keepdims=True)


def _route(lt, cap):
    e, nchunk, _ = lt.shape
    return pl.pallas_call(
        functools.partial(_route_kernel, cap=cap),
        grid=(e,),
        in_specs=[pl.BlockSpec((e, nchunk, LANES), lambda ei: (0, 0, 0))],
        out_specs=[pl.BlockSpec((1, 1, cap), lambda ei: (ei, 0, 0)),
                   pl.BlockSpec((1, 1, cap), lambda ei: (ei, 0, 0)),
                   pl.BlockSpec((1, nchunk, LANES), lambda ei: (ei, 0, 0)),
                   pl.BlockSpec((1, nchunk, LANES), lambda ei: (ei, 0, 0))],
        out_shape=[jax.ShapeDtypeStruct((e, 1, cap), jnp.int32),
                   jax.ShapeDtypeStruct((e, 1, cap), F32),
                   jax.ShapeDtypeStruct((e, nchunk, LANES), jnp.int32),
                   jax.ShapeDtypeStruct((e, nchunk, LANES), jnp.int32)],
        scratch_shapes=[pltpu.VMEM((e, nchunk, LANES), F32),
                        pltpu.VMEM((e, SUBLANES, LANES), F32)],
        compiler_params=_cparams(("arbitrary",)),
        name="route",
    )(lt)


COMB_TOK = 256
COMB_WIN = 64
COMB_ALIGN = 16


def _combine_kernel(tab_ref, tab_next_ref, startv_ref, pos_ref, x1_ref, gate_ref, ye_ref, o_ref,
                    ybuf, ybuf_x, sems, sem_x, *, cap):
    k = pl.program_id(0)
    nk = pl.num_programs(0)
    slot = lax.rem(k, 2)
    ne = ye_ref.shape[0]
    last_start = cap - COMB_WIN

    def window_copy(e, logical_start, dst, sem):
        row0 = pl.multiple_of(jnp.minimum(logical_start, last_start), COMB_ALIGN)
        return pltpu.make_async_copy(ye_ref.at[e, pl.ds(row0, COMB_WIN), :],
                                     dst.at[pl.ds(e * COMB_WIN, COMB_WIN), :], sem)

    def issue(tab, sl):
        for e in range(ne):
            window_copy(e, tab[0, 0, e], ybuf.at[sl], sems.at[sl]).start()

    @pl.when(k == 0)
    def _():
        issue(tab_ref, slot)

    @pl.when(k + 1 < nk)
    def _():
        issue(tab_next_ref, 1 - slot)

    pltpu.make_async_copy(ybuf.at[slot], ybuf.at[slot], sems.at[slot]).wait()

    pos = pos_ref[...]
    start0 = startv_ref[0]
    width = ne * COMB_WIN
    col = lax.broadcasted_iota(jnp.int32, (ne, width), 1)
    expand = (lax.shift_right_logical(col, int(math.log2(COMB_WIN)))
              == lax.broadcasted_iota(jnp.int32, (ne, width), 0)).astype(BF16)
    lane_in_win = lax.bitwise_and(lax.broadcasted_iota(jnp.int32, (1, width), 1),
                                  COMB_WIN - 1).astype(F32)

    def place(p):
        logical = start0 + p * COMB_WIN
        rel = pos - logical
        inside = (pos >= 0) & (rel >= 0) & (rel < COMB_WIN)
        row = jnp.where(inside, pos - jnp.minimum(logical, last_start), -1)
        spread = jnp.dot(row.astype(F32).astype(BF16), expand, preferred_element_type=F32)
        return (spread == lane_in_win).astype(BF16)

    acc = jnp.dot(place(0), ybuf[slot], preferred_element_type=F32)

    def extra_pass(p, acc):
        for e in range(ne):
            window_copy(e, tab_ref[0, 0, e] + p * COMB_WIN, ybuf_x, sem_x.at[0]).start()
        pltpu.make_async_copy(ybuf_x, ybuf_x, sem_x.at[0]).wait()
        return acc + jnp.dot(place(p), ybuf_x[...], preferred_element_type=F32)

    acc = lax.fori_loop(1, tab_ref[0, 0, ne], extra_pass, acc)
    o_ref[...] = x1_ref[...] + gate_ref[0] * acc


def _combine(pos, offs, ye, x1, mod3, *, t, boff):
    ne, cap, d = ye.shape
    n = x1.shape[0]
    ntiles = n // COMB_TOK
    chunks_per_tile = COMB_TOK // LANES
    before = offs[:, ::chunks_per_tile, 0].T
    after = jnp.concatenate([before[1:], jnp.full((1, ne), cap, jnp.int32)], axis=0)
    starts = before - before % COMB_ALIGN
    passes = jnp.maximum(1, (jnp.max(after - starts, axis=1) + COMB_WIN - 1) // COMB_WIN)
    table = jnp.concatenate([starts, passes[:, None]], axis=1).astype(jnp.int32)
    table = table.reshape(ntiles, 1, ne + 1)
    startv = starts.astype(jnp.int32).reshape(ntiles, 1, ne)
    pos_tm = pos.reshape(ne, n).T
    tiles_per_seq = t // COMB_TOK
    return pl.pallas_call(
        functools.partial(_combine_kernel, cap=cap),
        grid=(ntiles,),
        in_specs=[
            pl.BlockSpec((1, 1, ne + 1), lambda k: (k, 0, 0), memory_space=pltpu.SMEM),
            pl.BlockSpec((1, 1, ne + 1), lambda k: (jnp.minimum(k + 1, ntiles - 1), 0, 0),
                         memory_space=pltpu.SMEM),
            pl.BlockSpec((1, 1, ne), lambda k: (k, 0, 0)),
            pl.BlockSpec((COMB_TOK, ne), lambda k: (k, 0)),
            pl.BlockSpec((COMB_TOK, d), lambda k: (k, 0)),
            pl.BlockSpec((1, 1, d), lambda k: ((k // tiles_per_seq + boff) * 6 + 5, 0, 0)),
            pl.BlockSpec(memory_space=pl.ANY),
        ],
        out_specs=pl.BlockSpec((COMB_TOK, d), lambda k: (k, 0)),
        out_shape=jax.ShapeDtypeStruct(x1.shape, x1.dtype),
        scratch_shapes=[pltpu.VMEM((2, ne * COMB_WIN, d), BF16),
                        pltpu.VMEM((ne * COMB_WIN, d), BF16),
                        pltpu.SemaphoreType.DMA((2,)),
                        pltpu.SemaphoreType.DMA((1,))],
        compiler_params=_cparams(("arbitrary",)),
        name="combine",
    )(table, table, startv, pos_tm, x1, mod3, ye)


def _rope_tables(t):
    half = HEAD_DIM // 2
    quarter = half // 2
    freqs = ROPE_THETA ** (-(jnp.arange(quarter, dtype=F32) / quarter))
    rows = t // GRID_W
    row_ids = jnp.repeat(jnp.arange(rows, dtype=jnp.int32), GRID_W).astype(F32)
    col_ids = jnp.tile(jnp.arange(GRID_W, dtype=jnp.int32), rows).astype(F32)
    ang_r = row_ids[:, None] * freqs[None, :]
    ang_c = col_ids[:, None] * freqs[None, :]
    z = jnp.zeros_like(ang_r)
    cos_t = jnp.concatenate([jnp.cos(ang_r)] * 2 + [jnp.cos(ang_c)] * 2, axis=1)
    sa_t = jnp.concatenate([-jnp.sin(ang_r), z, -jnp.sin(ang_c), z], axis=1)
    sb_t = jnp.concatenate([z, jnp.sin(ang_r), z, jnp.sin(ang_c)], axis=1)
    return cos_t, sa_t, sb_t


def kernel(x_prompt, x_sample, c_prompt, c_sample, rel_bias_table, w_ada, b_ada, g_norm_mix, g_norm_ffn, w_in, g_q_a, g_k_a, g_q_b, g_k_b, g_out_a, g_out_b, w_out, w_router, w_gate, w_up, w_down):
    d = D_MODEL
    nbp, nbs = c_prompt.shape[0], c_sample.shape[0]
    c_all = jnp.concatenate([c_prompt, c_sample, jnp.zeros((16 - nbp - nbs, d), F32)], axis=0)
    mod = _ada(c_all, w_ada[0], b_ada[0])
    mod3 = mod.reshape(16 * 6, 1, d)

    scale = HEAD_DIM ** -0.5
    gains = jnp.stack([g_q_a[0] * scale, g_k_a[0], g_q_b[0] * scale, g_k_b[0]], axis=0)
    w_in_bf = w_in[0].astype(BF16)
    w_out_bf = w_out[0].astype(BF16)
    w_router_pad = jnp.pad(w_router[0], ((0, 0), (0, HEAD_DIM - N_EXPERTS))).astype(BF16)
    wg, wu, wd = w_gate[0], w_up[0], w_down[0]
    bias_tiles = _bias_tiles(rel_bias_table)
    gn1 = g_norm_mix[0].reshape(1, d)
    gn2 = g_norm_ffn[0].reshape(1, d)
    ga = g_out_a[0].reshape(1, WIDTH_A)
    gb = g_out_b[0].reshape(1, WIDTH_B)

    def run(x, boff):
        b, t, _ = x.shape
        cos_t, sa_t, sb_t = _rope_tables(t)
        proj = _inproj(x, mod3, boff, gn1, w_in_bf, gains, cos_t, sa_t, sb_t)
        oa = _attn_a(proj, bias_tiles)
        ob = _attn_b(proj)
        x1, h2, logits = _outproj(oa, ob, x, mod3, boff, ga, gb, gn2, w_out_bf, w_router_pad)
        n = b * t
        cap = EC_CAPACITY_FACTOR * n // N_EXPERTS
        lt = logits.reshape(n, -1)[:, :N_EXPERTS].T.reshape(N_EXPERTS, n // LANES, LANES)
        idx, g, pos, offs = _route(lt, cap)
        ye = _ffn(idx.reshape(N_EXPERTS, cap), h2, wg, wu, wd, g.reshape(N_EXPERTS, cap))
        out = _combine(pos, offs, ye, x1.reshape(n, d), mod3, t=t, boff=boff)
        return out.reshape(b, t, d)

    return (run(x_prompt, 0), run(x_sample, nbp))
```

```python
import functools
import math

import jax
import jax.numpy as jnp
import numpy as np
from jax import lax
from jax.experimental import pallas as pl
from jax.experimental.pallas import tpu as pltpu

F32 = jnp.float32
BF16 = jnp.bfloat16

D_MODEL = 2048
HEAD_DIM = 128
N_HEADS_A = 8
N_HEADS_B = 8
N_KV_B = 2
GQA_GROUP = N_HEADS_B // N_KV_B
WIDTH_A = N_HEADS_A * HEAD_DIM
WIDTH_B = N_HEADS_B * HEAD_DIM
KV_WIDTH_B = N_KV_B * HEAD_DIM
IN_COLS = 3 * WIDTH_A + WIDTH_B + 2 * KV_WIDTH_B
DIL_CONFIGS = ((128, 1), (512, 4), (2048, 16))
NUM_BUCKETS = 32
MAX_DISTANCE = 1024
GRID_W = 64
ROPE_THETA = 10000.0
N_EXPERTS = 16
EC_CAPACITY_FACTOR = 2
D_EXPERT = 2048
EPS = 1e-6
NEG_INF = -1e30

VMEM_LIMIT_V7X = 56 * 1024 * 1024
LANES = 128
SUBLANES = 8
TOK_PITCH = 20

COL_QA, COL_KA, COL_VA = 0, 8, 16
COL_QB, COL_KB, COL_VB = 24, 32, 34

IN_TN = 512
IN_CHAINS = 4
OUT_CHAINS = 2
A_QB = 128
A_KB = 256
A_RADIUS = 64
A_PAD = A_RADIUS * 16
A_UNROLL = 8
B_SUB = 128
B_SCORE_ELEMS = 8 * 1024 * 1024


def _cparams(sem):
    return pltpu.CompilerParams(dimension_semantics=sem, vmem_limit_bytes=VMEM_LIMIT_V7X)


def _ada_kernel(c_ref, w_ref, b_ref, o_ref):
    c = c_ref[...]
    s = c * (1.0 / (1.0 + jnp.exp(-c)))
    o_ref[...] = jnp.dot(s.astype(BF16), w_ref[...].astype(BF16),
                         preferred_element_type=F32) + b_ref[...]


def _ada(c_all, w_ada, b_ada):
    rows, d = c_all.shape
    n = w_ada.shape[1]
    tn = 1024
    return pl.pallas_call(
        _ada_kernel,
        grid=(n // tn,),
        in_specs=[pl.BlockSpec((rows, d), lambda j: (0, 0)),
                  pl.BlockSpec((d, tn), lambda j: (0, j)),
                  pl.BlockSpec((1, tn), lambda j: (0, j))],
        out_specs=pl.BlockSpec((rows, tn), lambda j: (0, j)),
        out_shape=jax.ShapeDtypeStruct((rows, n), F32),
        compiler_params=_cparams(("arbitrary",)),
        name="ada_mod",
    )(c_all, w_ada, b_ada.reshape(1, n))


def _head_norm(a, g):
    ms = jnp.mean(a * a, axis=-1, keepdims=True)
    return a * lax.rsqrt(ms + EPS) * g


def _inproj_kernel(x_ref, sc_ref, sh_ref, gn_ref, w_ref, gains_ref, cos_ref, sa_ref, sb_ref,
                   o_ref, h_scr):
    j = pl.program_id(2)

    @pl.when(j == 0)
    def _():
        x = x_ref[0]
        ms = jnp.mean(x * x, axis=-1, keepdims=True)
        y = x * lax.rsqrt(ms + EPS) * gn_ref[...]
        h_scr[...] = (y * (1.0 + sc_ref[0]) + sh_ref[0]).astype(BF16)

    def rope(a, rows):
        return (a * cos_ref[rows, :] + pltpu.roll(a, 96, 1) * sa_ref[rows, :]
                + pltpu.roll(a, 32, 1) * sb_ref[rows, :])

    def plain(a, rows):
        return a

    def norm(g_row):
        return lambda a, rows: _head_norm(a, gains_ref[g_row:g_row + 1, :])

    def norm_rope(g_row):
        return lambda a, rows: rope(_head_norm(a, gains_ref[g_row:g_row + 1, :]), rows)

    def tile(head_fns):
        sub = h_scr.shape[0] // IN_CHAINS
        for c in range(IN_CHAINS):
            rows = slice(c * sub, (c + 1) * sub)
            acc = jnp.dot(h_scr[rows, :], w_ref[...], preferred_element_type=F32)
            for hh, fn in enumerate(head_fns):
                sl = slice(hh * HEAD_DIM, (hh + 1) * HEAD_DIM)
                o_ref[0, rows, sl] = fn(acc[:, sl], rows).astype(BF16)

    @pl.when(j < 2)
    def _():
        tile([norm(0)] * 4)

    @pl.when((j >= 2) & (j < 4))
    def _():
        tile([norm(1)] * 4)

    @pl.when((j >= 4) & (j < 6))
    def _():
        tile([plain] * 4)

    @pl.when((j >= 6) & (j < 8))
    def _():
        tile([norm_rope(2)] * 4)

    @pl.when(j == 8)
    def _():
        tile([norm_rope(3)] * 2 + [plain] * 2)


def _inproj(x, mod3, boff, g_norm, w_in_bf, gains, cos_t, sa_t, sb_t):
    b, t, d = x.shape
    tm = 1024
    nj = IN_COLS // IN_TN
    return pl.pallas_call(
        _inproj_kernel,
        grid=(b, t // tm, nj),
        in_specs=[
            pl.BlockSpec((1, tm, d), lambda bi, ti, j: (bi, ti, 0)),
            pl.BlockSpec((1, 1, d), lambda bi, ti, j: ((bi + boff) * 6 + 1, 0, 0)),
            pl.BlockSpec((1, 1, d), lambda bi, ti, j: ((bi + boff) * 6 + 0, 0, 0)),
            pl.BlockSpec((1, d), lambda bi, ti, j: (0, 0)),
            pl.BlockSpec((d, IN_TN), lambda bi, ti, j: (0, j)),
            pl.BlockSpec((4, HEAD_DIM), lambda bi, ti, j: (0, 0)),
            pl.BlockSpec((tm, HEAD_DIM), lambda bi, ti, j: (ti, 0)),
            pl.BlockSpec((tm, HEAD_DIM), lambda bi, ti, j: (ti, 0)),
            pl.BlockSpec((tm, HEAD_DIM), lambda bi, ti, j: (ti, 0)),
        ],
        out_specs=pl.BlockSpec((1, tm, IN_TN), lambda bi, ti, j: (bi, ti, j)),
        out_shape=jax.ShapeDtypeStruct((b, t, IN_COLS), BF16),
        scratch_shapes=[pltpu.VMEM((tm, d), BF16)],
        compiler_params=_cparams(("arbitrary", "arbitrary", "arbitrary")),
        name="inproj",
    )(x, mod3, mod3, g_norm, w_in_bf, gains, cos_t, sa_t, sb_t)


def _attn_a_kernel(q_ref, k_ref, v_ref, bias_ref, o_ref, qf, kf, vf, acc, mm, ll, *, t):
    zpad = jnp.zeros((A_PAD, HEAD_DIM), F32)
    kf[0:A_PAD, :] = zpad
    vf[0:A_PAD, :] = zpad
    kf[A_PAD + t:A_PAD + t + A_PAD, :] = zpad
    vf[A_PAD + t:A_PAD + t + A_PAD, :] = zpad
    kf[A_PAD:A_PAD + t, :] = k_ref[0].astype(F32)
    vf[A_PAD:A_PAD + t, :] = v_ref[0].astype(F32)
    qf[...] = q_ref[0].astype(F32)
    ones = jnp.ones((A_KB, HEAD_DIM), BF16)

    for bi, (_, dil) in enumerate(DIL_CONFIGS):
        sub_len = t // dil
        nmb = sub_len // A_QB
        shift = int(math.log2(nmb))

        def rows(start, size, dil=dil):
            return pl.ds(start, size) if dil == 1 else pl.ds(start, size, stride=dil)

        def body(idx, carry, bi=bi, dil=dil, sub_len=sub_len, nmb=nmb, shift=shift, rows=rows):
            rho = lax.shift_right_logical(idx, shift)
            mb = lax.bitwise_and(idx, nmb - 1)
            qstart = rho + mb * (A_QB * dil)
            kstart = A_PAD + qstart - A_RADIUS * dil
            q = qf[rows(qstart, A_QB), :].astype(BF16)
            k = kf[rows(kstart, A_KB), :].astype(BF16)
            v = vf[rows(kstart, A_KB), :].astype(BF16)
            s = lax.dot_general(q, k, (((1,), (1,)), ((), ())), preferred_element_type=F32)
            s = s + bias_ref[bi, 0]
            kidx = mb * A_QB - A_RADIUS + lax.broadcasted_iota(jnp.int32, (1, A_KB), 1)
            s = jnp.where((kidx >= 0) & (kidx < sub_len), s, NEG_INF)
            mblk = jnp.max(s, axis=-1, keepdims=True)
            v1 = jnp.concatenate([v, ones], axis=1)
            p = jnp.exp(s - mblk).astype(BF16)
            pv = jnp.dot(p, v1, preferred_element_type=F32)
            acc[bi, rows(qstart, A_QB), :] = pv[:, :HEAD_DIM]
            ll[bi, rows(qstart, A_QB), :] = pv[:, HEAD_DIM:]
            mm[bi, rows(qstart, A_QB), :] = jnp.broadcast_to(mblk, (A_QB, HEAD_DIM))
            return carry

        lax.fori_loop(0, dil * nmb, body, 0, unroll=A_UNROLL)

    m0, m1, m2 = mm[0], mm[1], mm[2]
    mtop = jnp.maximum(m0, jnp.maximum(m1, m2))
    w0, w1, w2 = jnp.exp(m0 - mtop), jnp.exp(m1 - mtop), jnp.exp(m2 - mtop)
    num = w0 * acc[0] + w1 * acc[1] + w2 * acc[2]
    den = w0 * ll[0] + w1 * ll[1] + w2 * ll[2]
    o_ref[0] = (num / den).astype(BF16)


def _attn_a(proj, bias_tiles):
    b, t, _ = proj.shape
    kern = functools.partial(_attn_a_kernel, t=t)
    return pl.pallas_call(
        kern,
        grid=(b, N_HEADS_A),
        in_specs=[
            pl.BlockSpec((1, t, HEAD_DIM), lambda bi, h: (bi, 0, COL_QA + h)),
            pl.BlockSpec((1, t, HEAD_DIM), lambda bi, h: (bi, 0, COL_KA + h)),
            pl.BlockSpec((1, t, HEAD_DIM), lambda bi, h: (bi, 0, COL_VA + h)),
            pl.BlockSpec((3, 1, A_QB, A_KB), lambda bi, h: (0, h, 0, 0)),
        ],
        out_specs=pl.BlockSpec((1, t, HEAD_DIM), lambda bi, h: (bi, 0, h)),
        out_shape=jax.ShapeDtypeStruct((b, t, WIDTH_A), BF16),
        scratch_shapes=[
            pltpu.VMEM((t, HEAD_DIM), F32),
            pltpu.VMEM((t + 2 * A_PAD, HEAD_DIM), F32),
            pltpu.VMEM((t + 2 * A_PAD, HEAD_DIM), F32),
            pltpu.VMEM((len(DIL_CONFIGS), t, HEAD_DIM), F32),
            pltpu.VMEM((len(DIL_CONFIGS), t, HEAD_DIM), F32),
            pltpu.VMEM((len(DIL_CONFIGS), t, HEAD_DIM), F32),
        ],
        compiler_params=_cparams(("arbitrary", "arbitrary")),
        name="attn_dilated",
    )(proj, proj, proj, bias_tiles)


def _t5_bucket(rel):
    nb = NUM_BUCKETS // 2
    max_exact = nb // 2
    sign_off = np.where(rel > 0, nb, 0)
    n = np.abs(rel)
    nf = np.maximum(n, 1).astype(np.float32)
    large = max_exact + (np.log(nf / np.float32(max_exact))
                         / np.float32(math.log(MAX_DISTANCE / max_exact))
                         * np.float32(nb - max_exact)).astype(np.int32)
    large = np.minimum(large, nb - 1)
    return (sign_off + np.where(n < max_exact, n, large)).astype(np.int32)


def _bias_tiles(rel_bias_table):
    qi = np.arange(A_QB, dtype=np.int32)[:, None]
    kj = np.arange(A_KB, dtype=np.int32)[None, :]
    rel = kj - A_RADIUS - qi
    in_band = jnp.asarray(np.abs(rel) <= A_RADIUS)
    tiles = []
    for _, dil in DIL_CONFIGS:
        onehot = jnp.asarray(_t5_bucket(rel * dil)[..., None]
                             == np.arange(NUM_BUCKETS, dtype=np.int32), F32)
        bias = jnp.einsum('qkn,nh->hqk', onehot, rel_bias_table.astype(F32),
                          precision=lax.Precision.HIGHEST)
        tiles.append(jnp.where(in_band[None], bias, NEG_INF))
    return jnp.stack(tiles, axis=0)


def _attn_b_kernel(q_ref, k_ref, v_ref, o_ref, v1_scr, *, tq):
    @pl.when(pl.program_id(2) == 0)
    def _():
        v1_scr[:, 0:HEAD_DIM] = v_ref[0]
        v1_scr[:, HEAD_DIM:2 * HEAD_DIM] = jnp.ones(v_ref.shape[1:], BF16)

    for c in range(tq // B_SUB):
        q = q_ref[0, c * B_SUB:(c + 1) * B_SUB, :]
        qs = jnp.concatenate([q[:, i * HEAD_DIM:(i + 1) * HEAD_DIM] for i in range(GQA_GROUP)],
                             axis=0)
        s = lax.dot_general(qs, k_ref[0], (((1,), (1,)), ((), ())), preferred_element_type=F32)
        m = jnp.max(s, axis=-1, keepdims=True)
        p = jnp.exp(s - m).astype(BF16)
        pv = jnp.dot(p, v1_scr[...], preferred_element_type=F32)
        o = pv[:, :HEAD_DIM] / pv[:, HEAD_DIM:]
        o_ref[0, c * B_SUB:(c + 1) * B_SUB, :] = jnp.concatenate(
            [o[i * B_SUB:(i + 1) * B_SUB] for i in range(GQA_GROUP)], axis=1).astype(BF16)


def _attn_b(proj):
    b, t, _ = proj.shape
    tq = B_SUB * max(1, B_SCORE_ELEMS // (GQA_GROUP * B_SUB * t))
    gw = GQA_GROUP * HEAD_DIM
    return pl.pallas_call(
        functools.partial(_attn_b_kernel, tq=tq),
        grid=(b, N_KV_B, t // tq),
        in_specs=[
            pl.BlockSpec((1, tq, gw), lambda bi, g, qi: (bi, qi, COL_QB // GQA_GROUP + g)),
            pl.BlockSpec((1, t, HEAD_DIM), lambda bi, g, qi: (bi, 0, COL_KB + g)),
            pl.BlockSpec((1, t, HEAD_DIM), lambda bi, g, qi: (bi, 0, COL_VB + g)),
        ],
        out_specs=pl.BlockSpec((1, tq, gw), lambda bi, g, qi: (bi, qi, g)),
        out_shape=jax.ShapeDtypeStruct((b, t, WIDTH_B), BF16),
        scratch_shapes=[pltpu.VMEM((t, 2 * HEAD_DIM), BF16)],
        compiler_params=_cparams(("arbitrary", "arbitrary", "arbitrary")),
        name="attn_gqa",
    )(proj, proj, proj)


def _outproj_kernel(oa_ref, ob_ref, x_ref, gt_ref, sc_ref, sh_ref, ga_ref, gb_ref, gn_ref,
                    w_ref, wr_ref, x1_ref, h2_ref, lg_ref):
    def wide_norm(o, g_ref_):
        o = o.astype(F32)
        ms = jnp.mean(o * o, axis=-1, keepdims=True)
        return (o * lax.rsqrt(ms + EPS) * g_ref_[...]).astype(BF16)

    tm, dd = x_ref.shape[1:]
    slab = dd // LANES
    sub = tm // OUT_CHAINS
    for c in range(OUT_CHAINS):
        rows = slice(c * sub, (c + 1) * sub)
        na = wide_norm(oa_ref[0, rows, :], ga_ref)
        nb = wide_norm(ob_ref[0, rows, :], gb_ref)
        mix = (jnp.dot(na, w_ref[0:WIDTH_A, :], preferred_element_type=F32)
               + jnp.dot(nb, w_ref[WIDTH_A:WIDTH_A + WIDTH_B, :], preferred_element_type=F32))
        x1 = x_ref[0, rows, :] + gt_ref[0] * mix
        x1_ref[0, rows, :] = x1
        ms = jnp.mean(x1 * x1, axis=-1, keepdims=True)
        h2 = (x1 * lax.rsqrt(ms + EPS) * gn_ref[...]) * (1.0 + sc_ref[0]) + sh_ref[0]
        lg_ref[0, rows, :] = jnp.dot(h2.astype(BF16), wr_ref[...], preferred_element_type=F32)
        for s in range(TOK_PITCH):
            val = h2[:, s * LANES:(s + 1) * LANES] if s < slab else jnp.zeros((sub, LANES), F32)
            h2_ref[pl.ds(c * sub * TOK_PITCH + s, sub, stride=TOK_PITCH), :] = val


def _outproj(oa, ob, x, mod3, boff, g_out_a, g_out_b, g_norm_ffn, w_out_bf, w_router_pad):
    b, t, d = x.shape
    tm = 512
    nr = w_router_pad.shape[1]
    assert d // LANES <= TOK_PITCH
    row = lambda k: (lambda bi, ti: ((bi + boff) * 6 + k, 0, 0))
    return pl.pallas_call(
        _outproj_kernel,
        grid=(b, t // tm),
        in_specs=[
            pl.BlockSpec((1, tm, WIDTH_A), lambda bi, ti: (bi, ti, 0)),
            pl.BlockSpec((1, tm, WIDTH_B), lambda bi, ti: (bi, ti, 0)),
            pl.BlockSpec((1, tm, d), lambda bi, ti: (bi, ti, 0)),
            pl.BlockSpec((1, 1, d), row(2)),
            pl.BlockSpec((1, 1, d), row(4)),
            pl.BlockSpec((1, 1, d), row(3)),
            pl.BlockSpec((1, WIDTH_A), lambda bi, ti: (0, 0)),
            pl.BlockSpec((1, WIDTH_B), lambda bi, ti: (0, 0)),
            pl.BlockSpec((1, d), lambda bi, ti: (0, 0)),
            pl.BlockSpec((WIDTH_A + WIDTH_B, d), lambda bi, ti: (0, 0)),
            pl.BlockSpec((d, nr), lambda bi, ti: (0, 0)),
        ],
        out_specs=[
            pl.BlockSpec((1, tm, d), lambda bi, ti: (bi, ti, 0)),
            pl.BlockSpec((tm * TOK_PITCH, LANES), lambda bi, ti: (bi * (t // tm) + ti, 0)),
            pl.BlockSpec((1, tm, nr), lambda bi, ti: (bi, ti, 0)),
        ],
        out_shape=[jax.ShapeDtypeStruct((b, t, d), F32),
                   jax.ShapeDtypeStruct((b * t * TOK_PITCH, LANES), F32),
                   jax.ShapeDtypeStruct((b, t, nr), F32)],
        compiler_params=_cparams(("arbitrary", "arbitrary")),
        name="outproj",
    )(oa, ob, x, mod3, mod3, mod3, g_out_a, g_out_b, g_norm_ffn, w_out_bf, w_router_pad)


FFN_OUT_CHUNK = 512


def _ffn_kernel(idx_ref, idx_next_ref, h_ref, wg_ref, wu_ref, wd_ref, g_ref, o_ref,
                xslab, xb, hm, wdb, sem, *, tm, slab):
    f = pl.program_id(2)
    k = pl.program_id(0) * pl.num_programs(1) + pl.program_id(1)
    nk = pl.num_programs(0) * pl.num_programs(1)

    def issue_gather(ids_ref):
        def body(r, carry):
            pltpu.make_async_copy(h_ref.at[pl.ds(ids_ref[0, 0, r] * TOK_PITCH, slab), :],
                                  xslab.at[pl.ds(r * TOK_PITCH, slab), :], sem.at[0]).start()
            return carry
        lax.fori_loop(0, tm, body, 0, unroll=8)

    @pl.when(f == 0)
    def _():
        @pl.when(k == 0)
        def _():
            issue_gather(idx_ref)

        pltpu.make_async_copy(h_ref.at[pl.ds(0, tm * slab), :], xslab.at[pl.ds(0, tm * slab), :],
                              sem.at[0]).wait()
        for s in range(slab):
            xb[:, s * LANES:(s + 1) * LANES] = xslab[pl.ds(s, tm, stride=TOK_PITCH), :].astype(BF16)

        @pl.when(k + 1 < nk)
        def _():
            issue_gather(idx_next_ref)

    @pl.when(pl.program_id(1) == 0)
    def _():
        wdb[f] = wd_ref[0].astype(BF16)

    x = xb[...]
    a = jnp.dot(x, wg_ref[0].astype(BF16), preferred_element_type=F32)
    u = jnp.dot(x, wu_ref[0].astype(BF16), preferred_element_type=F32)
    hm[f] = (a * (1.0 / (1.0 + jnp.exp(-a))) * u).astype(BF16)

    @pl.when(f == pl.num_programs(2) - 1)
    def _():
        nf = hm.shape[0]
        for c in range(o_ref.shape[2] // FFN_OUT_CHUNK):
            cols = slice(c * FFN_OUT_CHUNK, (c + 1) * FFN_OUT_CHUNK)
            y = jnp.dot(hm[0], wdb[0, :, cols], preferred_element_type=F32)
            for j in range(1, nf):
                y += jnp.dot(hm[j], wdb[j, :, cols], preferred_element_type=F32)
            o_ref[0, :, cols] = (y * g_ref[0]).astype(o_ref.dtype)


def _ffn(idx, h2slab, wg, wu, wd, g):
    e, cap = idx.shape
    d = wg.shape[1]
    slab = d // LANES
    fdim = wg.shape[2]
    tm = min(cap, 1024)
    tf = 256
    nf = fdim // tf
    mt = cap // tm
    nk = e * mt
    idx3 = idx.reshape(nk, 1, tm)
    return pl.pallas_call(
        functools.partial(_ffn_kernel, tm=tm, slab=slab),
        grid=(e, mt, fdim // tf),
        in_specs=[
            pl.BlockSpec((1, 1, tm), lambda ei, mi, fi: (ei * mt + mi, 0, 0),
                         memory_space=pltpu.SMEM),
            pl.BlockSpec((1, 1, tm), lambda ei, mi, fi: (jnp.minimum(ei * mt + mi + 1, nk - 1), 0, 0),
                         memory_space=pltpu.SMEM),
            pl.BlockSpec(memory_space=pl.ANY),
            pl.BlockSpec((1, d, tf), lambda ei, mi, fi: (ei, 0, fi)),
            pl.BlockSpec((1, d, tf), lambda ei, mi, fi: (ei, 0, fi)),
            pl.BlockSpec((1, tf, d), lambda ei, mi, fi: (ei, jnp.where(mi == 0, fi, nf - 1), 0)),
            pl.BlockSpec((1, tm, 1), lambda ei, mi, fi: (ei, mi, 0)),
        ],
        out_specs=pl.BlockSpec((1, tm, d), lambda ei, mi, fi: (ei, mi, 0)),
        out_shape=jax.ShapeDtypeStruct((e, cap, d), BF16),
        scratch_shapes=[pltpu.VMEM((tm * TOK_PITCH, LANES), F32),
                        pltpu.VMEM((tm, d), BF16),
                        pltpu.VMEM((nf, tm, tf), BF16),
                        pltpu.VMEM((nf, tf, d), BF16),
                        pltpu.SemaphoreType.DMA((1,))],
        compiler_params=_cparams(("arbitrary", "arbitrary", "arbitrary")),
        name="expert_ffn",
    )(idx3, idx3, h2slab, wg, wu, wd, g.reshape(e, cap, 1))


def _prefix_counts(mask_f32, upper, lower):
    within = jnp.dot(mask_f32.astype(BF16), upper, preferred_element_type=F32)
    tot = jnp.broadcast_to(within[:, LANES - 1:LANES], within.shape)
    offs = jnp.dot(lower, tot.astype(BF16), preferred_element_type=F32)
    return within, offs


def _route_kernel(lt_ref, idx_ref, g_ref, pos_ref, offs_ref, aff_scr, thr_scr, *, cap):
    e = pl.program_id(0)
    nchunk = lt_ref.shape[1]
    capf = jnp.float32(cap)

    @pl.when(e == 0)
    def _():
        l = lt_ref[...]
        ex = jnp.exp(l - jnp.max(l, axis=0, keepdims=True))
        aff = ex / jnp.sum(ex, axis=0, keepdims=True)
        aff_scr[...] = aff

        def bit_step(i, thr_bits):
            cand = thr_bits | lax.shift_left(jnp.int32(1), 30 - i)
            cnt = jnp.sum((aff >= lax.bitcast_convert_type(cand, F32)).astype(F32),
                          axis=(1, 2), keepdims=True)
            return jnp.where(cnt >= capf, cand, thr_bits)

        thr_bits = lax.fori_loop(0, 31, bit_step, jnp.zeros((lt_ref.shape[0], 1, 1), jnp.int32))
        thr_scr[...] = jnp.broadcast_to(lax.bitcast_convert_type(thr_bits, F32), thr_scr.shape)

    a = aff_scr[e]
    thr = thr_scr[e][0:1, 0:1]

    def count(m):
        return jnp.sum(m.astype(F32), axis=(0, 1), keepdims=True)

    ri = lax.broadcasted_iota(jnp.int32, (LANES, LANES), 0)
    ci = lax.broadcasted_iota(jnp.int32, (LANES, LANES), 1)
    upper = (ri <= ci).astype(BF16)
    rc = lax.broadcasted_iota(jnp.int32, (nchunk, nchunk), 0)
    cc = lax.broadcasted_iota(jnp.int32, (nchunk, nchunk), 1)
    lower = (cc < rc).astype(BF16)

    gt = a > thr
    eq = a == thr
    need = capf - count(gt)
    eq_within, eq_offs = _prefix_counts(eq.astype(F32), upper, lower)
    sel = gt | (eq & ((eq_within + eq_offs) <= need))
    within, offs = _prefix_counts(sel.astype(F32), upper, lower)
    pos_ref[0] = jnp.where(sel, within + offs - 1.0, -1.0).astype(jnp.int32)
    offs_ref[0] = offs.astype(jnp.int32)

    offs_col = offs[:, 0:1]
    ends_col = offs_col + within[:, LANES - 1:LANES]
    slot = lax.broadcasted_iota(jnp.int32, (1, cap), 1).astype(F32)
    cstar = jnp.sum((ends_col <= slot).astype(F32), axis=0, keepdims=True)
    chunk_oh = lax.broadcasted_iota(jnp.int32, (nchunk, cap), 0).astype(F32) == cstar
    rank = slot - jnp.sum(jnp.where(chunk_oh, offs_col, 0.0), axis=0, keepdims=True)
    oh = chunk_oh.astype(BF16)
    tdot = functools.partial(lax.dot_general, dimension_numbers=(((0,), (0,)), ((), ())),
                             preferred_element_type=F32)
    wsel = tdot(within.astype(BF16), oh)
    lstar = jnp.sum((wsel <= rank).astype(F32), axis=0, keepdims=True)
    idx_ref[0] = (cstar * LANES + lstar).astype(jnp.int32)

    a1 = a.astype(BF16)
    r1 = a - a1.astype(F32)
    a2 = r1.astype(BF16)
    a3 = (r1 - a2.astype(F32)).astype(BF16)
    asel = (tdot(a1, oh) + tdot(a2, oh)) + tdot(a3, oh)
    lane_oh = lax.broadcasted_iota(jnp.int32, (LANES, cap), 0).astype(F32) == lstar
    g_ref[0] = jnp.sum(jnp.where(lane_oh, asel, 0.0), axis=0, keepdims=True)


def _route(lt, cap):
    e, nchunk, _ = lt.shape
    return pl.pallas_call(
        functools.partial(_route_kernel, cap=cap),
        grid=(e,),
        in_specs=[pl.BlockSpec((e, nchunk, LANES), lambda ei: (0, 0, 0))],
        out_specs=[pl.BlockSpec((1, 1, cap), lambda ei: (ei, 0, 0)),
                   pl.BlockSpec((1, 1, cap), lambda ei: (ei, 0, 0)),
                   pl.BlockSpec((1, nchunk, LANES), lambda ei: (ei, 0, 0)),
                   pl.BlockSpec((1, nchunk, LANES), lambda ei: (ei, 0, 0))],
        out_shape=[jax.ShapeDtypeStruct((e, 1, cap), jnp.int32),
                   jax.ShapeDtypeStruct((e, 1, cap), F32),
                   jax.ShapeDtypeStruct((e, nchunk, LANES), jnp.int32),
                   jax.ShapeDtypeStruct((e, nchunk, LANES), jnp.int32)],
        scratch_shapes=[pltpu.VMEM((e, nchunk, LANES), F32),
                        pltpu.VMEM((e, SUBLANES, LANES), F32)],
        compiler_params=_cparams(("arbitrary",)),
        name="route",
    )(lt)


COMB_TOK = 256
COMB_WIN = 64
COMB_ALIGN = 16


def _combine_kernel(tab_ref, tab_next_ref, startv_ref, pos_ref, x1_ref, gate_ref, ye_ref, o_ref,
                    ybuf, ybuf_x, sems, sem_x, *, cap):
    k = pl.program_id(0)
    nk = pl.num_programs(0)
    slot = lax.rem(k, 2)
    ne = ye_ref.shape[0]
    last_start = cap - COMB_WIN

    def window_copy(e, logical_start, dst, sem):
        row0 = pl.multiple_of(jnp.minimum(logical_start, last_start), COMB_ALIGN)
        return pltpu.make_async_copy(ye_ref.at[e, pl.ds(row0, COMB_WIN), :],
                                     dst.at[pl.ds(e * COMB_WIN, COMB_WIN), :], sem)

    def issue(tab, sl):
        for e in range(ne):
            window_copy(e, tab[0, 0, e], ybuf.at[sl], sems.at[sl]).start()

    @pl.when(k == 0)
    def _():
        issue(tab_ref, slot)

    @pl.when(k + 1 < nk)
    def _():
        issue(tab_next_ref, 1 - slot)

    pltpu.make_async_copy(ybuf.at[slot], ybuf.at[slot], sems.at[slot]).wait()

    pos = pos_ref[...]
    start0 = startv_ref[0]
    width = ne * COMB_WIN
    col = lax.broadcasted_iota(jnp.int32, (ne, width), 1)
    expand = (lax.shift_right_logical(col, int(math.log2(COMB_WIN)))
              == lax.broadcasted_iota(jnp.int32, (ne, width), 0)).astype(BF16)
    lane_in_win = lax.bitwise_and(lax.broadcasted_iota(jnp.int32, (1, width), 1),
                                  COMB_WIN - 1).astype(F32)

    def place(p):
        logical = start0 + p * COMB_WIN
        rel = pos - logical
        inside = (pos >= 0) & (rel >= 0) & (rel < COMB_WIN)
        row = jnp.where(inside, pos - jnp.minimum(logical, last_start), -1)
        spread = jnp.dot(row.astype(F32).astype(BF16), expand, preferred_element_type=F32)
        return (spread == lane_in_win).astype(BF16)

    gate = gate_ref[0]
    o_ref[...] = x1_ref[...] + gate * jnp.dot(place(0), ybuf[slot], preferred_element_type=F32)

    def extra_pass(p, carry):
        for e in range(ne):
            window_copy(e, tab_ref[0, 0, e] + p * COMB_WIN, ybuf_x, sem_x.at[0]).start()
        pltpu.make_async_copy(ybuf_x, ybuf_x, sem_x.at[0]).wait()
        o_ref[...] += gate * jnp.dot(place(p), ybuf_x[...], preferred_element_type=F32)
        return carry

    lax.fori_loop(1, tab_ref[0, 0, ne], extra_pass, 0)


def _combine(pos, offs, ye, x1, mod3, *, t, boff):
    ne, cap, d = ye.shape
    n = x1.shape[0]
    ntiles = n // COMB_TOK
    chunks_per_tile = COMB_TOK // LANES
    before = offs[:, ::chunks_per_tile, 0].T
    after = jnp.concatenate([before[1:], jnp.full((1, ne), cap, jnp.int32)], axis=0)
    starts = before - before % COMB_ALIGN
    passes = jnp.maximum(1, (jnp.max(after - starts, axis=1) + COMB_WIN - 1) // COMB_WIN)
    table = jnp.concatenate([starts, passes[:, None]], axis=1).astype(jnp.int32)
    table = table.reshape(ntiles, 1, ne + 1)
    startv = starts.astype(jnp.int32).reshape(ntiles, 1, ne)
    pos_tm = pos.reshape(ne, n).T
    tiles_per_seq = t // COMB_TOK
    return pl.pallas_call(
        functools.partial(_combine_kernel, cap=cap),
        grid=(ntiles,),
        in_specs=[
            pl.BlockSpec((1, 1, ne + 1), lambda k: (k, 0, 0), memory_space=pltpu.SMEM),
            pl.BlockSpec((1, 1, ne + 1), lambda k: (jnp.minimum(k + 1, ntiles - 1), 0, 0),
                         memory_space=pltpu.SMEM),
            pl.BlockSpec((1, 1, ne), lambda k: (k, 0, 0)),
            pl.BlockSpec((COMB_TOK, ne), lambda k: (k, 0)),
            pl.BlockSpec((COMB_TOK, d), lambda k: (k, 0)),
            pl.BlockSpec((1, 1, d), lambda k: ((k // tiles_per_seq + boff) * 6 + 5, 0, 0)),
            pl.BlockSpec(memory_space=pl.ANY),
        ],
        out_specs=pl.BlockSpec((COMB_TOK, d), lambda k: (k, 0)),
        out_shape=jax.ShapeDtypeStruct(x1.shape, x1.dtype),
        scratch_shapes=[pltpu.VMEM((2, ne * COMB_WIN, d), BF16),
                        pltpu.VMEM((ne * COMB_WIN, d), BF16),
                        pltpu.SemaphoreType.DMA((2,)),
                        pltpu.SemaphoreType.DMA((1,))],
        compiler_params=_cparams(("arbitrary",)),
        name="combine",
    )(table, table, startv, pos_tm, x1, mod3, ye)


def _rope_tables(t):
    half = HEAD_DIM // 2
    quarter = half // 2
    freqs = ROPE_THETA ** (-(jnp.arange(quarter, dtype=F32) / quarter))
    rows = t // GRID_W
    row_ids = jnp.repeat(jnp.arange(rows, dtype=jnp.int32), GRID_W).astype(F32)
    col_ids = jnp.tile(jnp.arange(GRID_W, dtype=jnp.int32), rows).astype(F32)
    ang_r = row_ids[:, None] * freqs[None, :]
    ang_c = col_ids[:, None] * freqs[None, :]
    z = jnp.zeros_like(ang_r)
    cos_t = jnp.concatenate([jnp.cos(ang_r)] * 2 + [jnp.cos(ang_c)] * 2, axis=1)
    sa_t = jnp.concatenate([-jnp.sin(ang_r), z, -jnp.sin(ang_c), z], axis=1)
    sb_t = jnp.concatenate([z, jnp.sin(ang_r), z, jnp.sin(ang_c)], axis=1)
    return cos_t, sa_t, sb_t


def kernel(x_prompt, x_sample, c_prompt, c_sample, rel_bias_table, w_ada, b_ada, g_norm_mix, g_norm_ffn, w_in, g_q_a, g_k_a, g_q_b, g_k_b, g_out_a, g_out_b, w_out, w_router, w_gate, w_up, w_down):
    d = D_MODEL
    nbp, nbs = c_prompt.shape[0], c_sample.shape[0]
    c_all = jnp.concatenate([c_prompt, c_sample, jnp.zeros((16 - nbp - nbs, d), F32)], axis=0)
    mod = _ada(c_all, w_ada[0], b_ada[0])
    mod3 = mod.reshape(16 * 6, 1, d)

    scale = HEAD_DIM ** -0.5
    gains = jnp.stack([g_q_a[0] * scale, g_k_a[0], g_q_b[0] * scale, g_k_b[0]], axis=0)
    w_in_bf = w_in[0].astype(BF16)
    w_out_bf = w_out[0].astype(BF16)
    w_router_pad = jnp.pad(w_router[0], ((0, 0), (0, HEAD_DIM - N_EXPERTS))).astype(BF16)
    wg, wu, wd = w_gate[0], w_up[0], w_down[0]
    bias_tiles = _bias_tiles(rel_bias_table)
    gn1 = g_norm_mix[0].reshape(1, d)
    gn2 = g_norm_ffn[0].reshape(1, d)
    ga = g_out_a[0].reshape(1, WIDTH_A)
    gb = g_out_b[0].reshape(1, WIDTH_B)

    def run(x, boff):
        b, t, _ = x.shape
        cos_t, sa_t, sb_t = _rope_tables(t)
        proj = _inproj(x, mod3, boff, gn1, w_in_bf, gains, cos_t, sa_t, sb_t)
        oa = _attn_a(proj, bias_tiles)
        ob = _attn_b(proj)
        x1, h2, logits = _outproj(oa, ob, x, mod3, boff, ga, gb, gn2, w_out_bf, w_router_pad)
        n = b * t
        cap = EC_CAPACITY_FACTOR * n // N_EXPERTS
        lt = logits.reshape(n, -1)[:, :N_EXPERTS].T.reshape(N_EXPERTS, n // LANES, LANES)
        idx, g, pos, offs = _route(lt, cap)
        ye = _ffn(idx.reshape(N_EXPERTS, cap), h2, wg, wu, wd, g.reshape(N_EXPERTS, cap))
        out = _combine(pos, offs, ye, x1.reshape(n, d), mod3, t=t, boff=boff)
        return out.reshape(b, t, d)

    return (run(x_prompt, 0), run(x_sample, nbp))
```

```python
import functools
import math

import jax
import jax.numpy as jnp
import numpy as np
from jax import lax
from jax.experimental import pallas as pl
from jax.experimental.pallas import tpu as pltpu

F32 = jnp.float32
BF16 = jnp.bfloat16

D_MODEL = 2048
HEAD_DIM = 128
N_HEADS_A = 8
N_HEADS_B = 8
N_KV_B = 2
GQA_GROUP = N_HEADS_B // N_KV_B
WIDTH_A = N_HEADS_A * HEAD_DIM
WIDTH_B = N_HEADS_B * HEAD_DIM
KV_WIDTH_B = N_KV_B * HEAD_DIM
IN_COLS = 3 * WIDTH_A + WIDTH_B + 2 * KV_WIDTH_B
DIL_CONFIGS = ((128, 1), (512, 4), (2048, 16))
NUM_BUCKETS = 32
MAX_DISTANCE = 1024
GRID_W = 64
ROPE_THETA = 10000.0
N_EXPERTS = 16
EC_CAPACITY_FACTOR = 2
D_EXPERT = 2048
EPS = 1e-6
NEG_INF = -1e30

VMEM_LIMIT_V7X = 56 * 1024 * 1024
LANES = 128
SUBLANES = 8
TOK_PITCH = 20

COL_QA, COL_KA, COL_VA = 0, 8, 16
COL_QB, COL_KB, COL_VB = 24, 32, 34

IN_TN = 512
IN_CHAINS = 4
OUT_CHAINS = 2
A_QB = 128
A_KB = 256
A_RADIUS = 64
A_PAD = A_RADIUS * 16
A_UNROLL = 16
B_SUB = 128
B_SCORE_ELEMS = 8 * 1024 * 1024


def _cparams(sem):
    return pltpu.CompilerParams(dimension_semantics=sem, vmem_limit_bytes=VMEM_LIMIT_V7X)


def _ada_kernel(c_ref, w_ref, b_ref, o_ref):
    c = c_ref[...]
    s = c * (1.0 / (1.0 + jnp.exp(-c)))
    o_ref[...] = jnp.dot(s.astype(BF16), w_ref[...].astype(BF16),
                         preferred_element_type=F32) + b_ref[...]


def _ada(c_all, w_ada, b_ada):
    rows, d = c_all.shape
    n = w_ada.shape[1]
    tn = 1024
    return pl.pallas_call(
        _ada_kernel,
        grid=(n // tn,),
        in_specs=[pl.BlockSpec((rows, d), lambda j: (0, 0)),
                  pl.BlockSpec((d, tn), lambda j: (0, j)),
                  pl.BlockSpec((1, tn), lambda j: (0, j))],
        out_specs=pl.BlockSpec((rows, tn), lambda j: (0, j)),
        out_shape=jax.ShapeDtypeStruct((rows, n), F32),
        compiler_params=_cparams(("arbitrary",)),
        name="ada_mod",
    )(c_all, w_ada, b_ada.reshape(1, n))


def _head_norm(a, g):
    ms = jnp.mean(a * a, axis=-1, keepdims=True)
    return a * lax.rsqrt(ms + EPS) * g


def _inproj_kernel(x_ref, sc_ref, sh_ref, gn_ref, w_ref, gains_ref, cos_ref, sa_ref, sb_ref,
                   o_ref, h_scr):
    j = pl.program_id(2)

    @pl.when(j == 0)
    def _():
        x = x_ref[0]
        ms = jnp.mean(x * x, axis=-1, keepdims=True)
        y = x * lax.rsqrt(ms + EPS) * gn_ref[...]
        h_scr[...] = (y * (1.0 + sc_ref[0]) + sh_ref[0]).astype(BF16)

    def rope(a, rows):
        return (a * cos_ref[rows, :] + pltpu.roll(a, 96, 1) * sa_ref[rows, :]
                + pltpu.roll(a, 32, 1) * sb_ref[rows, :])

    def plain(a, rows):
        return a

    def norm(g_row):
        return lambda a, rows: _head_norm(a, gains_ref[g_row:g_row + 1, :])

    def norm_rope(g_row):
        return lambda a, rows: rope(_head_norm(a, gains_ref[g_row:g_row + 1, :]), rows)

    def tile(head_fns):
        sub = h_scr.shape[0] // IN_CHAINS
        for c in range(IN_CHAINS):
            rows = slice(c * sub, (c + 1) * sub)
            acc = jnp.dot(h_scr[rows, :], w_ref[...], preferred_element_type=F32)
            for hh, fn in enumerate(head_fns):
                sl = slice(hh * HEAD_DIM, (hh + 1) * HEAD_DIM)
                o_ref[0, rows, sl] = fn(acc[:, sl], rows).astype(BF16)

    @pl.when(j < 2)
    def _():
        tile([norm(0)] * 4)

    @pl.when((j >= 2) & (j < 4))
    def _():
        tile([norm(1)] * 4)

    @pl.when((j >= 4) & (j < 6))
    def _():
        tile([plain] * 4)

    @pl.when((j >= 6) & (j < 8))
    def _():
        tile([norm_rope(2)] * 4)

    @pl.when(j == 8)
    def _():
        tile([norm_rope(3)] * 2 + [plain] * 2)


def _inproj(x, mod3, boff, g_norm, w_in_bf, gains, cos_t, sa_t, sb_t):
    b, t, d = x.shape
    tm = 1024
    nj = IN_COLS // IN_TN
    return pl.pallas_call(
        _inproj_kernel,
        grid=(b, t // tm, nj),
        in_specs=[
            pl.BlockSpec((1, tm, d), lambda bi, ti, j: (bi, ti, 0)),
            pl.BlockSpec((1, 1, d), lambda bi, ti, j: ((bi + boff) * 6 + 1, 0, 0)),
            pl.BlockSpec((1, 1, d), lambda bi, ti, j: ((bi + boff) * 6 + 0, 0, 0)),
            pl.BlockSpec((1, d), lambda bi, ti, j: (0, 0)),
            pl.BlockSpec((d, IN_TN), lambda bi, ti, j: (0, j)),
            pl.BlockSpec((4, HEAD_DIM), lambda bi, ti, j: (0, 0)),
            pl.BlockSpec((tm, HEAD_DIM), lambda bi, ti, j: (ti, 0)),
            pl.BlockSpec((tm, HEAD_DIM), lambda bi, ti, j: (ti, 0)),
            pl.BlockSpec((tm, HEAD_DIM), lambda bi, ti, j: (ti, 0)),
        ],
        out_specs=pl.BlockSpec((1, tm, IN_TN), lambda bi, ti, j: (bi, ti, j)),
        out_shape=jax.ShapeDtypeStruct((b, t, IN_COLS), BF16),
        scratch_shapes=[pltpu.VMEM((tm, d), BF16)],
        compiler_params=_cparams(("arbitrary", "arbitrary", "arbitrary")),
        name="inproj",
    )(x, mod3, mod3, g_norm, w_in_bf, gains, cos_t, sa_t, sb_t)


def _attn_a_kernel(q_ref, k_ref, v_ref, bias_ref, o_ref, qf, kf, vf, acc, mm, ll, *, t):
    zpad = jnp.zeros((A_PAD, HEAD_DIM), F32)
    kf[0:A_PAD, :] = zpad
    vf[0:A_PAD, :] = zpad
    kf[A_PAD + t:A_PAD + t + A_PAD, :] = zpad
    vf[A_PAD + t:A_PAD + t + A_PAD, :] = zpad
    kf[A_PAD:A_PAD + t, :] = k_ref[0].astype(F32)
    vf[A_PAD:A_PAD + t, :] = v_ref[0].astype(F32)
    qf[...] = q_ref[0].astype(F32)
    ones = jnp.ones((A_KB, HEAD_DIM), BF16)

    for bi, (_, dil) in enumerate(DIL_CONFIGS):
        sub_len = t // dil
        nmb = sub_len // A_QB
        shift = int(math.log2(nmb))

        def rows(start, size, dil=dil):
            return pl.ds(start, size) if dil == 1 else pl.ds(start, size, stride=dil)

        def body(idx, carry, bi=bi, dil=dil, sub_len=sub_len, nmb=nmb, shift=shift, rows=rows):
            rho = lax.shift_right_logical(idx, shift)
            mb = lax.bitwise_and(idx, nmb - 1)
            qstart = rho + mb * (A_QB * dil)
            kstart = A_PAD + qstart - A_RADIUS * dil
            q = qf[rows(qstart, A_QB), :].astype(BF16)
            k = kf[rows(kstart, A_KB), :].astype(BF16)
            v = vf[rows(kstart, A_KB), :].astype(BF16)
            s = lax.dot_general(q, k, (((1,), (1,)), ((), ())), preferred_element_type=F32)
            s = s + bias_ref[bi, 0]
            kidx = mb * A_QB - A_RADIUS + lax.broadcasted_iota(jnp.int32, (1, A_KB), 1)
            s = jnp.where((kidx >= 0) & (kidx < sub_len), s, NEG_INF)
            mblk = jnp.max(s, axis=-1, keepdims=True)
            v1 = jnp.concatenate([v, ones], axis=1)
            p = jnp.exp(s - mblk).astype(BF16)
            pv = jnp.dot(p, v1, preferred_element_type=F32)
            acc[bi, rows(qstart, A_QB), :] = pv[:, :HEAD_DIM]
            ll[bi, rows(qstart, A_QB), :] = pv[:, HEAD_DIM:]
            mm[bi, rows(qstart, A_QB), :] = jnp.broadcast_to(mblk, (A_QB, HEAD_DIM))
            return carry

        lax.fori_loop(0, dil * nmb, body, 0, unroll=A_UNROLL)

    m0, m1, m2 = mm[0], mm[1], mm[2]
    mtop = jnp.maximum(m0, jnp.maximum(m1, m2))
    w0, w1, w2 = jnp.exp(m0 - mtop), jnp.exp(m1 - mtop), jnp.exp(m2 - mtop)
    num = w0 * acc[0] + w1 * acc[1] + w2 * acc[2]
    den = w0 * ll[0] + w1 * ll[1] + w2 * ll[2]
    o_ref[0] = (num / den).astype(BF16)


def _attn_a(proj, bias_tiles):
    b, t, _ = proj.shape
    kern = functools.partial(_attn_a_kernel, t=t)
    return pl.pallas_call(
        kern,
        grid=(b, N_HEADS_A),
        in_specs=[
            pl.BlockSpec((1, t, HEAD_DIM), lambda bi, h: (bi, 0, COL_QA + h)),
            pl.BlockSpec((1, t, HEAD_DIM), lambda bi, h: (bi, 0, COL_KA + h)),
            pl.BlockSpec((1, t, HEAD_DIM), lambda bi, h: (bi, 0, COL_VA + h)),
            pl.BlockSpec((3, 1, A_QB, A_KB), lambda bi, h: (0, h, 0, 0)),
        ],
        out_specs=pl.BlockSpec((1, t, HEAD_DIM), lambda bi, h: (bi, 0, h)),
        out_shape=jax.ShapeDtypeStruct((b, t, WIDTH_A), BF16),
        scratch_shapes=[
            pltpu.VMEM((t, HEAD_DIM), F32),
            pltpu.VMEM((t + 2 * A_PAD, HEAD_DIM), F32),
            pltpu.VMEM((t + 2 * A_PAD, HEAD_DIM), F32),
            pltpu.VMEM((len(DIL_CONFIGS), t, HEAD_DIM), F32),
            pltpu.VMEM((len(DIL_CONFIGS), t, HEAD_DIM), F32),
            pltpu.VMEM((len(DIL_CONFIGS), t, HEAD_DIM), F32),
        ],
        compiler_params=_cparams(("arbitrary", "arbitrary")),
        name="attn_dilated",
    )(proj, proj, proj, bias_tiles)


def _t5_bucket(rel):
    nb = NUM_BUCKETS // 2
    max_exact = nb // 2
    sign_off = np.where(rel > 0, nb, 0)
    n = np.abs(rel)
    nf = np.maximum(n, 1).astype(np.float32)
    large = max_exact + (np.log(nf / np.float32(max_exact))
                         / np.float32(math.log(MAX_DISTANCE / max_exact))
                         * np.float32(nb - max_exact)).astype(np.int32)
    large = np.minimum(large, nb - 1)
    return (sign_off + np.where(n < max_exact, n, large)).astype(np.int32)


def _bias_tiles(rel_bias_table):
    qi = np.arange(A_QB, dtype=np.int32)[:, None]
    kj = np.arange(A_KB, dtype=np.int32)[None, :]
    rel = kj - A_RADIUS - qi
    in_band = jnp.asarray(np.abs(rel) <= A_RADIUS)
    tiles = []
    for _, dil in DIL_CONFIGS:
        onehot = jnp.asarray(_t5_bucket(rel * dil)[..., None]
                             == np.arange(NUM_BUCKETS, dtype=np.int32), F32)
        bias = jnp.einsum('qkn,nh->hqk', onehot, rel_bias_table.astype(F32),
                          precision=lax.Precision.HIGHEST)
        tiles.append(jnp.where(in_band[None], bias, NEG_INF))
    return jnp.stack(tiles, axis=0)


def _attn_b_kernel(q_ref, k_ref, v_ref, o_ref, v1_scr, *, tq):
    @pl.when(pl.program_id(2) == 0)
    def _():
        v1_scr[:, 0:HEAD_DIM] = v_ref[0]
        v1_scr[:, HEAD_DIM:2 * HEAD_DIM] = jnp.ones(v_ref.shape[1:], BF16)

    for c in range(tq // B_SUB):
        q = q_ref[0, c * B_SUB:(c + 1) * B_SUB, :]
        qs = jnp.concatenate([q[:, i * HEAD_DIM:(i + 1) * HEAD_DIM] for i in range(GQA_GROUP)],
                             axis=0)
        s = lax.dot_general(qs, k_ref[0], (((1,), (1,)), ((), ())), preferred_element_type=F32)
        m = jnp.max(s, axis=-1, keepdims=True)
        p = jnp.exp(s - m).astype(BF16)
        pv = jnp.dot(p, v1_scr[...], preferred_element_type=F32)
        o = pv[:, :HEAD_DIM] / pv[:, HEAD_DIM:]
        o_ref[0, c * B_SUB:(c + 1) * B_SUB, :] = jnp.concatenate(
            [o[i * B_SUB:(i + 1) * B_SUB] for i in range(GQA_GROUP)], axis=1).astype(BF16)


def _attn_b(proj):
    b, t, _ = proj.shape
    tq = B_SUB * max(1, B_SCORE_ELEMS // (GQA_GROUP * B_SUB * t))
    gw = GQA_GROUP * HEAD_DIM
    return pl.pallas_call(
        functools.partial(_attn_b_kernel, tq=tq),
        grid=(b, N_KV_B, t // tq),
        in_specs=[
            pl.BlockSpec((1, tq, gw), lambda bi, g, qi: (bi, qi, COL_QB // GQA_GROUP + g)),
            pl.BlockSpec((1, t, HEAD_DIM), lambda bi, g, qi: (bi, 0, COL_KB + g)),
            pl.BlockSpec((1, t, HEAD_DIM), lambda bi, g, qi: (bi, 0, COL_VB + g)),
        ],
        out_specs=pl.BlockSpec((1, tq, gw), lambda bi, g, qi: (bi, qi, g)),
        out_shape=jax.ShapeDtypeStruct((b, t, WIDTH_B), BF16),
        scratch_shapes=[pltpu.VMEM((t, 2 * HEAD_DIM), BF16)],
        compiler_params=_cparams(("arbitrary", "arbitrary", "arbitrary")),
        name="attn_gqa",
    )(proj, proj, proj)


def _outproj_kernel(oa_ref, ob_ref, x_ref, gt_ref, sc_ref, sh_ref, ga_ref, gb_ref, gn_ref,
                    w_ref, wr_ref, x1_ref, h2_ref, lg_ref):
    def wide_norm(o, g_ref_):
        o = o.astype(F32)
        ms = jnp.mean(o * o, axis=-1, keepdims=True)
        return (o * lax.rsqrt(ms + EPS) * g_ref_[...]).astype(BF16)

    tm, dd = x_ref.shape[1:]
    slab = dd // LANES
    sub = tm // OUT_CHAINS
    for c in range(OUT_CHAINS):
        rows = slice(c * sub, (c + 1) * sub)
        na = wide_norm(oa_ref[0, rows, :], ga_ref)
        nb = wide_norm(ob_ref[0, rows, :], gb_ref)
        mix = (jnp.dot(na, w_ref[0:WIDTH_A, :], preferred_element_type=F32)
               + jnp.dot(nb, w_ref[WIDTH_A:WIDTH_A + WIDTH_B, :], preferred_element_type=F32))
        x1 = x_ref[0, rows, :] + gt_ref[0] * mix
        x1_ref[0, rows, :] = x1
        ms = jnp.mean(x1 * x1, axis=-1, keepdims=True)
        h2 = (x1 * lax.rsqrt(ms + EPS) * gn_ref[...]) * (1.0 + sc_ref[0]) + sh_ref[0]
        lg_ref[0, rows, :] = jnp.dot(h2.astype(BF16), wr_ref[...], preferred_element_type=F32)
        for s in range(TOK_PITCH):
            val = h2[:, s * LANES:(s + 1) * LANES] if s < slab else jnp.zeros((sub, LANES), F32)
            h2_ref[pl.ds(c * sub * TOK_PITCH + s, sub, stride=TOK_PITCH), :] = val


def _outproj(oa, ob, x, mod3, boff, g_out_a, g_out_b, g_norm_ffn, w_out_bf, w_router_pad):
    b, t, d = x.shape
    tm = 512
    nr = w_router_pad.shape[1]
    assert d // LANES <= TOK_PITCH
    row = lambda k: (lambda bi, ti: ((bi + boff) * 6 + k, 0, 0))
    return pl.pallas_call(
        _outproj_kernel,
        grid=(b, t // tm),
        in_specs=[
            pl.BlockSpec((1, tm, WIDTH_A), lambda bi, ti: (bi, ti, 0)),
            pl.BlockSpec((1, tm, WIDTH_B), lambda bi, ti: (bi, ti, 0)),
            pl.BlockSpec((1, tm, d), lambda bi, ti: (bi, ti, 0)),
            pl.BlockSpec((1, 1, d), row(2)),
            pl.BlockSpec((1, 1, d), row(4)),
            pl.BlockSpec((1, 1, d), row(3)),
            pl.BlockSpec((1, WIDTH_A), lambda bi, ti: (0, 0)),
            pl.BlockSpec((1, WIDTH_B), lambda bi, ti: (0, 0)),
            pl.BlockSpec((1, d), lambda bi, ti: (0, 0)),
            pl.BlockSpec((WIDTH_A + WIDTH_B, d), lambda bi, ti: (0, 0)),
            pl.BlockSpec((d, nr), lambda bi, ti: (0, 0)),
        ],
        out_specs=[
            pl.BlockSpec((1, tm, d), lambda bi, ti: (bi, ti, 0)),
            pl.BlockSpec((tm * TOK_PITCH, LANES), lambda bi, ti: (bi * (t // tm) + ti, 0)),
            pl.BlockSpec((1, tm, nr), lambda bi, ti: (bi, ti, 0)),
        ],
        out_shape=[jax.ShapeDtypeStruct((b, t, d), F32),
                   jax.ShapeDtypeStruct((b * t * TOK_PITCH, LANES), F32),
                   jax.ShapeDtypeStruct((b, t, nr), F32)],
        compiler_params=_cparams(("arbitrary", "arbitrary")),
        name="outproj",
    )(oa, ob, x, mod3, mod3, mod3, g_out_a, g_out_b, g_norm_ffn, w_out_bf, w_router_pad)


FFN_OUT_CHUNK = 512


def _ffn_kernel(idx_ref, idx_next_ref, h_ref, wg_ref, wu_ref, wd_ref, g_ref, o_ref,
                xslab, xb, hm, wdb, sem, *, tm, slab):
    f = pl.program_id(2)
    k = pl.program_id(0) * pl.num_programs(1) + pl.program_id(1)
    nk = pl.num_programs(0) * pl.num_programs(1)

    def issue_gather(ids_ref):
        def body(r, carry):
            pltpu.make_async_copy(h_ref.at[pl.ds(ids_ref[0, 0, r] * TOK_PITCH, slab), :],
                                  xslab.at[pl.ds(r * TOK_PITCH, slab), :], sem.at[0]).start()
            return carry
        lax.fori_loop(0, tm, body, 0, unroll=8)

    @pl.when(f == 0)
    def _():
        @pl.when(k == 0)
        def _():
            issue_gather(idx_ref)

        pltpu.make_async_copy(h_ref.at[pl.ds(0, tm * slab), :], xslab.at[pl.ds(0, tm * slab), :],
                              sem.at[0]).wait()
        for s in range(slab):
            xb[:, s * LANES:(s + 1) * LANES] = xslab[pl.ds(s, tm, stride=TOK_PITCH), :].astype(BF16)

        @pl.when(k + 1 < nk)
        def _():
            issue_gather(idx_next_ref)

    @pl.when(pl.program_id(1) == 0)
    def _():
        wdb[f] = wd_ref[0].astype(BF16)

    x = xb[...]
    a = jnp.dot(x, wg_ref[0].astype(BF16), preferred_element_type=F32)
    u = jnp.dot(x, wu_ref[0].astype(BF16), preferred_element_type=F32)
    hm[f] = (a * (1.0 / (1.0 + jnp.exp(-a))) * u).astype(BF16)

    @pl.when(f == pl.num_programs(2) - 1)
    def _():
        nf = hm.shape[0]
        for c in range(o_ref.shape[2] // FFN_OUT_CHUNK):
            cols = slice(c * FFN_OUT_CHUNK, (c + 1) * FFN_OUT_CHUNK)
            y = jnp.dot(hm[0], wdb[0, :, cols], preferred_element_type=F32)
            for j in range(1, nf):
                y += jnp.dot(hm[j], wdb[j, :, cols], preferred_element_type=F32)
            o_ref[0, :, cols] = (y * g_ref[0]).astype(o_ref.dtype)


def _ffn(idx, h2slab, wg, wu, wd, g):
    e, cap = idx.shape
    d = wg.shape[1]
    slab = d // LANES
    fdim = wg.shape[2]
    tm = min(cap, 1024)
    tf = 256
    nf = fdim // tf
    mt = cap // tm
    nk = e * mt
    idx3 = idx.reshape(nk, 1, tm)
    return pl.pallas_call(
        functools.partial(_ffn_kernel, tm=tm, slab=slab),
        grid=(e, mt, fdim // tf),
        in_specs=[
            pl.BlockSpec((1, 1, tm), lambda ei, mi, fi: (ei * mt + mi, 0, 0),
                         memory_space=pltpu.SMEM),
            pl.BlockSpec((1, 1, tm), lambda ei, mi, fi: (jnp.minimum(ei * mt + mi + 1, nk - 1), 0, 0),
                         memory_space=pltpu.SMEM),
            pl.BlockSpec(memory_space=pl.ANY),
            pl.BlockSpec((1, d, tf), lambda ei, mi, fi: (ei, 0, fi)),
            pl.BlockSpec((1, d, tf), lambda ei, mi, fi: (ei, 0, fi)),
            pl.BlockSpec((1, tf, d), lambda ei, mi, fi: (ei, jnp.where(mi == 0, fi, nf - 1), 0)),
            pl.BlockSpec((1, tm, 1), lambda ei, mi, fi: (ei, mi, 0)),
        ],
        out_specs=pl.BlockSpec((1, tm, d), lambda ei, mi, fi: (ei, mi, 0)),
        out_shape=jax.ShapeDtypeStruct((e, cap, d), BF16),
        scratch_shapes=[pltpu.VMEM((tm * TOK_PITCH, LANES), F32),
                        pltpu.VMEM((tm, d), BF16),
                        pltpu.VMEM((nf, tm, tf), BF16),
                        pltpu.VMEM((nf, tf, d), BF16),
                        pltpu.SemaphoreType.DMA((1,))],
        compiler_params=_cparams(("arbitrary", "arbitrary", "arbitrary")),
        name="expert_ffn",
    )(idx3, idx3, h2slab, wg, wu, wd, g.reshape(e, cap, 1))


def _prefix_counts(mask_f32, upper, lower):
    within = jnp.dot(mask_f32.astype(BF16), upper, preferred_element_type=F32)
    tot = jnp.broadcast_to(within[:, LANES - 1:LANES], within.shape)
    offs = jnp.dot(lower, tot.astype(BF16), preferred_element_type=F32)
    return within, offs


def _route_kernel(lt_ref, idx_ref, g_ref, pos_ref, offs_ref, aff_scr, thr_scr, *, cap):
    e = pl.program_id(0)
    nchunk = lt_ref.shape[1]
    capf = jnp.float32(cap)

    @pl.when(e == 0)
    def _():
        l = lt_ref[...]
        ex = jnp.exp(l - jnp.max(l, axis=0, keepdims=True))
        aff = ex / jnp.sum(ex, axis=0, keepdims=True)
        aff_scr[...] = aff

        def bit_step(i, thr_bits):
            cand = thr_bits | lax.shift_left(jnp.int32(1), 30 - i)
            cnt = jnp.sum((aff >= lax.bitcast_convert_type(cand, F32)).astype(F32),
                          axis=(1, 2), keepdims=True)
            return jnp.where(cnt >= capf, cand, thr_bits)

        thr_bits = lax.fori_loop(0, 31, bit_step, jnp.zeros((lt_ref.shape[0], 1, 1), jnp.int32))
        thr_scr[...] = jnp.broadcast_to(lax.bitcast_convert_type(thr_bits, F32), thr_scr.shape)

    a = aff_scr[e]
    thr = thr_scr[e][0:1, 0:1]

    def count(m):
        return jnp.sum(m.astype(F32), axis=(0, 1), keepdims=True)

    ri = lax.broadcasted_iota(jnp.int32, (LANES, LANES), 0)
    ci = lax.broadcasted_iota(jnp.int32, (LANES, LANES), 1)
    upper = (ri <= ci).astype(BF16)
    rc = lax.broadcasted_iota(jnp.int32, (nchunk, nchunk), 0)
    cc = lax.broadcasted_iota(jnp.int32, (nchunk, nchunk), 1)
    lower = (cc < rc).astype(BF16)

    gt = a > thr
    eq = a == thr
    need = capf - count(gt)
    eq_within, eq_offs = _prefix_counts(eq.astype(F32), upper, lower)
    sel = gt | (eq & ((eq_within + eq_offs) <= need))
    within, offs = _prefix_counts(sel.astype(F32), upper, lower)
    pos_ref[0] = jnp.where(sel, within + offs - 1.0, -1.0).astype(jnp.int32)
    offs_ref[0] = offs.astype(jnp.int32)

    offs_col = offs[:, 0:1]
    ends_col = offs_col + within[:, LANES - 1:LANES]
    slot = lax.broadcasted_iota(jnp.int32, (1, cap), 1).astype(F32)
    cstar = jnp.sum((ends_col <= slot).astype(F32), axis=0, keepdims=True)
    chunk_oh = lax.broadcasted_iota(jnp.int32, (nchunk, cap), 0).astype(F32) == cstar
    rank = slot - jnp.sum(jnp.where(chunk_oh, offs_col, 0.0), axis=0, keepdims=True)
    oh = chunk_oh.astype(BF16)
    tdot = functools.partial(lax.dot_general, dimension_numbers=(((0,), (0,)), ((), ())),
                             preferred_element_type=F32)
    wsel = tdot(within.astype(BF16), oh)
    lstar = jnp.sum((wsel <= rank).astype(F32), axis=0, keepdims=True)
    idx_ref[0] = (cstar * LANES + lstar).astype(jnp.int32)

    a1 = a.astype(BF16)
    r1 = a - a1.astype(F32)
    a2 = r1.astype(BF16)
    a3 = (r1 - a2.astype(F32)).astype(BF16)
    asel = (tdot(a1, oh) + tdot(a2, oh)) + tdot(a3, oh)
    lane_oh = lax.broadcasted_iota(jnp.int32, (LANES, cap), 0).astype(F32) == lstar
    g_ref[0] = jnp.sum(jnp.where(lane_oh, asel, 0.0), axis=0, keepdims=True)


def _route(lt, cap):
    e, nchunk, _ = lt.shape
    return pl.pallas_call(
        functools.partial(_route_kernel, cap=cap),
        grid=(e,),
        in_specs=[pl.BlockSpec((e, nchunk, LANES), lambda ei: (0, 0, 0))],
        out_specs=[pl.BlockSpec((1, 1, cap), lambda ei: (ei, 0, 0)),
                   pl.BlockSpec((1, 1, cap), lambda ei: (ei, 0, 0)),
                   pl.BlockSpec((1, nchunk, LANES), lambda ei: (ei, 0, 0)),
                   pl.BlockSpec((1, nchunk, LANES), lambda ei: (ei, 0, 0))],
        out_shape=[jax.ShapeDtypeStruct((e, 1, cap), jnp.int32),
                   jax.ShapeDtypeStruct((e, 1, cap), F32),
                   jax.ShapeDtypeStruct((e, nchunk, LANES), jnp.int32),
                   jax.ShapeDtypeStruct((e, nchunk, LANES), jnp.int32)],
        scratch_shapes=[pltpu.VMEM((e, nchunk, LANES), F32),
                        pltpu.VMEM((e, SUBLANES, LANES), F32)],
        compiler_params=_cparams(("arbitrary",)),
        name="route",
    )(lt)


COMB_TOK = 256
COMB_WIN = 64
COMB_ALIGN = 16


def _combine_kernel(tab_ref, tab_next_ref, startv_ref, pos_ref, x1_ref, gate_ref, ye_ref, o_ref,
                    ybuf, ybuf_x, sems, sem_x, *, cap):
    k = pl.program_id(0)
    nk = pl.num_programs(0)
    slot = lax.rem(k, 2)
    ne = ye_ref.shape[0]
    last_start = cap - COMB_WIN

    def window_copy(e, logical_start, dst, sem):
        row0 = pl.multiple_of(jnp.minimum(logical_start, last_start), COMB_ALIGN)
        return pltpu.make_async_copy(ye_ref.at[e, pl.ds(row0, COMB_WIN), :],
                                     dst.at[pl.ds(e * COMB_WIN, COMB_WIN), :], sem)

    def issue(tab, sl):
        for e in range(ne):
            window_copy(e, tab[0, 0, e], ybuf.at[sl], sems.at[sl]).start()

    @pl.when(k == 0)
    def _():
        issue(tab_ref, slot)

    @pl.when(k + 1 < nk)
    def _():
        issue(tab_next_ref, 1 - slot)

    pltpu.make_async_copy(ybuf.at[slot], ybuf.at[slot], sems.at[slot]).wait()

    pos = pos_ref[0]
    start0 = startv_ref[0]
    width = ne * COMB_WIN
    brow = lax.broadcasted_iota(jnp.int32, (width, ne), 0)
    expand = (lax.shift_right_logical(brow, int(math.log2(COMB_WIN)))
              == lax.broadcasted_iota(jnp.int32, (width, ne), 1)).astype(BF16)
    row_in_win = lax.bitwise_and(lax.broadcasted_iota(jnp.int32, (width, 1), 0),
                                 COMB_WIN - 1).astype(F32)

    def placed_sum(p, rows_bf16):
        logical = start0 + p * COMB_WIN
        rel = pos - logical
        inside = (pos >= 0) & (rel >= 0) & (rel < COMB_WIN)
        row = jnp.where(inside, pos - jnp.minimum(logical, last_start), -1)
        spread = jnp.dot(expand, row.astype(F32).astype(BF16), preferred_element_type=F32)
        onehot = (spread == row_in_win).astype(BF16)
        return lax.dot_general(onehot, rows_bf16, (((0,), (0,)), ((), ())),
                               preferred_element_type=F32)

    gate = gate_ref[0]
    o_ref[...] = x1_ref[...] + gate * placed_sum(0, ybuf[slot])

    def extra_pass(p, carry):
        for e in range(ne):
            window_copy(e, tab_ref[0, 0, e] + p * COMB_WIN, ybuf_x, sem_x.at[0]).start()
        pltpu.make_async_copy(ybuf_x, ybuf_x, sem_x.at[0]).wait()
        o_ref[...] += gate * placed_sum(p, ybuf_x[...])
        return carry

    lax.fori_loop(1, tab_ref[0, 0, ne], extra_pass, 0)


def _combine(pos, offs, ye, x1, mod3, *, t, boff):
    ne, cap, d = ye.shape
    n = x1.shape[0]
    ntiles = n // COMB_TOK
    chunks_per_tile = COMB_TOK // LANES
    before = offs[:, ::chunks_per_tile, 0].T
    after = jnp.concatenate([before[1:], jnp.full((1, ne), cap, jnp.int32)], axis=0)
    starts = before - before % COMB_ALIGN
    passes = jnp.maximum(1, (jnp.max(after - starts, axis=1) + COMB_WIN - 1) // COMB_WIN)
    table = jnp.concatenate([starts, passes[:, None]], axis=1).astype(jnp.int32)
    table = table.reshape(ntiles, 1, ne + 1)
    startv = starts.astype(jnp.int32).reshape(ntiles, ne, 1)
    pos_tiles = pos.reshape(ne, ntiles, COMB_TOK).transpose(1, 0, 2)
    tiles_per_seq = t // COMB_TOK
    return pl.pallas_call(
        functools.partial(_combine_kernel, cap=cap),
        grid=(ntiles,),
        in_specs=[
            pl.BlockSpec((1, 1, ne + 1), lambda k: (k, 0, 0), memory_space=pltpu.SMEM),
            pl.BlockSpec((1, 1, ne + 1), lambda k: (jnp.minimum(k + 1, ntiles - 1), 0, 0),
                         memory_space=pltpu.SMEM),
            pl.BlockSpec((1, ne, 1), lambda k: (k, 0, 0)),
            pl.BlockSpec((1, ne, COMB_TOK), lambda k: (k, 0, 0)),
            pl.BlockSpec((COMB_TOK, d), lambda k: (k, 0)),
            pl.BlockSpec((1, 1, d), lambda k: ((k // tiles_per_seq + boff) * 6 + 5, 0, 0)),
            pl.BlockSpec(memory_space=pl.ANY),
        ],
        out_specs=pl.BlockSpec((COMB_TOK, d), lambda k: (k, 0)),
        out_shape=jax.ShapeDtypeStruct(x1.shape, x1.dtype),
        scratch_shapes=[pltpu.VMEM((2, ne * COMB_WIN, d), BF16),
                        pltpu.VMEM((ne * COMB_WIN, d), BF16),
                        pltpu.SemaphoreType.DMA((2,)),
                        pltpu.SemaphoreType.DMA((1,))],
        compiler_params=_cparams(("arbitrary",)),
        name="combine",
    )(table, table, startv, pos_tiles, x1, mod3, ye)


def _rope_tables(t):
    half = HEAD_DIM // 2
    quarter = half // 2
    freqs = ROPE_THETA ** (-(jnp.arange(quarter, dtype=F32) / quarter))
    rows = t // GRID_W
    row_ids = jnp.repeat(jnp.arange(rows, dtype=jnp.int32), GRID_W).astype(F32)
    col_ids = jnp.tile(jnp.arange(GRID_W, dtype=jnp.int32), rows).astype(F32)
    ang_r = row_ids[:, None] * freqs[None, :]
    ang_c = col_ids[:, None] * freqs[None, :]
    z = jnp.zeros_like(ang_r)
    cos_t = jnp.concatenate([jnp.cos(ang_r)] * 2 + [jnp.cos(ang_c)] * 2, axis=1)
    sa_t = jnp.concatenate([-jnp.sin(ang_r), z, -jnp.sin(ang_c), z], axis=1)
    sb_t = jnp.concatenate([z, jnp.sin(ang_r), z, jnp.sin(ang_c)], axis=1)
    return cos_t, sa_t, sb_t


def kernel(x_prompt, x_sample, c_prompt, c_sample, rel_bias_table, w_ada, b_ada, g_norm_mix, g_norm_ffn, w_in, g_q_a, g_k_a, g_q_b, g_k_b, g_out_a, g_out_b, w_out, w_router, w_gate, w_up, w_down):
    d = D_MODEL
    nbp, nbs = c_prompt.shape[0], c_sample.shape[0]
    c_all = jnp.concatenate([c_prompt, c_sample, jnp.zeros((16 - nbp - nbs, d), F32)], axis=0)
    mod = _ada(c_all, w_ada[0], b_ada[0])
    mod3 = mod.reshape(16 * 6, 1, d)

    scale = HEAD_DIM ** -0.5
    gains = jnp.stack([g_q_a[0] * scale, g_k_a[0], g_q_b[0] * scale, g_k_b[0]], axis=0)
    w_in_bf = w_in[0].astype(BF16)
    w_out_bf = w_out[0].astype(BF16)
    w_router_pad = jnp.pad(w_router[0], ((0, 0), (0, HEAD_DIM - N_EXPERTS))).astype(BF16)
    wg, wu, wd = w_gate[0], w_up[0], w_down[0]
    bias_tiles = _bias_tiles(rel_bias_table)
    gn1 = g_norm_mix[0].reshape(1, d)
    gn2 = g_norm_ffn[0].reshape(1, d)
    ga = g_out_a[0].reshape(1, WIDTH_A)
    gb = g_out_b[0].reshape(1, WIDTH_B)

    def run(x, boff):
        b, t, _ = x.shape
        cos_t, sa_t, sb_t = _rope_tables(t)
        proj = _inproj(x, mod3, boff, gn1, w_in_bf, gains, cos_t, sa_t, sb_t)
        oa = _attn_a(proj, bias_tiles)
        ob = _attn_b(proj)
        x1, h2, logits = _outproj(oa, ob, x, mod3, boff, ga, gb, gn2, w_out_bf, w_router_pad)
        n = b * t
        cap = EC_CAPACITY_FACTOR * n // N_EXPERTS
        lt = logits.reshape(n, -1)[:, :N_EXPERTS].T.reshape(N_EXPERTS, n // LANES, LANES)
        idx, g, pos, offs = _route(lt, cap)
        ye = _ffn(idx.reshape(N_EXPERTS, cap), h2, wg, wu, wd, g.reshape(N_EXPERTS, cap))
        out = _combine(pos, offs, ye, x1.reshape(n, d), mod3, t=t, boff=boff)
        return out.reshape(b, t, d)

    return (run(x_prompt, 0), run(x_sample, nbp))
```

```python
import functools
import math

import jax
import jax.numpy as jnp
import numpy as np
from jax import lax
from jax.experimental import pallas as pl
from jax.experimental.pallas import tpu as pltpu

F32 = jnp.float32
BF16 = jnp.bfloat16

D_MODEL = 2048
HEAD_DIM = 128
N_HEADS_A = 8
N_HEADS_B = 8
N_KV_B = 2
GQA_GROUP = N_HEADS_B // N_KV_B
WIDTH_A = N_HEADS_A * HEAD_DIM
WIDTH_B = N_HEADS_B * HEAD_DIM
KV_WIDTH_B = N_KV_B * HEAD_DIM
IN_COLS = 3 * WIDTH_A + WIDTH_B + 2 * KV_WIDTH_B
DIL_CONFIGS = ((128, 1), (512, 4), (2048, 16))
NUM_BUCKETS = 32
MAX_DISTANCE = 1024
GRID_W = 64
ROPE_THETA = 10000.0
N_EXPERTS = 16
EC_CAPACITY_FACTOR = 2
D_EXPERT = 2048
EPS = 1e-6
NEG_INF = -1e30

VMEM_LIMIT_V7X = 56 * 1024 * 1024
LANES = 128
SUBLANES = 8
TOK_PITCH = 20

COL_QA, COL_KA, COL_VA = 0, 8, 16
COL_QB, COL_KB, COL_VB = 24, 32, 34

IN_TN = 512
IN_CHAINS = 4
OUT_CHAINS = 2
A_QB = 128
A_KB = 256
A_RADIUS = 64
A_PAD = A_RADIUS * 16
A_UNROLL = 16
B_SUB = 128
B_SCORE_ELEMS = 8 * 1024 * 1024


def _cparams(sem):
    return pltpu.CompilerParams(dimension_semantics=sem, vmem_limit_bytes=VMEM_LIMIT_V7X)


def _ada_kernel(c_ref, w_ref, b_ref, o_ref):
    c = c_ref[...]
    s = c * (1.0 / (1.0 + jnp.exp(-c)))
    o_ref[...] = jnp.dot(s.astype(BF16), w_ref[...].astype(BF16),
                         preferred_element_type=F32) + b_ref[...]


def _ada(c_all, w_ada, b_ada):
    rows, d = c_all.shape
    n = w_ada.shape[1]
    tn = 1024
    return pl.pallas_call(
        _ada_kernel,
        grid=(n // tn,),
        in_specs=[pl.BlockSpec((rows, d), lambda j: (0, 0)),
                  pl.BlockSpec((d, tn), lambda j: (0, j)),
                  pl.BlockSpec((1, tn), lambda j: (0, j))],
        out_specs=pl.BlockSpec((rows, tn), lambda j: (0, j)),
        out_shape=jax.ShapeDtypeStruct((rows, n), F32),
        compiler_params=_cparams(("arbitrary",)),
        name="ada_mod",
    )(c_all, w_ada, b_ada.reshape(1, n))


def _head_norm(a, g):
    ms = jnp.mean(a * a, axis=-1, keepdims=True)
    return a * lax.rsqrt(ms + EPS) * g


def _inproj_kernel(x_ref, sc_ref, sh_ref, gn_ref, w_ref, gains_ref, cos_ref, sa_ref, sb_ref,
                   o_ref, h_scr):
    j = pl.program_id(2)

    @pl.when(j == 0)
    def _():
        x = x_ref[0]
        ms = jnp.mean(x * x, axis=-1, keepdims=True)
        y = x * lax.rsqrt(ms + EPS) * gn_ref[...]
        h_scr[...] = (y * (1.0 + sc_ref[0]) + sh_ref[0]).astype(BF16)

    def rope(a, rows):
        return (a * cos_ref[rows, :] + pltpu.roll(a, 96, 1) * sa_ref[rows, :]
                + pltpu.roll(a, 32, 1) * sb_ref[rows, :])

    def plain(a, rows):
        return a

    def norm(g_row):
        return lambda a, rows: _head_norm(a, gains_ref[g_row:g_row + 1, :])

    def norm_rope(g_row):
        return lambda a, rows: rope(_head_norm(a, gains_ref[g_row:g_row + 1, :]), rows)

    def tile(head_fns):
        sub = h_scr.shape[0] // IN_CHAINS
        for c in range(IN_CHAINS):
            rows = slice(c * sub, (c + 1) * sub)
            acc = jnp.dot(h_scr[rows, :], w_ref[...], preferred_element_type=F32)
            for hh, fn in enumerate(head_fns):
                sl = slice(hh * HEAD_DIM, (hh + 1) * HEAD_DIM)
                o_ref[0, rows, sl] = fn(acc[:, sl], rows).astype(BF16)

    @pl.when(j < 2)
    def _():
        tile([norm(0)] * 4)

    @pl.when((j >= 2) & (j < 4))
    def _():
        tile([norm(1)] * 4)

    @pl.when((j >= 4) & (j < 6))
    def _():
        tile([plain] * 4)

    @pl.when((j >= 6) & (j < 8))
    def _():
        tile([norm_rope(2)] * 4)

    @pl.when(j == 8)
    def _():
        tile([norm_rope(3)] * 2 + [plain] * 2)


def _inproj(x, mod3, boff, g_norm, w_in_bf, gains, cos_t, sa_t, sb_t):
    b, t, d = x.shape
    tm = 1024
    nj = IN_COLS // IN_TN
    return pl.pallas_call(
        _inproj_kernel,
        grid=(b, t // tm, nj),
        in_specs=[
            pl.BlockSpec((1, tm, d), lambda bi, ti, j: (bi, ti, 0)),
            pl.BlockSpec((1, 1, d), lambda bi, ti, j: ((bi + boff) * 6 + 1, 0, 0)),
            pl.BlockSpec((1, 1, d), lambda bi, ti, j: ((bi + boff) * 6 + 0, 0, 0)),
            pl.BlockSpec((1, d), lambda bi, ti, j: (0, 0)),
            pl.BlockSpec((d, IN_TN), lambda bi, ti, j: (0, j)),
            pl.BlockSpec((4, HEAD_DIM), lambda bi, ti, j: (0, 0)),
            pl.BlockSpec((tm, HEAD_DIM), lambda bi, ti, j: (ti, 0)),
            pl.BlockSpec((tm, HEAD_DIM), lambda bi, ti, j: (ti, 0)),
            pl.BlockSpec((tm, HEAD_DIM), lambda bi, ti, j: (ti, 0)),
        ],
        out_specs=pl.BlockSpec((1, tm, IN_TN), lambda bi, ti, j: (bi, ti, j)),
        out_shape=jax.ShapeDtypeStruct((b, t, IN_COLS), BF16),
        scratch_shapes=[pltpu.VMEM((tm, d), BF16)],
        compiler_params=_cparams(("arbitrary", "arbitrary", "arbitrary")),
        name="inproj",
    )(x, mod3, mod3, g_norm, w_in_bf, gains, cos_t, sa_t, sb_t)


def _attn_a_kernel(q_ref, k_ref, v_ref, bias_ref, o_ref, qf, kf, vf, acc, mm, ll, *, t):
    zpad = jnp.zeros((A_PAD, HEAD_DIM), F32)
    kf[0:A_PAD, :] = zpad
    vf[0:A_PAD, :] = zpad
    kf[A_PAD + t:A_PAD + t + A_PAD, :] = zpad
    vf[A_PAD + t:A_PAD + t + A_PAD, :] = zpad
    kf[A_PAD:A_PAD + t, :] = k_ref[0].astype(F32)
    vf[A_PAD:A_PAD + t, :] = v_ref[0].astype(F32)
    qf[...] = q_ref[0].astype(F32)
    ones = jnp.ones((A_KB, HEAD_DIM), BF16)

    for bi, (_, dil) in enumerate(DIL_CONFIGS):
        sub_len = t // dil
        nmb = sub_len // A_QB
        shift = int(math.log2(nmb))

        def rows(start, size, dil=dil):
            return pl.ds(start, size) if dil == 1 else pl.ds(start, size, stride=dil)

        def body(idx, carry, bi=bi, dil=dil, sub_len=sub_len, nmb=nmb, shift=shift, rows=rows):
            rho = lax.shift_right_logical(idx, shift)
            mb = lax.bitwise_and(idx, nmb - 1)
            qstart = rho + mb * (A_QB * dil)
            kstart = A_PAD + qstart - A_RADIUS * dil
            q = qf[rows(qstart, A_QB), :].astype(BF16)
            k = kf[rows(kstart, A_KB), :].astype(BF16)
            v = vf[rows(kstart, A_KB), :].astype(BF16)
            s = lax.dot_general(q, k, (((1,), (1,)), ((), ())), preferred_element_type=F32)
            s = s + bias_ref[bi, 0]
            kidx = mb * A_QB - A_RADIUS + lax.broadcasted_iota(jnp.int32, (1, A_KB), 1)
            s = jnp.where((kidx >= 0) & (kidx < sub_len), s, NEG_INF)
            mblk = jnp.max(s, axis=-1, keepdims=True)
            v1 = jnp.concatenate([v, ones], axis=1)
            p = jnp.exp(s - mblk).astype(BF16)
            pv = jnp.dot(p, v1, preferred_element_type=F32)
            acc[bi, rows(qstart, A_QB), :] = pv[:, :HEAD_DIM]
            ll[bi, rows(qstart, A_QB), :] = pv[:, HEAD_DIM:]
            mm[bi, rows(qstart, A_QB), :] = jnp.broadcast_to(mblk, (A_QB, HEAD_DIM))
            return carry

        lax.fori_loop(0, dil * nmb, body, 0, unroll=A_UNROLL)

    m0, m1, m2 = mm[0], mm[1], mm[2]
    mtop = jnp.maximum(m0, jnp.maximum(m1, m2))
    w0, w1, w2 = jnp.exp(m0 - mtop), jnp.exp(m1 - mtop), jnp.exp(m2 - mtop)
    num = w0 * acc[0] + w1 * acc[1] + w2 * acc[2]
    den = w0 * ll[0] + w1 * ll[1] + w2 * ll[2]
    o_ref[0] = (num / den).astype(BF16)


def _attn_a(proj, bias_tiles):
    b, t, _ = proj.shape
    kern = functools.partial(_attn_a_kernel, t=t)
    return pl.pallas_call(
        kern,
        grid=(b, N_HEADS_A),
        in_specs=[
            pl.BlockSpec((1, t, HEAD_DIM), lambda bi, h: (bi, 0, COL_QA + h)),
            pl.BlockSpec((1, t, HEAD_DIM), lambda bi, h: (bi, 0, COL_KA + h)),
            pl.BlockSpec((1, t, HEAD_DIM), lambda bi, h: (bi, 0, COL_VA + h)),
            pl.BlockSpec((3, 1, A_QB, A_KB), lambda bi, h: (0, h, 0, 0)),
        ],
        out_specs=pl.BlockSpec((1, t, HEAD_DIM), lambda bi, h: (bi, 0, h)),
        out_shape=jax.ShapeDtypeStruct((b, t, WIDTH_A), BF16),
        scratch_shapes=[
            pltpu.VMEM((t, HEAD_DIM), F32),
            pltpu.VMEM((t + 2 * A_PAD, HEAD_DIM), F32),
            pltpu.VMEM((t + 2 * A_PAD, HEAD_DIM), F32),
            pltpu.VMEM((len(DIL_CONFIGS), t, HEAD_DIM), F32),
            pltpu.VMEM((len(DIL_CONFIGS), t, HEAD_DIM), F32),
            pltpu.VMEM((len(DIL_CONFIGS), t, HEAD_DIM), F32),
        ],
        compiler_params=_cparams(("arbitrary", "arbitrary")),
        name="attn_dilated",
    )(proj, proj, proj, bias_tiles)


def _t5_bucket(rel):
    nb = NUM_BUCKETS // 2
    max_exact = nb // 2
    sign_off = np.where(rel > 0, nb, 0)
    n = np.abs(rel)
    nf = np.maximum(n, 1).astype(np.float32)
    large = max_exact + (np.log(nf / np.float32(max_exact))
                         / np.float32(math.log(MAX_DISTANCE / max_exact))
                         * np.float32(nb - max_exact)).astype(np.int32)
    large = np.minimum(large, nb - 1)
    return (sign_off + np.where(n < max_exact, n, large)).astype(np.int32)


def _bias_tiles(rel_bias_table):
    qi = np.arange(A_QB, dtype=np.int32)[:, None]
    kj = np.arange(A_KB, dtype=np.int32)[None, :]
    rel = kj - A_RADIUS - qi
    in_band = jnp.asarray(np.abs(rel) <= A_RADIUS)
    tiles = []
    for _, dil in DIL_CONFIGS:
        onehot = jnp.asarray(_t5_bucket(rel * dil)[..., None]
                             == np.arange(NUM_BUCKETS, dtype=np.int32), F32)
        bias = jnp.einsum('qkn,nh->hqk', onehot, rel_bias_table.astype(F32),
                          precision=lax.Precision.HIGHEST)
        tiles.append(jnp.where(in_band[None], bias, NEG_INF))
    return jnp.stack(tiles, axis=0)


def _attn_b_kernel(q_ref, k_ref, v_ref, o_ref, v1_scr, *, tq):
    @pl.when(pl.program_id(2) == 0)
    def _():
        v1_scr[:, 0:HEAD_DIM] = v_ref[0]
        v1_scr[:, HEAD_DIM:2 * HEAD_DIM] = jnp.ones(v_ref.shape[1:], BF16)

    for c in range(tq // B_SUB):
        q = q_ref[0, c * B_SUB:(c + 1) * B_SUB, :]
        qs = jnp.concatenate([q[:, i * HEAD_DIM:(i + 1) * HEAD_DIM] for i in range(GQA_GROUP)],
                             axis=0)
        s = lax.dot_general(qs, k_ref[0], (((1,), (1,)), ((), ())), preferred_element_type=F32)
        m = jnp.max(s, axis=-1, keepdims=True)
        p = jnp.exp(s - m).astype(BF16)
        pv = jnp.dot(p, v1_scr[...], preferred_element_type=F32)
        o = pv[:, :HEAD_DIM] / pv[:, HEAD_DIM:]
        o_ref[0, c * B_SUB:(c + 1) * B_SUB, :] = jnp.concatenate(
            [o[i * B_SUB:(i + 1) * B_SUB] for i in range(GQA_GROUP)], axis=1).astype(BF16)


def _attn_b(proj):
    b, t, _ = proj.shape
    tq = B_SUB * max(1, B_SCORE_ELEMS // (GQA_GROUP * B_SUB * t))
    gw = GQA_GROUP * HEAD_DIM
    return pl.pallas_call(
        functools.partial(_attn_b_kernel, tq=tq),
        grid=(b, N_KV_B, t // tq),
        in_specs=[
            pl.BlockSpec((1, tq, gw), lambda bi, g, qi: (bi, qi, COL_QB // GQA_GROUP + g)),
            pl.BlockSpec((1, t, HEAD_DIM), lambda bi, g, qi: (bi, 0, COL_KB + g)),
            pl.BlockSpec((1, t, HEAD_DIM), lambda bi, g, qi: (bi, 0, COL_VB + g)),
        ],
        out_specs=pl.BlockSpec((1, tq, gw), lambda bi, g, qi: (bi, qi, g)),
        out_shape=jax.ShapeDtypeStruct((b, t, WIDTH_B), BF16),
        scratch_shapes=[pltpu.VMEM((t, 2 * HEAD_DIM), BF16)],
        compiler_params=_cparams(("arbitrary", "arbitrary", "arbitrary")),
        name="attn_gqa",
    )(proj, proj, proj)


def _outproj_kernel(oa_ref, ob_ref, x_ref, gt_ref, sc_ref, sh_ref, ga_ref, gb_ref, gn_ref,
                    w_ref, wr_ref, x1_ref, h2_ref, lg_ref):
    def wide_norm(o, g_ref_):
        o = o.astype(F32)
        ms = jnp.mean(o * o, axis=-1, keepdims=True)
        return (o * lax.rsqrt(ms + EPS) * g_ref_[...]).astype(BF16)

    tm, dd = x_ref.shape[1:]
    slab = dd // LANES
    sub = tm // OUT_CHAINS
    for c in range(OUT_CHAINS):
        rows = slice(c * sub, (c + 1) * sub)
        na = wide_norm(oa_ref[0, rows, :], ga_ref)
        nb = wide_norm(ob_ref[0, rows, :], gb_ref)
        mix = (jnp.dot(na, w_ref[0:WIDTH_A, :], preferred_element_type=F32)
               + jnp.dot(nb, w_ref[WIDTH_A:WIDTH_A + WIDTH_B, :], preferred_element_type=F32))
        x1 = x_ref[0, rows, :] + gt_ref[0] * mix
        x1_ref[0, rows, :] = x1
        ms = jnp.mean(x1 * x1, axis=-1, keepdims=True)
        h2 = (x1 * lax.rsqrt(ms + EPS) * gn_ref[...]) * (1.0 + sc_ref[0]) + sh_ref[0]
        lg_ref[0, rows, :] = jnp.dot(h2.astype(BF16), wr_ref[...], preferred_element_type=F32)
        for s in range(TOK_PITCH):
            val = h2[:, s * LANES:(s + 1) * LANES] if s < slab else jnp.zeros((sub, LANES), F32)
            h2_ref[pl.ds(c * sub * TOK_PITCH + s, sub, stride=TOK_PITCH), :] = val


def _outproj(oa, ob, x, mod3, boff, g_out_a, g_out_b, g_norm_ffn, w_out_bf, w_router_pad):
    b, t, d = x.shape
    tm = 512
    nr = w_router_pad.shape[1]
    assert d // LANES <= TOK_PITCH
    row = lambda k: (lambda bi, ti: ((bi + boff) * 6 + k, 0, 0))
    return pl.pallas_call(
        _outproj_kernel,
        grid=(b, t // tm),
        in_specs=[
            pl.BlockSpec((1, tm, WIDTH_A), lambda bi, ti: (bi, ti, 0)),
            pl.BlockSpec((1, tm, WIDTH_B), lambda bi, ti: (bi, ti, 0)),
            pl.BlockSpec((1, tm, d), lambda bi, ti: (bi, ti, 0)),
            pl.BlockSpec((1, 1, d), row(2)),
            pl.BlockSpec((1, 1, d), row(4)),
            pl.BlockSpec((1, 1, d), row(3)),
            pl.BlockSpec((1, WIDTH_A), lambda bi, ti: (0, 0)),
            pl.BlockSpec((1, WIDTH_B), lambda bi, ti: (0, 0)),
            pl.BlockSpec((1, d), lambda bi, ti: (0, 0)),
            pl.BlockSpec((WIDTH_A + WIDTH_B, d), lambda bi, ti: (0, 0)),
            pl.BlockSpec((d, nr), lambda bi, ti: (0, 0)),
        ],
        out_specs=[
            pl.BlockSpec((1, tm, d), lambda bi, ti: (bi, ti, 0)),
            pl.BlockSpec((tm * TOK_PITCH, LANES), lambda bi, ti: (bi * (t // tm) + ti, 0)),
            pl.BlockSpec((1, tm, nr), lambda bi, ti: (bi, ti, 0)),
        ],
        out_shape=[jax.ShapeDtypeStruct((b, t, d), F32),
                   jax.ShapeDtypeStruct((b * t * TOK_PITCH, LANES), F32),
                   jax.ShapeDtypeStruct((b, t, nr), F32)],
        compiler_params=_cparams(("arbitrary", "arbitrary")),
        name="outproj",
    )(oa, ob, x, mod3, mod3, mod3, g_out_a, g_out_b, g_norm_ffn, w_out_bf, w_router_pad)


FFN_OUT_CHUNK = 512


def _ffn_kernel(idx_ref, idx_next_ref, h_ref, wg_ref, wu_ref, wd_ref, g_ref, o_ref,
                xslab, xb, hm, wdb, sem, *, tm, slab):
    f = pl.program_id(2)
    k = pl.program_id(0) * pl.num_programs(1) + pl.program_id(1)
    nk = pl.num_programs(0) * pl.num_programs(1)

    def issue_gather(ids_ref):
        def body(r, carry):
            pltpu.make_async_copy(h_ref.at[pl.ds(ids_ref[0, 0, r] * TOK_PITCH, slab), :],
                                  xslab.at[pl.ds(r * TOK_PITCH, slab), :], sem.at[0]).start()
            return carry
        lax.fori_loop(0, tm, body, 0, unroll=8)

    @pl.when(f == 0)
    def _():
        @pl.when(k == 0)
        def _():
            issue_gather(idx_ref)

        pltpu.make_async_copy(h_ref.at[pl.ds(0, tm * slab), :], xslab.at[pl.ds(0, tm * slab), :],
                              sem.at[0]).wait()
        for s in range(slab):
            xb[:, s * LANES:(s + 1) * LANES] = xslab[pl.ds(s, tm, stride=TOK_PITCH), :].astype(BF16)

        @pl.when(k + 1 < nk)
        def _():
            issue_gather(idx_next_ref)

    @pl.when(pl.program_id(1) == 0)
    def _():
        wdb[f] = wd_ref[0].astype(BF16)

    x = xb[...]
    a = jnp.dot(x, wg_ref[0].astype(BF16), preferred_element_type=F32)
    u = jnp.dot(x, wu_ref[0].astype(BF16), preferred_element_type=F32)
    hm[f] = (a * (1.0 / (1.0 + jnp.exp(-a))) * u).astype(BF16)

    @pl.when(f == pl.num_programs(2) - 1)
    def _():
        nf = hm.shape[0]
        for c in range(o_ref.shape[2] // FFN_OUT_CHUNK):
            cols = slice(c * FFN_OUT_CHUNK, (c + 1) * FFN_OUT_CHUNK)
            y = jnp.dot(hm[0], wdb[0, :, cols], preferred_element_type=F32)
            for j in range(1, nf):
                y += jnp.dot(hm[j], wdb[j, :, cols], preferred_element_type=F32)
            o_ref[0, :, cols] = (y * g_ref[0]).astype(o_ref.dtype)


def _ffn(idx, h2slab, wg, wu, wd, g):
    e, cap = idx.shape
    d = wg.shape[1]
    slab = d // LANES
    fdim = wg.shape[2]
    tm = min(cap, 1024)
    tf = 256
    nf = fdim // tf
    mt = cap // tm
    nk = e * mt
    idx3 = idx.reshape(nk, 1, tm)
    return pl.pallas_call(
        functools.partial(_ffn_kernel, tm=tm, slab=slab),
        grid=(e, mt, fdim // tf),
        in_specs=[
            pl.BlockSpec((1, 1, tm), lambda ei, mi, fi: (ei * mt + mi, 0, 0),
                         memory_space=pltpu.SMEM),
            pl.BlockSpec((1, 1, tm), lambda ei, mi, fi: (jnp.minimum(ei * mt + mi + 1, nk - 1), 0, 0),
                         memory_space=pltpu.SMEM),
            pl.BlockSpec(memory_space=pl.ANY),
            pl.BlockSpec((1, d, tf), lambda ei, mi, fi: (ei, 0, fi)),
            pl.BlockSpec((1, d, tf), lambda ei, mi, fi: (ei, 0, fi)),
            pl.BlockSpec((1, tf, d), lambda ei, mi, fi: (ei, jnp.where(mi == 0, fi, nf - 1), 0)),
            pl.BlockSpec((1, tm, 1), lambda ei, mi, fi: (ei, mi, 0)),
        ],
        out_specs=pl.BlockSpec((1, tm, d), lambda ei, mi, fi: (ei, mi, 0)),
        out_shape=jax.ShapeDtypeStruct((e, cap, d), BF16),
        scratch_shapes=[pltpu.VMEM((tm * TOK_PITCH, LANES), F32),
                        pltpu.VMEM((tm, d), BF16),
                        pltpu.VMEM((nf, tm, tf), BF16),
                        pltpu.VMEM((nf, tf, d), BF16),
                        pltpu.SemaphoreType.DMA((1,))],
        compiler_params=_cparams(("arbitrary", "arbitrary", "arbitrary")),
        name="expert_ffn",
    )(idx3, idx3, h2slab, wg, wu, wd, g.reshape(e, cap, 1))


def _prefix_counts(mask_f32, upper, lower):
    within = jnp.dot(mask_f32.astype(BF16), upper, preferred_element_type=F32)
    tot = jnp.broadcast_to(within[:, LANES - 1:LANES], within.shape)
    offs = jnp.dot(lower, tot.astype(BF16), preferred_element_type=F32)
    return within, offs


def _route_kernel(lt_ref, idx_ref, g_ref, pos_ref, offs_ref, aff_scr, thr_scr, *, cap):
    e = pl.program_id(0)
    nchunk = lt_ref.shape[1]
    capf = jnp.float32(cap)

    @pl.when(e == 0)
    def _():
        l = lt_ref[...]
        ex = jnp.exp(l - jnp.max(l, axis=0, keepdims=True))
        aff = ex / jnp.sum(ex, axis=0, keepdims=True)
        aff_scr[...] = aff

        def bit_step(i, thr_bits):
            cand = thr_bits | lax.shift_left(jnp.int32(1), 30 - i)
            cnt = jnp.sum((aff >= lax.bitcast_convert_type(cand, F32)).astype(F32),
                          axis=(1, 2), keepdims=True)
            return jnp.where(cnt >= capf, cand, thr_bits)

        thr_bits = lax.fori_loop(0, 31, bit_step, jnp.zeros((lt_ref.shape[0], 1, 1), jnp.int32))
        thr_scr[...] = jnp.broadcast_to(lax.bitcast_convert_type(thr_bits, F32), thr_scr.shape)

    a = aff_scr[e]
    thr = thr_scr[e][0:1, 0:1]

    def count(m):
        return jnp.sum(m.astype(F32), axis=(0, 1), keepdims=True)

    ri = lax.broadcasted_iota(jnp.int32, (LANES, LANES), 0)
    ci = lax.broadcasted_iota(jnp.int32, (LANES, LANES), 1)
    upper = (ri <= ci).astype(BF16)
    rc = lax.broadcasted_iota(jnp.int32, (nchunk, nchunk), 0)
    cc = lax.broadcasted_iota(jnp.int32, (nchunk, nchunk), 1)
    lower = (cc < rc).astype(BF16)

    gt = a > thr
    eq = a == thr
    need = capf - count(gt)
    eq_within, eq_offs = _prefix_counts(eq.astype(F32), upper, lower)
    sel = gt | (eq & ((eq_within + eq_offs) <= need))
    within, offs = _prefix_counts(sel.astype(F32), upper, lower)
    pos_ref[0] = jnp.where(sel, within + offs - 1.0, -1.0).astype(jnp.int32)
    offs_ref[0] = offs.astype(jnp.int32)

    offs_col = offs[:, 0:1]
    ends_col = offs_col + within[:, LANES - 1:LANES]
    slot = lax.broadcasted_iota(jnp.int32, (1, cap), 1).astype(F32)
    cstar = jnp.sum((ends_col <= slot).astype(F32), axis=0, keepdims=True)
    chunk_oh = lax.broadcasted_iota(jnp.int32, (nchunk, cap), 0).astype(F32) == cstar
    rank = slot - jnp.sum(jnp.where(chunk_oh, offs_col, 0.0), axis=0, keepdims=True)
    oh = chunk_oh.astype(BF16)
    tdot = functools.partial(lax.dot_general, dimension_numbers=(((0,), (0,)), ((), ())),
                             preferred_element_type=F32)
    wsel = tdot(within.astype(BF16), oh)
    lstar = jnp.sum((wsel <= rank).astype(F32), axis=0, keepdims=True)
    idx_ref[0] = (cstar * LANES + lstar).astype(jnp.int32)

    a1 = a.astype(BF16)
    r1 = a - a1.astype(F32)
    a2 = r1.astype(BF16)
    a3 = (r1 - a2.astype(F32)).astype(BF16)
    asel = (tdot(a1, oh) + tdot(a2, oh)) + tdot(a3, oh)
    lane_oh = lax.broadcasted_iota(jnp.int32, (LANES, cap), 0).astype(F32) == lstar
    g_ref[0] = jnp.sum(jnp.where(lane_oh, asel, 0.0), axis=0, keepdims=True)


def _route(lt, cap):
    e, nchunk, _ = lt.shape
    return pl.pallas_call(
        functools.partial(_route_kernel, cap=cap),
        grid=(e,),
        in_specs=[pl.BlockSpec((e, nchunk, LANES), lambda ei: (0, 0, 0))],
        out_specs=[pl.BlockSpec((1, 1, cap), lambda ei: (ei, 0, 0)),
                   pl.BlockSpec((1, 1, cap), lambda ei: (ei, 0, 0)),
                   pl.BlockSpec((1, nchunk, LANES), lambda ei: (ei, 0, 0)),
                   pl.BlockSpec((1, nchunk, LANES), lambda ei: (ei, 0, 0))],
        out_shape=[jax.ShapeDtypeStruct((e, 1, cap), jnp.int32),
                   jax.ShapeDtypeStruct((e, 1, cap), F32),
                   jax.ShapeDtypeStruct((e, nchunk, LANES), jnp.int32),
                   jax.ShapeDtypeStruct((e, nchunk, LANES), jnp.int32)],
        scratch_shapes=[pltpu.VMEM((e, nchunk, LANES), F32),
                        pltpu.VMEM((e, SUBLANES, LANES), F32)],
        compiler_params=_cparams(("arbitrary",)),
        name="route",
    )(lt)


COMB_TOK = 256
COMB_WIN = 64
COMB_ALIGN = 16


def _combine_kernel(tab_ref, startv_ref, pos_ref, x1_ref, gate_ref, ye_ref, o_ref,
                    ybuf, ybuf_x, sems, sem_x, *, cap):
    k = pl.program_id(0)
    nk = pl.num_programs(0)
    slot = lax.rem(k, 2)
    ne = ye_ref.shape[0]
    last_start = cap - COMB_WIN

    def window_copy(e, logical_start, dst, sem):
        row0 = pl.multiple_of(jnp.minimum(logical_start, last_start), COMB_ALIGN)
        return pltpu.make_async_copy(ye_ref.at[e, pl.ds(row0, COMB_WIN), :],
                                     dst.at[pl.ds(e * COMB_WIN, COMB_WIN), :], sem)

    def issue(tile, sl):
        for e in range(ne):
            window_copy(e, tab_ref[tile, e], ybuf.at[sl], sems.at[sl]).start()

    @pl.when(k == 0)
    def _():
        issue(k, slot)

    @pl.when(k + 1 < nk)
    def _():
        issue(k + 1, 1 - slot)

    pltpu.make_async_copy(ybuf.at[slot], ybuf.at[slot], sems.at[slot]).wait()

    pos = pos_ref[0]
    start0 = startv_ref[0]
    width = ne * COMB_WIN
    brow = lax.broadcasted_iota(jnp.int32, (width, ne), 0)
    expand = (lax.shift_right_logical(brow, int(math.log2(COMB_WIN)))
              == lax.broadcasted_iota(jnp.int32, (width, ne), 1)).astype(BF16)
    row_in_win = lax.bitwise_and(lax.broadcasted_iota(jnp.int32, (width, 1), 0),
                                 COMB_WIN - 1).astype(F32)

    def placed_sum(p, rows_bf16):
        logical = start0 + p * COMB_WIN
        rel = pos - logical
        inside = (pos >= 0) & (rel >= 0) & (rel < COMB_WIN)
        row = jnp.where(inside, pos - jnp.minimum(logical, last_start), -1)
        spread = jnp.dot(expand, row.astype(F32).astype(BF16), preferred_element_type=F32)
        onehot = (spread == row_in_win).astype(BF16)
        return lax.dot_general(onehot, rows_bf16, (((0,), (0,)), ((), ())),
                               preferred_element_type=F32)

    gate = gate_ref[0]
    o_ref[...] = x1_ref[...] + gate * placed_sum(0, ybuf[slot])

    def extra_pass(p, carry):
        for e in range(ne):
            window_copy(e, tab_ref[k, e] + p * COMB_WIN, ybuf_x, sem_x.at[0]).start()
        pltpu.make_async_copy(ybuf_x, ybuf_x, sem_x.at[0]).wait()
        o_ref[...] += gate * placed_sum(p, ybuf_x[...])
        return carry

    lax.fori_loop(1, tab_ref[k, ne], extra_pass, 0)


def _combine(pos, offs, ye, x1, mod3, *, t, boff):
    ne, cap, d = ye.shape
    n = x1.shape[0]
    ntiles = n // COMB_TOK
    chunks_per_tile = COMB_TOK // LANES
    before = offs[:, ::chunks_per_tile, 0].T
    after = jnp.concatenate([before[1:], jnp.full((1, ne), cap, jnp.int32)], axis=0)
    starts = before - before % COMB_ALIGN
    passes = jnp.maximum(1, (jnp.max(after - starts, axis=1) + COMB_WIN - 1) // COMB_WIN)
    table = jnp.concatenate([starts, passes[:, None]], axis=1).astype(jnp.int32)
    startv = starts.astype(jnp.int32).reshape(ntiles, ne, 1)
    pos_tiles = pos.reshape(ne, ntiles, COMB_TOK).transpose(1, 0, 2)
    tiles_per_seq = t // COMB_TOK
    grid_spec = pltpu.PrefetchScalarGridSpec(
        num_scalar_prefetch=1,
        grid=(ntiles,),
        in_specs=[
            pl.BlockSpec((1, ne, 1), lambda k, tab: (k, 0, 0)),
            pl.BlockSpec((1, ne, COMB_TOK), lambda k, tab: (k, 0, 0)),
            pl.BlockSpec((COMB_TOK, d), lambda k, tab: (k, 0)),
            pl.BlockSpec((1, 1, d), lambda k, tab: ((k // tiles_per_seq + boff) * 6 + 5, 0, 0)),
            pl.BlockSpec(memory_space=pl.ANY),
        ],
        out_specs=pl.BlockSpec((COMB_TOK, d), lambda k, tab: (k, 0)),
        scratch_shapes=[pltpu.VMEM((2, ne * COMB_WIN, d), BF16),
                        pltpu.VMEM((ne * COMB_WIN, d), BF16),
                        pltpu.SemaphoreType.DMA((2,)),
                        pltpu.SemaphoreType.DMA((1,))],
    )
    return pl.pallas_call(
        functools.partial(_combine_kernel, cap=cap),
        grid_spec=grid_spec,
        out_shape=jax.ShapeDtypeStruct(x1.shape, x1.dtype),
        compiler_params=_cparams(("arbitrary",)),
        name="combine",
    )(table, startv, pos_tiles, x1, mod3, ye)


def _rope_tables(t):
    half = HEAD_DIM // 2
    quarter = half // 2
    freqs = ROPE_THETA ** (-(jnp.arange(quarter, dtype=F32) / quarter))
    rows = t // GRID_W
    row_ids = jnp.repeat(jnp.arange(rows, dtype=jnp.int32), GRID_W).astype(F32)
    col_ids = jnp.tile(jnp.arange(GRID_W, dtype=jnp.int32), rows).astype(F32)
    ang_r = row_ids[:, None] * freqs[None, :]
    ang_c = col_ids[:, None] * freqs[None, :]
    z = jnp.zeros_like(ang_r)
    cos_t = jnp.concatenate([jnp.cos(ang_r)] * 2 + [jnp.cos(ang_c)] * 2, axis=1)
    sa_t = jnp.concatenate([-jnp.sin(ang_r), z, -jnp.sin(ang_c), z], axis=1)
    sb_t = jnp.concatenate([z, jnp.sin(ang_r), z, jnp.sin(ang_c)], axis=1)
    return cos_t, sa_t, sb_t


def kernel(x_prompt, x_sample, c_prompt, c_sample, rel_bias_table, w_ada, b_ada, g_norm_mix, g_norm_ffn, w_in, g_q_a, g_k_a, g_q_b, g_k_b, g_out_a, g_out_b, w_out, w_router, w_gate, w_up, w_down):
    d = D_MODEL
    nbp, nbs = c_prompt.shape[0], c_sample.shape[0]
    c_all = jnp.concatenate([c_prompt, c_sample, jnp.zeros((16 - nbp - nbs, d), F32)], axis=0)
    mod = _ada(c_all, w_ada[0], b_ada[0])
    mod3 = mod.reshape(16 * 6, 1, d)

    scale = HEAD_DIM ** -0.5
    gains = jnp.stack([g_q_a[0] * scale, g_k_a[0], g_q_b[0] * scale, g_k_b[0]], axis=0)
    w_in_bf = w_in[0].astype(BF16)
    w_out_bf = w_out[0].astype(BF16)
    w_router_pad = jnp.pad(w_router[0], ((0, 0), (0, HEAD_DIM - N_EXPERTS))).astype(BF16)
    wg, wu, wd = w_gate[0], w_up[0], w_down[0]
    bias_tiles = _bias_tiles(rel_bias_table)
    gn1 = g_norm_mix[0].reshape(1, d)
    gn2 = g_norm_ffn[0].reshape(1, d)
    ga = g_out_a[0].reshape(1, WIDTH_A)
    gb = g_out_b[0].reshape(1, WIDTH_B)

    def run(x, boff):
        b, t, _ = x.shape
        cos_t, sa_t, sb_t = _rope_tables(t)
        proj = _inproj(x, mod3, boff, gn1, w_in_bf, gains, cos_t, sa_t, sb_t)
        oa = _attn_a(proj, bias_tiles)
        ob = _attn_b(proj)
        x1, h2, logits = _outproj(oa, ob, x, mod3, boff, ga, gb, gn2, w_out_bf, w_router_pad)
        n = b * t
        cap = EC_CAPACITY_FACTOR * n // N_EXPERTS
        lt = logits.reshape(n, -1)[:, :N_EXPERTS].T.reshape(N_EXPERTS, n // LANES, LANES)
        idx, g, pos, offs = _route(lt, cap)
        ye = _ffn(idx.reshape(N_EXPERTS, cap), h2, wg, wu, wd, g.reshape(N_EXPERTS, cap))
        out = _combine(pos, offs, ye, x1.reshape(n, d), mod3, t=t, boff=boff)
        return out.reshape(b, t, d)

    return (run(x_prompt, 0), run(x_sample, nbp))
```

```python
import functools
import math

import jax
import jax.numpy as jnp
import numpy as np
from jax import lax
from jax.experimental import pallas as pl
from jax.experimental.pallas import tpu as pltpu

F32 = jnp.float32
BF16 = jnp.bfloat16

D_MODEL = 2048
HEAD_DIM = 128
N_HEADS_A = 8
N_HEADS_B = 8
N_KV_B = 2
GQA_GROUP = N_HEADS_B // N_KV_B
WIDTH_A = N_HEADS_A * HEAD_DIM
WIDTH_B = N_HEADS_B * HEAD_DIM
KV_WIDTH_B = N_KV_B * HEAD_DIM
IN_COLS = 3 * WIDTH_A + WIDTH_B + 2 * KV_WIDTH_B
DIL_CONFIGS = ((128, 1), (512, 4), (2048, 16))
NUM_BUCKETS = 32
MAX_DISTANCE = 1024
GRID_W = 64
ROPE_THETA = 10000.0
N_EXPERTS = 16
EC_CAPACITY_FACTOR = 2
D_EXPERT = 2048
EPS = 1e-6
NEG_INF = -1e30

VMEM_LIMIT_V7X = 56 * 1024 * 1024
LANES = 128
SUBLANES = 8
TOK_PITCH = 20

COL_QA, COL_KA, COL_VA = 0, 8, 16
COL_QB, COL_KB, COL_VB = 24, 32, 34

IN_TN = 512
IN_CHAINS = 4
OUT_CHAINS = 2
A_QB = 128
A_KB = 256
A_RADIUS = 64
A_PAD = A_RADIUS * 16
A_UNROLL = 16
B_SUB = 128
B_SCORE_ELEMS = 8 * 1024 * 1024


def _cparams(sem):
    return pltpu.CompilerParams(dimension_semantics=sem, vmem_limit_bytes=VMEM_LIMIT_V7X)


def _ada_kernel(c_ref, w_ref, b_ref, o_ref):
    c = c_ref[...]
    s = c * (1.0 / (1.0 + jnp.exp(-c)))
    o_ref[...] = jnp.dot(s.astype(BF16), w_ref[...].astype(BF16),
                         preferred_element_type=F32) + b_ref[...]


def _ada(c_all, w_ada, b_ada):
    rows, d = c_all.shape
    n = w_ada.shape[1]
    tn = 1024
    return pl.pallas_call(
        _ada_kernel,
        grid=(n // tn,),
        in_specs=[pl.BlockSpec((rows, d), lambda j: (0, 0)),
                  pl.BlockSpec((d, tn), lambda j: (0, j)),
                  pl.BlockSpec((1, tn), lambda j: (0, j))],
        out_specs=pl.BlockSpec((rows, tn), lambda j: (0, j)),
        out_shape=jax.ShapeDtypeStruct((rows, n), F32),
        compiler_params=_cparams(("arbitrary",)),
        name="ada_mod",
    )(c_all, w_ada, b_ada.reshape(1, n))


def _head_norm(a, g):
    ms = jnp.mean(a * a, axis=-1, keepdims=True)
    return a * lax.rsqrt(ms + EPS) * g


def _inproj_kernel(x_ref, sc_ref, sh_ref, gn_ref, w_ref, gains_ref, cos_ref, sa_ref, sb_ref,
                   o_ref, h_scr):
    j = pl.program_id(2)

    @pl.when(j == 0)
    def _():
        x = x_ref[0]
        ms = jnp.mean(x * x, axis=-1, keepdims=True)
        y = x * lax.rsqrt(ms + EPS) * gn_ref[...]
        h_scr[...] = (y * (1.0 + sc_ref[0]) + sh_ref[0]).astype(BF16)

    def rope(a, rows):
        return (a * cos_ref[rows, :] + pltpu.roll(a, 96, 1) * sa_ref[rows, :]
                + pltpu.roll(a, 32, 1) * sb_ref[rows, :])

    def plain(a, rows):
        return a

    def norm(g_row):
        return lambda a, rows: _head_norm(a, gains_ref[g_row:g_row + 1, :])

    def norm_rope(g_row):
        return lambda a, rows: rope(_head_norm(a, gains_ref[g_row:g_row + 1, :]), rows)

    def tile(head_fns):
        sub = h_scr.shape[0] // IN_CHAINS
        for c in range(IN_CHAINS):
            rows = slice(c * sub, (c + 1) * sub)
            acc = jnp.dot(h_scr[rows, :], w_ref[...], preferred_element_type=F32)
            for hh, fn in enumerate(head_fns):
                sl = slice(hh * HEAD_DIM, (hh + 1) * HEAD_DIM)
                o_ref[0, rows, sl] = fn(acc[:, sl], rows).astype(BF16)

    @pl.when(j < 2)
    def _():
        tile([norm(0)] * 4)

    @pl.when((j >= 2) & (j < 4))
    def _():
        tile([norm(1)] * 4)

    @pl.when((j >= 4) & (j < 6))
    def _():
        tile([plain] * 4)

    @pl.when((j >= 6) & (j < 8))
    def _():
        tile([norm_rope(2)] * 4)

    @pl.when(j == 8)
    def _():
        tile([norm_rope(3)] * 2 + [plain] * 2)


def _inproj(x, mod3, boff, g_norm, w_in_bf, gains, cos_t, sa_t, sb_t):
    b, t, d = x.shape
    tm = 1024
    nj = IN_COLS // IN_TN
    return pl.pallas_call(
        _inproj_kernel,
        grid=(b, t // tm, nj),
        in_specs=[
            pl.BlockSpec((1, tm, d), lambda bi, ti, j: (bi, ti, 0)),
            pl.BlockSpec((1, 1, d), lambda bi, ti, j: ((bi + boff) * 6 + 1, 0, 0)),
            pl.BlockSpec((1, 1, d), lambda bi, ti, j: ((bi + boff) * 6 + 0, 0, 0)),
            pl.BlockSpec((1, d), lambda bi, ti, j: (0, 0)),
            pl.BlockSpec((d, IN_TN), lambda bi, ti, j: (0, j)),
            pl.BlockSpec((4, HEAD_DIM), lambda bi, ti, j: (0, 0)),
            pl.BlockSpec((tm, HEAD_DIM), lambda bi, ti, j: (ti, 0)),
            pl.BlockSpec((tm, HEAD_DIM), lambda bi, ti, j: (ti, 0)),
            pl.BlockSpec((tm, HEAD_DIM), lambda bi, ti, j: (ti, 0)),
        ],
        out_specs=pl.BlockSpec((1, tm, IN_TN), lambda bi, ti, j: (bi, ti, j)),
        out_shape=jax.ShapeDtypeStruct((b, t, IN_COLS), BF16),
        scratch_shapes=[pltpu.VMEM((tm, d), BF16)],
        compiler_params=_cparams(("arbitrary", "arbitrary", "arbitrary")),
        name="inproj",
    )(x, mod3, mod3, g_norm, w_in_bf, gains, cos_t, sa_t, sb_t)


def _attn_a_kernel(q_ref, k_ref, v_ref, bias_ref, o_ref, qf, kf, vf, acc, mm, ll, *, t):
    zpad = jnp.zeros((A_PAD, HEAD_DIM), F32)
    kf[0:A_PAD, :] = zpad
    vf[0:A_PAD, :] = zpad
    kf[A_PAD + t:A_PAD + t + A_PAD, :] = zpad
    vf[A_PAD + t:A_PAD + t + A_PAD, :] = zpad
    kf[A_PAD:A_PAD + t, :] = k_ref[0].astype(F32)
    vf[A_PAD:A_PAD + t, :] = v_ref[0].astype(F32)
    qf[...] = q_ref[0].astype(F32)
    ones = jnp.ones((A_KB, HEAD_DIM), BF16)

    for bi, (_, dil) in enumerate(DIL_CONFIGS):
        sub_len = t // dil
        nmb = sub_len // A_QB
        shift = int(math.log2(nmb))

        def rows(start, size, dil=dil):
            return pl.ds(start, size) if dil == 1 else pl.ds(start, size, stride=dil)

        def body(idx, carry, bi=bi, dil=dil, sub_len=sub_len, nmb=nmb, shift=shift, rows=rows):
            rho = lax.shift_right_logical(idx, shift)
            mb = lax.bitwise_and(idx, nmb - 1)
            qstart = rho + mb * (A_QB * dil)
            kstart = A_PAD + qstart - A_RADIUS * dil
            q = qf[rows(qstart, A_QB), :].astype(BF16)
            k = kf[rows(kstart, A_KB), :].astype(BF16)
            v = vf[rows(kstart, A_KB), :].astype(BF16)
            s = lax.dot_general(q, k, (((1,), (1,)), ((), ())), preferred_element_type=F32)
            s = s + bias_ref[bi, 0]
            kidx = mb * A_QB - A_RADIUS + lax.broadcasted_iota(jnp.int32, (1, A_KB), 1)
            s = jnp.where((kidx >= 0) & (kidx < sub_len), s, NEG_INF)
            mblk = jnp.max(s, axis=-1, keepdims=True)
            v1 = jnp.concatenate([v, ones], axis=1)
            p = jnp.exp(s - mblk).astype(BF16)
            pv = jnp.dot(p, v1, preferred_element_type=F32)
            acc[bi, rows(qstart, A_QB), :] = pv[:, :HEAD_DIM]
            ll[bi, rows(qstart, A_QB), :] = pv[:, HEAD_DIM:]
            mm[bi, rows(qstart, A_QB), :] = jnp.broadcast_to(mblk, (A_QB, HEAD_DIM))
            return carry

        lax.fori_loop(0, dil * nmb, body, 0, unroll=A_UNROLL)

    m0, m1, m2 = mm[0], mm[1], mm[2]
    mtop = jnp.maximum(m0, jnp.maximum(m1, m2))
    w0, w1, w2 = jnp.exp(m0 - mtop), jnp.exp(m1 - mtop), jnp.exp(m2 - mtop)
    num = w0 * acc[0] + w1 * acc[1] + w2 * acc[2]
    den = w0 * ll[0] + w1 * ll[1] + w2 * ll[2]
    o_ref[0] = (num / den).astype(BF16)


def _attn_a(proj, bias_tiles):
    b, t, _ = proj.shape
    kern = functools.partial(_attn_a_kernel, t=t)
    return pl.pallas_call(
        kern,
        grid=(b, N_HEADS_A),
        in_specs=[
            pl.BlockSpec((1, t, HEAD_DIM), lambda bi, h: (bi, 0, COL_QA + h)),
            pl.BlockSpec((1, t, HEAD_DIM), lambda bi, h: (bi, 0, COL_KA + h)),
            pl.BlockSpec((1, t, HEAD_DIM), lambda bi, h: (bi, 0, COL_VA + h)),
            pl.BlockSpec((3, 1, A_QB, A_KB), lambda bi, h: (0, h, 0, 0)),
        ],
        out_specs=pl.BlockSpec((1, t, HEAD_DIM), lambda bi, h: (bi, 0, h)),
        out_shape=jax.ShapeDtypeStruct((b, t, WIDTH_A), BF16),
        scratch_shapes=[
            pltpu.VMEM((t, HEAD_DIM), F32),
            pltpu.VMEM((t + 2 * A_PAD, HEAD_DIM), F32),
            pltpu.VMEM((t + 2 * A_PAD, HEAD_DIM), F32),
            pltpu.VMEM((len(DIL_CONFIGS), t, HEAD_DIM), F32),
            pltpu.VMEM((len(DIL_CONFIGS), t, HEAD_DIM), F32),
            pltpu.VMEM((len(DIL_CONFIGS), t, HEAD_DIM), F32),
        ],
        compiler_params=_cparams(("arbitrary", "arbitrary")),
        name="attn_dilated",
    )(proj, proj, proj, bias_tiles)


def _t5_bucket(rel):
    nb = NUM_BUCKETS // 2
    max_exact = nb // 2
    sign_off = np.where(rel > 0, nb, 0)
    n = np.abs(rel)
    nf = np.maximum(n, 1).astype(np.float32)
    large = max_exact + (np.log(nf / np.float32(max_exact))
                         / np.float32(math.log(MAX_DISTANCE / max_exact))
                         * np.float32(nb - max_exact)).astype(np.int32)
    large = np.minimum(large, nb - 1)
    return (sign_off + np.where(n < max_exact, n, large)).astype(np.int32)


def _bias_tiles(rel_bias_table):
    qi = np.arange(A_QB, dtype=np.int32)[:, None]
    kj = np.arange(A_KB, dtype=np.int32)[None, :]
    rel = kj - A_RADIUS - qi
    in_band = jnp.asarray(np.abs(rel) <= A_RADIUS)
    tiles = []
    for _, dil in DIL_CONFIGS:
        onehot = jnp.asarray(_t5_bucket(rel * dil)[..., None]
                             == np.arange(NUM_BUCKETS, dtype=np.int32), F32)
        bias = jnp.einsum('qkn,nh->hqk', onehot, rel_bias_table.astype(F32),
                          precision=lax.Precision.HIGHEST)
        tiles.append(jnp.where(in_band[None], bias, NEG_INF))
    return jnp.stack(tiles, axis=0)


def _attn_b_kernel(q_ref, k_ref, v_ref, o_ref, v1_scr, *, tq):
    @pl.when(pl.program_id(2) == 0)
    def _():
        v1_scr[:, 0:HEAD_DIM] = v_ref[0]
        v1_scr[:, HEAD_DIM:2 * HEAD_DIM] = jnp.ones(v_ref.shape[1:], BF16)

    for c in range(tq // B_SUB):
        q = q_ref[0, c * B_SUB:(c + 1) * B_SUB, :]
        qs = jnp.concatenate([q[:, i * HEAD_DIM:(i + 1) * HEAD_DIM] for i in range(GQA_GROUP)],
                             axis=0)
        s = lax.dot_general(qs, k_ref[0], (((1,), (1,)), ((), ())), preferred_element_type=F32)
        m = jnp.max(s, axis=-1, keepdims=True)
        p = jnp.exp(s - m).astype(BF16)
        pv = jnp.dot(p, v1_scr[...], preferred_element_type=F32)
        o = pv[:, :HEAD_DIM] / pv[:, HEAD_DIM:]
        o_ref[0, c * B_SUB:(c + 1) * B_SUB, :] = jnp.concatenate(
            [o[i * B_SUB:(i + 1) * B_SUB] for i in range(GQA_GROUP)], axis=1).astype(BF16)


def _attn_b(proj):
    b, t, _ = proj.shape
    tq = B_SUB * max(1, B_SCORE_ELEMS // (GQA_GROUP * B_SUB * t))
    gw = GQA_GROUP * HEAD_DIM
    return pl.pallas_call(
        functools.partial(_attn_b_kernel, tq=tq),
        grid=(b, N_KV_B, t // tq),
        in_specs=[
            pl.BlockSpec((1, tq, gw), lambda bi, g, qi: (bi, qi, COL_QB // GQA_GROUP + g)),
            pl.BlockSpec((1, t, HEAD_DIM), lambda bi, g, qi: (bi, 0, COL_KB + g)),
            pl.BlockSpec((1, t, HEAD_DIM), lambda bi, g, qi: (bi, 0, COL_VB + g)),
        ],
        out_specs=pl.BlockSpec((1, tq, gw), lambda bi, g, qi: (bi, qi, g)),
        out_shape=jax.ShapeDtypeStruct((b, t, WIDTH_B), BF16),
        scratch_shapes=[pltpu.VMEM((t, 2 * HEAD_DIM), BF16)],
        compiler_params=_cparams(("arbitrary", "arbitrary", "arbitrary")),
        name="attn_gqa",
    )(proj, proj, proj)


def _outproj_kernel(oa_ref, ob_ref, x_ref, gt_ref, sc_ref, sh_ref, ga_ref, gb_ref, gn_ref,
                    w_ref, wr_ref, x1_ref, h2_ref, lg_ref):
    def wide_norm(o, g_ref_):
        o = o.astype(F32)
        ms = jnp.mean(o * o, axis=-1, keepdims=True)
        return (o * lax.rsqrt(ms + EPS) * g_ref_[...]).astype(BF16)

    tm, dd = x_ref.shape[1:]
    slab = dd // LANES
    sub = tm // OUT_CHAINS
    for c in range(OUT_CHAINS):
        rows = slice(c * sub, (c + 1) * sub)
        na = wide_norm(oa_ref[0, rows, :], ga_ref)
        nb = wide_norm(ob_ref[0, rows, :], gb_ref)
        mix = (jnp.dot(na, w_ref[0:WIDTH_A, :], preferred_element_type=F32)
               + jnp.dot(nb, w_ref[WIDTH_A:WIDTH_A + WIDTH_B, :], preferred_element_type=F32))
        x1 = x_ref[0, rows, :] + gt_ref[0] * mix
        x1_ref[0, rows, :] = x1
        ms = jnp.mean(x1 * x1, axis=-1, keepdims=True)
        h2 = (x1 * lax.rsqrt(ms + EPS) * gn_ref[...]) * (1.0 + sc_ref[0]) + sh_ref[0]
        lg_ref[0, rows, :] = jnp.dot(h2.astype(BF16), wr_ref[...], preferred_element_type=F32)
        for s in range(TOK_PITCH):
            val = h2[:, s * LANES:(s + 1) * LANES] if s < slab else jnp.zeros((sub, LANES), F32)
            h2_ref[pl.ds(c * sub * TOK_PITCH + s, sub, stride=TOK_PITCH), :] = val


def _outproj(oa, ob, x, mod3, boff, g_out_a, g_out_b, g_norm_ffn, w_out_bf, w_router_pad):
    b, t, d = x.shape
    tm = 512
    nr = w_router_pad.shape[1]
    assert d // LANES <= TOK_PITCH
    row = lambda k: (lambda bi, ti: ((bi + boff) * 6 + k, 0, 0))
    return pl.pallas_call(
        _outproj_kernel,
        grid=(b, t // tm),
        in_specs=[
            pl.BlockSpec((1, tm, WIDTH_A), lambda bi, ti: (bi, ti, 0)),
            pl.BlockSpec((1, tm, WIDTH_B), lambda bi, ti: (bi, ti, 0)),
            pl.BlockSpec((1, tm, d), lambda bi, ti: (bi, ti, 0)),
            pl.BlockSpec((1, 1, d), row(2)),
            pl.BlockSpec((1, 1, d), row(4)),
            pl.BlockSpec((1, 1, d), row(3)),
            pl.BlockSpec((1, WIDTH_A), lambda bi, ti: (0, 0)),
            pl.BlockSpec((1, WIDTH_B), lambda bi, ti: (0, 0)),
            pl.BlockSpec((1, d), lambda bi, ti: (0, 0)),
            pl.BlockSpec((WIDTH_A + WIDTH_B, d), lambda bi, ti: (0, 0)),
            pl.BlockSpec((d, nr), lambda bi, ti: (0, 0)),
        ],
        out_specs=[
            pl.BlockSpec((1, tm, d), lambda bi, ti: (bi, ti, 0)),
            pl.BlockSpec((tm * TOK_PITCH, LANES), lambda bi, ti: (bi * (t // tm) + ti, 0)),
            pl.BlockSpec((1, tm, nr), lambda bi, ti: (bi, ti, 0)),
        ],
        out_shape=[jax.ShapeDtypeStruct((b, t, d), F32),
                   jax.ShapeDtypeStruct((b * t * TOK_PITCH, LANES), F32),
                   jax.ShapeDtypeStruct((b, t, nr), F32)],
        compiler_params=_cparams(("arbitrary", "arbitrary")),
        name="outproj",
    )(oa, ob, x, mod3, mod3, mod3, g_out_a, g_out_b, g_norm_ffn, w_out_bf, w_router_pad)


FFN_OUT_CHUNK = 512


def _ffn_kernel(idx_ref, idx_next_ref, h_ref, wg_ref, wu_ref, wd_ref, g_ref, o_ref,
                xslab, xb, hm, wdb, sem, *, tm, slab):
    f = pl.program_id(2)
    k = pl.program_id(0) * pl.num_programs(1) + pl.program_id(1)
    nk = pl.num_programs(0) * pl.num_programs(1)

    def issue_gather(ids_ref):
        def body(r, carry):
            pltpu.make_async_copy(h_ref.at[pl.ds(ids_ref[0, 0, r] * TOK_PITCH, slab), :],
                                  xslab.at[pl.ds(r * TOK_PITCH, slab), :], sem.at[0]).start()
            return carry
        lax.fori_loop(0, tm, body, 0, unroll=8)

    @pl.when(f == 0)
    def _():
        @pl.when(k == 0)
        def _():
            issue_gather(idx_ref)

        pltpu.make_async_copy(h_ref.at[pl.ds(0, tm * slab), :], xslab.at[pl.ds(0, tm * slab), :],
                              sem.at[0]).wait()
        for s in range(slab):
            xb[:, s * LANES:(s + 1) * LANES] = xslab[pl.ds(s, tm, stride=TOK_PITCH), :].astype(BF16)

        @pl.when(k + 1 < nk)
        def _():
            issue_gather(idx_next_ref)

    @pl.when(pl.program_id(1) == 0)
    def _():
        wdb[f] = wd_ref[0].astype(BF16)

    x = xb[...]
    a = jnp.dot(x, wg_ref[0].astype(BF16), preferred_element_type=F32)
    u = jnp.dot(x, wu_ref[0].astype(BF16), preferred_element_type=F32)
    hm[f] = (a * (1.0 / (1.0 + jnp.exp(-a))) * u).astype(BF16)

    @pl.when(f == pl.num_programs(2) - 1)
    def _():
        nf = hm.shape[0]
        for c in range(o_ref.shape[2] // FFN_OUT_CHUNK):
            cols = slice(c * FFN_OUT_CHUNK, (c + 1) * FFN_OUT_CHUNK)
            y = jnp.dot(hm[0], wdb[0, :, cols], preferred_element_type=F32)
            for j in range(1, nf):
                y += jnp.dot(hm[j], wdb[j, :, cols], preferred_element_type=F32)
            o_ref[0, :, cols] = (y * g_ref[0]).astype(o_ref.dtype)


def _ffn(idx, h2slab, wg, wu, wd, g):
    e, cap = idx.shape
    d = wg.shape[1]
    slab = d // LANES
    fdim = wg.shape[2]
    tm = min(cap, 1024)
    tf = 256
    nf = fdim // tf
    mt = cap // tm
    nk = e * mt
    idx3 = idx.reshape(nk, 1, tm)
    return pl.pallas_call(
        functools.partial(_ffn_kernel, tm=tm, slab=slab),
        grid=(e, mt, fdim // tf),
        in_specs=[
            pl.BlockSpec((1, 1, tm), lambda ei, mi, fi: (ei * mt + mi, 0, 0),
                         memory_space=pltpu.SMEM),
            pl.BlockSpec((1, 1, tm), lambda ei, mi, fi: (jnp.minimum(ei * mt + mi + 1, nk - 1), 0, 0),
                         memory_space=pltpu.SMEM),
            pl.BlockSpec(memory_space=pl.ANY),
            pl.BlockSpec((1, d, tf), lambda ei, mi, fi: (ei, 0, fi)),
            pl.BlockSpec((1, d, tf), lambda ei, mi, fi: (ei, 0, fi)),
            pl.BlockSpec((1, tf, d), lambda ei, mi, fi: (ei, jnp.where(mi == 0, fi, nf - 1), 0)),
            pl.BlockSpec((1, tm, 1), lambda ei, mi, fi: (ei, mi, 0)),
        ],
        out_specs=pl.BlockSpec((1, tm, d), lambda ei, mi, fi: (ei, mi, 0)),
        out_shape=jax.ShapeDtypeStruct((e, cap, d), BF16),
        scratch_shapes=[pltpu.VMEM((tm * TOK_PITCH, LANES), F32),
                        pltpu.VMEM((tm, d), BF16),
                        pltpu.VMEM((nf, tm, tf), BF16),
                        pltpu.VMEM((nf, tf, d), BF16),
                        pltpu.SemaphoreType.DMA((1,))],
        compiler_params=_cparams(("arbitrary", "arbitrary", "arbitrary")),
        name="expert_ffn",
    )(idx3, idx3, h2slab, wg, wu, wd, g.reshape(e, cap, 1))


def _prefix_counts(mask_f32, upper, lower):
    within = jnp.dot(mask_f32.astype(BF16), upper, preferred_element_type=F32)
    tot = jnp.broadcast_to(within[:, LANES - 1:LANES], within.shape)
    offs = jnp.dot(lower, tot.astype(BF16), preferred_element_type=F32)
    return within, offs


def _route_kernel(lt_ref, idx_ref, g_ref, pos_ref, offs_ref, aff_scr, thr_scr, *, cap):
    e = pl.program_id(0)
    nchunk = lt_ref.shape[1]
    capf = jnp.float32(cap)

    @pl.when(e == 0)
    def _():
        l = lt_ref[...]
        ex = jnp.exp(l - jnp.max(l, axis=0, keepdims=True))
        aff = ex / jnp.sum(ex, axis=0, keepdims=True)
        aff_scr[...] = aff

        def bit_step(i, thr_bits):
            cand = thr_bits | lax.shift_left(jnp.int32(1), 30 - i)
            cnt = jnp.sum((aff >= lax.bitcast_convert_type(cand, F32)).astype(F32),
                          axis=(1, 2), keepdims=True)
            return jnp.where(cnt >= capf, cand, thr_bits)

        thr_bits = lax.fori_loop(0, 31, bit_step, jnp.zeros((lt_ref.shape[0], 1, 1), jnp.int32))
        thr_scr[...] = jnp.broadcast_to(lax.bitcast_convert_type(thr_bits, F32), thr_scr.shape)

    a = aff_scr[e]
    thr = thr_scr[e][0:1, 0:1]

    def count(m):
        return jnp.sum(m.astype(F32), axis=(0, 1), keepdims=True)

    ri = lax.broadcasted_iota(jnp.int32, (LANES, LANES), 0)
    ci = lax.broadcasted_iota(jnp.int32, (LANES, LANES), 1)
    upper = (ri <= ci).astype(BF16)
    rc = lax.broadcasted_iota(jnp.int32, (nchunk, nchunk), 0)
    cc = lax.broadcasted_iota(jnp.int32, (nchunk, nchunk), 1)
    lower = (cc < rc).astype(BF16)

    gt = a > thr
    eq = a == thr
    need = capf - count(gt)
    eq_within, eq_offs = _prefix_counts(eq.astype(F32), upper, lower)
    sel = gt | (eq & ((eq_within + eq_offs) <= need))
    within, offs = _prefix_counts(sel.astype(F32), upper, lower)
    pos_ref[0] = jnp.where(sel, within + offs - 1.0, -1.0).astype(jnp.int32)
    offs_ref[0] = offs.astype(jnp.int32)

    offs_col = offs[:, 0:1]
    ends_col = offs_col + within[:, LANES - 1:LANES]
    slot = lax.broadcasted_iota(jnp.int32, (1, cap), 1).astype(F32)
    cstar = jnp.sum((ends_col <= slot).astype(F32), axis=0, keepdims=True)
    chunk_oh = lax.broadcasted_iota(jnp.int32, (nchunk, cap), 0).astype(F32) == cstar
    rank = slot - jnp.sum(jnp.where(chunk_oh, offs_col, 0.0), axis=0, keepdims=True)
    oh = chunk_oh.astype(BF16)
    tdot = functools.partial(lax.dot_general, dimension_numbers=(((0,), (0,)), ((), ())),
                             preferred_element_type=F32)
    wsel = tdot(within.astype(BF16), oh)
    lstar = jnp.sum((wsel <= rank).astype(F32), axis=0, keepdims=True)
    idx_ref[0] = (cstar * LANES + lstar).astype(jnp.int32)

    a1 = a.astype(BF16)
    r1 = a - a1.astype(F32)
    a2 = r1.astype(BF16)
    a3 = (r1 - a2.astype(F32)).astype(BF16)
    asel = (tdot(a1, oh) + tdot(a2, oh)) + tdot(a3, oh)
    lane_oh = lax.broadcasted_iota(jnp.int32, (LANES, cap), 0).astype(F32) == lstar
    g_ref[0] = jnp.sum(jnp.where(lane_oh, asel, 0.0), axis=0, keepdims=True)


def _route(lt, cap):
    e, nchunk, _ = lt.shape
    return pl.pallas_call(
        functools.partial(_route_kernel, cap=cap),
        grid=(e,),
        in_specs=[pl.BlockSpec((e, nchunk, LANES), lambda ei: (0, 0, 0))],
        out_specs=[pl.BlockSpec((1, 1, cap), lambda ei: (ei, 0, 0)),
                   pl.BlockSpec((1, 1, cap), lambda ei: (ei, 0, 0)),
                   pl.BlockSpec((1, nchunk, LANES), lambda ei: (ei, 0, 0)),
                   pl.BlockSpec((1, nchunk, LANES), lambda ei: (ei, 0, 0))],
        out_shape=[jax.ShapeDtypeStruct((e, 1, cap), jnp.int32),
                   jax.ShapeDtypeStruct((e, 1, cap), F32),
                   jax.ShapeDtypeStruct((e, nchunk, LANES), jnp.int32),
                   jax.ShapeDtypeStruct((e, nchunk, LANES), jnp.int32)],
        scratch_shapes=[pltpu.VMEM((e, nchunk, LANES), F32),
                        pltpu.VMEM((e, SUBLANES, LANES), F32)],
        compiler_params=_cparams(("arbitrary",)),
        name="route",
    )(lt)


COMB_TOK = 512
COMB_WIN = 128
COMB_ALIGN = 16


def _combine_kernel(tab_ref, startv_ref, pos_ref, x1_ref, gate_ref, ye_ref, o_ref,
                    ybuf, ybuf_x, sems, sem_x, *, cap):
    k = pl.program_id(0)
    nk = pl.num_programs(0)
    slot = lax.rem(k, 2)
    ne = ye_ref.shape[0]
    last_start = cap - COMB_WIN

    def window_copy(e, logical_start, dst, sem):
        row0 = pl.multiple_of(jnp.minimum(logical_start, last_start), COMB_ALIGN)
        return pltpu.make_async_copy(ye_ref.at[e, pl.ds(row0, COMB_WIN), :],
                                     dst.at[pl.ds(e * COMB_WIN, COMB_WIN), :], sem)

    def issue(tile, sl):
        for e in range(ne):
            window_copy(e, tab_ref[tile, e], ybuf.at[sl], sems.at[sl]).start()

    @pl.when(k == 0)
    def _():
        issue(k, slot)

    @pl.when(k + 1 < nk)
    def _():
        issue(k + 1, 1 - slot)

    pltpu.make_async_copy(ybuf.at[slot], ybuf.at[slot], sems.at[slot]).wait()

    pos = pos_ref[0]
    start0 = startv_ref[0]
    width = ne * COMB_WIN
    brow = lax.broadcasted_iota(jnp.int32, (width, ne), 0)
    expand = (lax.shift_right_logical(brow, int(math.log2(COMB_WIN)))
              == lax.broadcasted_iota(jnp.int32, (width, ne), 1)).astype(BF16)
    row_in_win = lax.bitwise_and(lax.broadcasted_iota(jnp.int32, (width, 1), 0),
                                 COMB_WIN - 1).astype(F32)

    def placed_sum(p, rows_bf16):
        logical = start0 + p * COMB_WIN
        rel = pos - logical
        inside = (pos >= 0) & (rel >= 0) & (rel < COMB_WIN)
        row = jnp.where(inside, pos - jnp.minimum(logical, last_start), -1)
        spread = jnp.dot(expand, row.astype(F32).astype(BF16), preferred_element_type=F32)
        onehot = (spread == row_in_win).astype(BF16)
        return lax.dot_general(onehot, rows_bf16, (((0,), (0,)), ((), ())),
                               preferred_element_type=F32)

    gate = gate_ref[0]
    o_ref[...] = x1_ref[...] + gate * placed_sum(0, ybuf[slot])

    def extra_pass(p, carry):
        for e in range(ne):
            window_copy(e, tab_ref[k, e] + p * COMB_WIN, ybuf_x, sem_x.at[0]).start()
        pltpu.make_async_copy(ybuf_x, ybuf_x, sem_x.at[0]).wait()
        o_ref[...] += gate * placed_sum(p, ybuf_x[...])
        return carry

    lax.fori_loop(1, tab_ref[k, ne], extra_pass, 0)


def _combine(pos, offs, ye, x1, mod3, *, t, boff):
    ne, cap, d = ye.shape
    n = x1.shape[0]
    ntiles = n // COMB_TOK
    chunks_per_tile = COMB_TOK // LANES
    before = offs[:, ::chunks_per_tile, 0].T
    after = jnp.concatenate([before[1:], jnp.full((1, ne), cap, jnp.int32)], axis=0)
    starts = before - before % COMB_ALIGN
    passes = jnp.maximum(1, (jnp.max(after - starts, axis=1) + COMB_WIN - 1) // COMB_WIN)
    table = jnp.concatenate([starts, passes[:, None]], axis=1).astype(jnp.int32)
    startv = starts.astype(jnp.int32).reshape(ntiles, ne, 1)
    pos_tiles = pos.reshape(ne, ntiles, COMB_TOK).transpose(1, 0, 2)
    tiles_per_seq = t // COMB_TOK
    grid_spec = pltpu.PrefetchScalarGridSpec(
        num_scalar_prefetch=1,
        grid=(ntiles,),
        in_specs=[
            pl.BlockSpec((1, ne, 1), lambda k, tab: (k, 0, 0)),
            pl.BlockSpec((1, ne, COMB_TOK), lambda k, tab: (k, 0, 0)),
            pl.BlockSpec((COMB_TOK, d), lambda k, tab: (k, 0)),
            pl.BlockSpec((1, 1, d), lambda k, tab: ((k // tiles_per_seq + boff) * 6 + 5, 0, 0)),
            pl.BlockSpec(memory_space=pl.ANY),
        ],
        out_specs=pl.BlockSpec((COMB_TOK, d), lambda k, tab: (k, 0)),
        scratch_shapes=[pltpu.VMEM((2, ne * COMB_WIN, d), BF16),
                        pltpu.VMEM((ne * COMB_WIN, d), BF16),
                        pltpu.SemaphoreType.DMA((2,)),
                        pltpu.SemaphoreType.DMA((1,))],
    )
    return pl.pallas_call(
        functools.partial(_combine_kernel, cap=cap),
        grid_spec=grid_spec,
        out_shape=jax.ShapeDtypeStruct(x1.shape, x1.dtype),
        compiler_params=_cparams(("arbitrary",)),
        name="combine",
    )(table, startv, pos_tiles, x1, mod3, ye)


def _rope_tables(t):
    half = HEAD_DIM // 2
    quarter = half // 2
    freqs = ROPE_THETA ** (-(jnp.arange(quarter, dtype=F32) / quarter))
    rows = t // GRID_W
    row_ids = jnp.repeat(jnp.arange(rows, dtype=jnp.int32), GRID_W).astype(F32)
    col_ids = jnp.tile(jnp.arange(GRID_W, dtype=jnp.int32), rows).astype(F32)
    ang_r = row_ids[:, None] * freqs[None, :]
    ang_c = col_ids[:, None] * freqs[None, :]
    z = jnp.zeros_like(ang_r)
    cos_t = jnp.concatenate([jnp.cos(ang_r)] * 2 + [jnp.cos(ang_c)] * 2, axis=1)
    sa_t = jnp.concatenate([-jnp.sin(ang_r), z, -jnp.sin(ang_c), z], axis=1)
    sb_t = jnp.concatenate([z, jnp.sin(ang_r), z, jnp.sin(ang_c)], axis=1)
    return cos_t, sa_t, sb_t


def kernel(x_prompt, x_sample, c_prompt, c_sample, rel_bias_table, w_ada, b_ada, g_norm_mix, g_norm_ffn, w_in, g_q_a, g_k_a, g_q_b, g_k_b, g_out_a, g_out_b, w_out, w_router, w_gate, w_up, w_down):
    d = D_MODEL
    nbp, nbs = c_prompt.shape[0], c_sample.shape[0]
    c_all = jnp.concatenate([c_prompt, c_sample, jnp.zeros((16 - nbp - nbs, d), F32)], axis=0)
    mod = _ada(c_all, w_ada[0], b_ada[0])
    mod3 = mod.reshape(16 * 6, 1, d)

    scale = HEAD_DIM ** -0.5
    gains = jnp.stack([g_q_a[0] * scale, g_k_a[0], g_q_b[0] * scale, g_k_b[0]], axis=0)
    w_in_bf = w_in[0].astype(BF16)
    w_out_bf = w_out[0].astype(BF16)
    w_router_pad = jnp.pad(w_router[0], ((0, 0), (0, HEAD_DIM - N_EXPERTS))).astype(BF16)
    wg, wu, wd = w_gate[0], w_up[0], w_down[0]
    bias_tiles = _bias_tiles(rel_bias_table)
    gn1 = g_norm_mix[0].reshape(1, d)
    gn2 = g_norm_ffn[0].reshape(1, d)
    ga = g_out_a[0].reshape(1, WIDTH_A)
    gb = g_out_b[0].reshape(1, WIDTH_B)

    def run(x, boff):
        b, t, _ = x.shape
        cos_t, sa_t, sb_t = _rope_tables(t)
        proj = _inproj(x, mod3, boff, gn1, w_in_bf, gains, cos_t, sa_t, sb_t)
        oa = _attn_a(proj, bias_tiles)
        ob = _attn_b(proj)
        x1, h2, logits = _outproj(oa, ob, x, mod3, boff, ga, gb, gn2, w_out_bf, w_router_pad)
        n = b * t
        cap = EC_CAPACITY_FACTOR * n // N_EXPERTS
        lt = logits.reshape(n, -1)[:, :N_EXPERTS].T.reshape(N_EXPERTS, n // LANES, LANES)
        idx, g, pos, offs = _route(lt, cap)
        ye = _ffn(idx.reshape(N_EXPERTS, cap), h2, wg, wu, wd, g.reshape(N_EXPERTS, cap))
        out = _combine(pos, offs, ye, x1.reshape(n, d), mod3, t=t, boff=boff)
        return out.reshape(b, t, d)

    return (run(x_prompt, 0), run(x_sample, nbp))
```
